```python
import jax, jax.numpy as jnp
from jax import lax
import numpy as np

D_MODEL = 1024
BATCH = 8
SEQ = 2048
DEPTH = 2
DEC_BATCH = 128
DEC_SEQ = 1
PAST_LEN = 2048
PAGE_SIZE = 128

HEAD_DIM = 64
MIX_WIDTH = D_MODEL
N_HEADS_TOTAL = MIX_WIDTH // HEAD_DIM
H_A = N_HEADS_TOTAL // 2
H_B = N_HEADS_TOTAL // 4
H_C = N_HEADS_TOTAL - H_A - H_B
KV_A = max(1, H_A // 4)
GQA_A = H_A // KV_A
W_A = H_A * HEAD_DIM
W_B = H_B * HEAD_DIM
W_C = H_C * HEAD_DIM
KVW_A = KV_A * HEAD_DIM
L_CMP = 32
L_SEL = 64
TOP_N = 8
WINDOW = 512
Q_BLOCK = 128
RET_CHUNK = 128
ROPE_THETA = 10000.0
EPS = 1e-6
ATT_SCALE = HEAD_DIM ** -0.5
RET_KEY_SCALE = HEAD_DIM ** -0.5
FORCE_BONUS = 1e3
FOX_BIAS_CENTER = 2.5
_SPLIT_SIZES = (W_A, KVW_A, KVW_A, KVW_A, KVW_A, KVW_A, KVW_A, 3 * H_A, W_A,
                W_B, W_B, W_B, H_B, W_B,
                W_C, W_C, W_C, W_C)
D_IN = sum(_SPLIT_SIZES)

kernel_name = 'hymba_nsa_fox_retention_step'

f32 = jnp.float32


def _rmsnorm(x, w):
    xf = x.astype(f32)
    y = xf * lax.rsqrt(jnp.mean(xf * xf, axis=-1, keepdims=True) + EPS)
    return (y * w.astype(f32)).astype(x.dtype)


def _rope(x, pos):
    half = HEAD_DIM // 2
    inv = ROPE_THETA ** (-jnp.arange(half, dtype=f32) / half)
    ang = pos.astype(f32)[:, None] * inv[None, :]
    cos = jnp.cos(ang)[None, :, None, :]
    sin = jnp.sin(ang)[None, :, None, :]
    xf = x.astype(f32)
    x1, x2 = xf[..., :half], xf[..., half:]
    return jnp.concatenate([x1 * cos - x2 * sin, x2 * cos + x1 * sin], axis=-1).astype(x.dtype)


def _masked_softmax(s, mask):
    s = jnp.where(mask, s.astype(f32), -jnp.inf)
    m = jnp.max(s, axis=-1, keepdims=True)
    m = jnp.where(jnp.isfinite(m), m, 0.0)
    p = jnp.where(mask, jnp.exp(s - m), 0.0)
    den = jnp.sum(p, axis=-1, keepdims=True)
    return p / jnp.where(den > 0, den, 1.0)


def _rows(a, start, n):
    return lax.dynamic_slice_in_dim(a, start, n, axis=1)


def _sweep_qblocks(fn, T):
    qb = Q_BLOCK if T % Q_BLOCK == 0 else T
    out = lax.map(lambda i: fn(i * qb, qb), jnp.arange(T // qb))
    out = jnp.moveaxis(out, 0, 1)
    return out.reshape(out.shape[0], T, *out.shape[3:])


def _gather_pages(cache, layer, page_table):
    g = cache[layer, page_table]
    return g.reshape(g.shape[0], g.shape[1] * g.shape[2], *g.shape[3:])


def _project(x, pos, norm_w, w_in, fox_bf, nsa_qn, nsa_kn, fox_qn, fox_kn):
    B, T, _ = x.shape
    z = _rmsnorm(x, norm_w) @ w_in
    points = [int(p) for p in np.cumsum(_SPLIT_SIZES)[:-1]]
    (a_q, a_kc, a_vc, a_ks, a_vs, a_kw, a_vw, a_g, a_gate,
     b_q, b_k, b_v, b_f, b_gate, c_q, c_k, c_v, c_gate) = jnp.split(z, points, axis=-1)

    def heads(t, h):
        return t.reshape(B, T, h, HEAD_DIM)

    def qk(t, h, g):
        return _rope(_rmsnorm(heads(t, h), g), pos)

    aq = qk(a_q, H_A, nsa_qn)
    akc = qk(a_kc, KV_A, nsa_kn[0])
    aks = qk(a_ks, KV_A, nsa_kn[1])
    akw = qk(a_kw, KV_A, nsa_kn[2])
    avc, avs, avw = heads(a_vc, KV_A), heads(a_vs, KV_A), heads(a_vw, KV_A)
    ag = jax.nn.sigmoid(a_g.reshape(B, T, H_A, 3))
    bq = _rmsnorm(heads(b_q, H_B), fox_qn)
    bk = _rmsnorm(heads(b_k, H_B), fox_kn)
    bv = heads(b_v, H_B)
    logf = jax.nn.log_sigmoid(b_f.astype(f32) + fox_bf.astype(f32))
    cq = _rope(heads(c_q, H_C), pos)
    ck = _rope(heads(c_k, H_C), pos) * RET_KEY_SCALE
    cv = heads(c_v, H_C)
    gate = jax.nn.silu(jnp.concatenate([a_gate, b_gate, c_gate], axis=-1))
    return (aq, akc, avc, aks, avs, akw, avw, ag, bq, bk, bv, logf, cq, ck, cv, gate)


def _nsa_compressed(q, kc, vc, q_pos):
    B, Tq = q.shape[:2]
    T = kc.shape[1]
    nb = T // L_CMP
    kb = kc[:, :nb * L_CMP].reshape(B, nb, L_CMP, KV_A, HEAD_DIM).mean(axis=2)
    vb = vc[:, :nb * L_CMP].reshape(B, nb, L_CMP, KV_A, HEAD_DIM).mean(axis=2)
    qg = q.reshape(B, Tq, KV_A, GQA_A, HEAD_DIM)
    s = jnp.einsum('btkgd,bnkd->btkgn', qg, kb) * ATT_SCALE
    blk_end = (jnp.arange(nb) + 1) * L_CMP - 1
    mask = (blk_end[None, :] <= q_pos[:, None])[None, :, None, None, :]
    p = _masked_softmax(s, mask)
    o = jnp.einsum('btkgn,bnkd->btkgd', p.astype(vb.dtype), vb)
    return o.reshape(B, Tq, H_A, HEAD_DIM), p


def _nsa_select_indices(p_cmp, q_pos, T):
    imp = p_cmp.sum(axis=3)
    nb = imp.shape[-1]
    r = L_SEL // L_CMP
    n_sel = -(-T // L_SEL)
    imp = jnp.pad(imp, ((0, 0), (0, 0), (0, 0), (0, n_sel * r - nb)))
    imp = imp.reshape(*imp.shape[:3], n_sel, r).sum(-1)
    j = jnp.arange(n_sel)
    cur = q_pos // L_SEL
    valid = j[None, :] <= cur[:, None]
    forced = (j[None, :] == 0) | (j[None, :] == cur[:, None]) | (j[None, :] == cur[:, None] - 1)
    score = jnp.where(valid[None, :, None, :], imp + jnp.where(forced, FORCE_BONUS, 0.0)[None, :, None, :], -1.0)
    _, idx = lax.top_k(score, min(TOP_N, n_sel))
    return idx


def _sel_blocks(k):
    B, T = k.shape[:2]
    n_sel = -(-T // L_SEL)
    k = jnp.pad(k, ((0, 0), (0, n_sel * L_SEL - T), (0, 0), (0, 0)))
    return k.reshape(B, n_sel, L_SEL, KV_A, HEAD_DIM).transpose(0, 3, 1, 2, 4)


def _nsa_selected(q, idx, q_pos, ksb, vsb):
    B, Tq = q.shape[:2]
    qg = q.reshape(B, Tq, KV_A, GQA_A, HEAD_DIM)
    bi = jnp.arange(B)[:, None, None, None]
    hi = jnp.arange(KV_A)[None, None, :, None]
    kg = ksb[bi, hi, idx]
    vg = vsb[bi, hi, idx]
    s = jnp.einsum('btkgd,btknld->btkgnl', qg, kg) * ATT_SCALE
    s = s.reshape(B, Tq, KV_A, GQA_A, -1)
    kpos = (idx[..., None] * L_SEL + jnp.arange(L_SEL)).reshape(B, Tq, KV_A, 1, -1)
    mask = kpos <= q_pos[None, :, None, None, None]
    p = _masked_softmax(s, mask).astype(vg.dtype)
    o = jnp.einsum('btkgs,btksd->btkgd', p, vg.reshape(B, Tq, KV_A, -1, HEAD_DIM))
    return o.reshape(B, Tq, H_A, HEAD_DIM)


def _window_attend(q, kw, vw, q_pos, k_pos):
    B, Tq = q.shape[:2]
    qg = q.reshape(B, Tq, KV_A, GQA_A, HEAD_DIM)
    s = jnp.einsum('btkgd,bskd->btkgs', qg, kw) * ATT_SCALE
    d = q_pos[:, None] - k_pos[None, :]
    mask = ((d >= 0) & (d < WINDOW) & (k_pos[None, :] >= 0))[None, :, None, None, :]
    p = _masked_softmax(s, mask).astype(vw.dtype)
    o = jnp.einsum('btkgs,bskd->btkgd', p, vw)
    return o.reshape(B, Tq, H_A, HEAD_DIM)


def _nsa_combine(ag, o_cmp, o_slc, o_win):
    return ag[..., 0:1] * o_cmp + ag[..., 1:2] * o_slc + ag[..., 2:3] * o_win


def _fox_attend(q, k, v, cq, ck, q_pos, k_pos):
    s = jnp.einsum('bthd,bshd->bhts', q, k).astype(f32) * ATT_SCALE
    s = s + jnp.moveaxis(cq, 1, 2)[..., :, None] - jnp.moveaxis(ck, 1, 2)[..., None, :]
    mask = (k_pos[None, :] <= q_pos[:, None])[None, None]
    p = _masked_softmax(s, mask).astype(v.dtype)
    return jnp.einsum('bhts,bshd->bthd', p, v)


def _retention(q, k, v, r0):
    B, T, H, D = q.shape
    c = RET_CHUNK if T % RET_CHUNK == 0 else T
    n = T // c
    log_g = jnp.log1p(-jnp.exp2(-5.0 - jnp.arange(H, dtype=f32)))
    i = jnp.arange(c, dtype=f32)
    diff = i[:, None] - i[None, :]
    dmask = jnp.where(diff >= 0, jnp.exp(jnp.maximum(diff, 0.0) * log_g[:, None, None]), 0.0)
    xi = jnp.exp((i + 1.0)[None, :] * log_g[:, None])
    zeta = jnp.exp((c - 1.0 - i)[None, :] * log_g[:, None])
    g_c = jnp.exp(c * log_g)

    def to_chunks(a):
        return jnp.moveaxis(a.astype(f32).reshape(B, n, c, H, D), 1, 0)

    def step(R, qkv):
        qc, kc, vc = qkv
        a = jnp.einsum('bihd,bjhd->bhij', qc, kc) * dmask
        o = (jnp.einsum('bhij,bjhe->bihe', a, vc)
             + jnp.einsum('bihd,bhde->bihe', qc, R) * xi.T[None, :, :, None])
        R = R * g_c[None, :, None, None] + jnp.einsum('bjhd,bjhe,hj->bhde', kc, vc, zeta)
        return R, o

    R, o = lax.scan(step, r0.astype(f32), (to_chunks(q), to_chunks(k), to_chunks(v)))
    o = jnp.moveaxis(o, 0, 1).reshape(B, T, H, D)
    return o, R


def _head_groupnorm(o, w):
    B, T, H, D = o.shape
    mu = jnp.mean(o, axis=-1, keepdims=True)
    var = jnp.mean(jnp.square(o - mu), axis=-1, keepdims=True)
    y = (o - mu) * lax.rsqrt(var + EPS)
    return y.reshape(B, T, H * D) * w.astype(f32)


def _merge(x, o_a, o_b, o_c, gate, ret_gn_w, w_out):
    B, T, _ = x.shape
    o = jnp.concatenate([o_a.reshape(B, T, W_A).astype(x.dtype),
                         o_b.reshape(B, T, W_B).astype(x.dtype),
                         _head_groupnorm(o_c, ret_gn_w).astype(x.dtype)], axis=-1)
    return x + (o * gate) @ w_out


def _layer_prompt(x, norm_w, w_in, fox_bf, nsa_qn, nsa_kn, fox_qn, fox_kn, ret_gn_w, w_out):
    B, T, _ = x.shape
    pos = jnp.arange(T, dtype=jnp.int32)
    (aq, akc, avc, aks, avs, akw, avw, ag, bq, bk, bv, logf, cq, ck, cv, gate) = _project(
        x, pos, norm_w, w_in, fox_bf, nsa_qn, nsa_kn, fox_qn, fox_kn)
    o_cmp, p_cmp = _nsa_compressed(aq, akc, avc, pos)
    idx = _nsa_select_indices(p_cmp, pos, T)
    ksb, vsb = _sel_blocks(aks), _sel_blocks(avs)

    def slc_blk(start, qb):
        return _nsa_selected(_rows(aq, start, qb), _rows(idx, start, qb), start + jnp.arange(qb), ksb, vsb)
    o_slc = _sweep_qblocks(slc_blk, T)
    kwp = jnp.pad(akw, ((0, 0), (WINDOW, 0), (0, 0), (0, 0)))
    vwp = jnp.pad(avw, ((0, 0), (WINDOW, 0), (0, 0), (0, 0)))

    def win_blk(start, qb):
        k_pos = start - WINDOW + jnp.arange(WINDOW + qb)
        return _window_attend(_rows(aq, start, qb), _rows(kwp, start, WINDOW + qb),
                              _rows(vwp, start, WINDOW + qb), start + jnp.arange(qb), k_pos)
    o_win = _sweep_qblocks(win_blk, T)
    o_a = _nsa_combine(ag, o_cmp, o_slc, o_win)
    c = lax.cumsum(logf, axis=1)

    def fox_blk(start, qb):
        return _fox_attend(_rows(bq, start, qb), bk, bv, _rows(c, start, qb), c, start + jnp.arange(qb), pos)
    o_b = _sweep_qblocks(fox_blk, T)
    r0 = jnp.zeros((B, H_C, HEAD_DIM, HEAD_DIM), f32)
    o_c, r_new = _retention(cq, ck, cv, r0)
    y = _merge(x, o_a, o_b, o_c, gate, ret_gn_w, w_out)
    wk = min(WINDOW, T)
    return y, (jnp.stack([akc, avc], axis=2), jnp.stack([aks, avs], axis=2), jnp.stack([bk, bv], axis=2),
               logf, jnp.stack([akw, avw], axis=2)[:, T - wk:], r_new.astype(x.dtype))


def _layer_sample(x, cache_cmp, cache_slc, cache_fkv, cache_flogf, win_state, ret_state, page_table, layer,
                  norm_w, w_in, fox_bf, nsa_qn, nsa_kn, fox_qn, fox_kn, ret_gn_w, w_out):
    B, T, _ = x.shape
    past = page_table.shape[1] * PAGE_SIZE
    pos = past + jnp.arange(T, dtype=jnp.int32)
    (aq, akc, avc, aks, avs, akw, avw, ag, bq, bk, bv, logf, cq, ck, cv, gate) = _project(
        x, pos, norm_w, w_in, fox_bf, nsa_qn, nsa_kn, fox_qn, fox_kn)
    cmp_past = _gather_pages(cache_cmp, layer, page_table)
    kc_all = jnp.concatenate([cmp_past[:, :, 0].astype(akc.dtype), akc], axis=1)
    vc_all = jnp.concatenate([cmp_past[:, :, 1].astype(avc.dtype), avc], axis=1)
    o_cmp, p_cmp = _nsa_compressed(aq, kc_all, vc_all, pos)
    slc_past = _gather_pages(cache_slc, layer, page_table)
    ks_all = jnp.concatenate([slc_past[:, :, 0].astype(aks.dtype), aks], axis=1)
    vs_all = jnp.concatenate([slc_past[:, :, 1].astype(avs.dtype), avs], axis=1)
    idx = _nsa_select_indices(p_cmp, pos, past + T)
    o_slc = _nsa_selected(aq, idx, pos, _sel_blocks(ks_all), _sel_blocks(vs_all))
    wk = win_state.shape[1]
    kw_all = jnp.concatenate([win_state[:, :, 0].astype(akw.dtype), akw], axis=1)
    vw_all = jnp.concatenate([win_state[:, :, 1].astype(avw.dtype), avw], axis=1)
    o_win = _window_attend(aq, kw_all, vw_all, pos, past - wk + jnp.arange(wk + T))
    o_a = _nsa_combine(ag, o_cmp, o_slc, o_win)
    fkv_past = _gather_pages(cache_fkv, layer, page_table)
    lf_past = _gather_pages(cache_flogf, layer, page_table)
    k_all = jnp.concatenate([fkv_past[:, :, 0].astype(bk.dtype), bk], axis=1)
    v_all = jnp.concatenate([fkv_past[:, :, 1].astype(bv.dtype), bv], axis=1)
    c_all = lax.cumsum(jnp.concatenate([lf_past.astype(f32), logf], axis=1), axis=1)
    o_b = _fox_attend(bq, k_all, v_all, c_all[:, past:], c_all, pos, jnp.arange(past + T))
    o_c, r_new = _retention(cq, ck, cv, ret_state)
    y = _merge(x, o_a, o_b, o_c, gate, ret_gn_w, w_out)
    return y, (jnp.stack([akc, avc], axis=2), jnp.stack([aks, avs], axis=2), jnp.stack([bk, bv], axis=2),
               logf, jnp.stack([kw_all, vw_all], axis=2)[:, T:], r_new.astype(ret_state.dtype))


def setup_inputs(seed: int = 0) -> dict:
    key = jax.random.key(seed)
    ks = jax.random.split(key, 20)
    nrm = jax.random.normal
    n_pages = PAST_LEN // PAGE_SIZE
    n_used = DEC_BATCH * n_pages
    n_phys = n_used + max(1, n_used // 4)
    w_keep = min(WINDOW, PAST_LEN)
    x_prompt = nrm(ks[0], (BATCH, SEQ, D_MODEL), f32)
    x_sample = nrm(ks[1], (DEC_BATCH, DEC_SEQ, D_MODEL), f32)
    cache_nsa_cmp = nrm(ks[2], (DEPTH, n_phys, PAGE_SIZE, 2, KV_A, HEAD_DIM), f32)
    cache_nsa_slc = nrm(ks[3], (DEPTH, n_phys, PAGE_SIZE, 2, KV_A, HEAD_DIM), f32)
    cache_fox_kv = nrm(ks[4], (DEPTH, n_phys, PAGE_SIZE, 2, H_B, HEAD_DIM), f32)
    cache_fox_logf = jax.nn.log_sigmoid(FOX_BIAS_CENTER + nrm(ks[5], (DEPTH, n_phys, PAGE_SIZE, H_B), f32))
    state_nsa_win = nrm(ks[6], (DEPTH, DEC_BATCH, w_keep, 2, KV_A, HEAD_DIM), f32)
    state_ret = nrm(ks[7], (DEPTH, DEC_BATCH, H_C, HEAD_DIM, HEAD_DIM), f32)
    page_table = jax.random.permutation(ks[8], n_phys)[:n_used].reshape(DEC_BATCH, n_pages).astype(jnp.int32)
    norm_w = 1.0 + 0.02 * nrm(ks[9], (DEPTH, D_MODEL), f32)
    w_in = nrm(ks[10], (DEPTH, D_MODEL, D_IN), f32) * (D_MODEL ** -0.5)
    fox_bf = jax.random.uniform(ks[11], (DEPTH, H_B), f32, 1.0, 4.0)
    nsa_q_norm = 1.0 + 0.02 * nrm(ks[12], (DEPTH, HEAD_DIM), f32)
    nsa_k_norm = 1.0 + 0.02 * nrm(ks[13], (DEPTH, 3, HEAD_DIM), f32)
    fox_q_norm = 1.0 + 0.02 * nrm(ks[14], (DEPTH, HEAD_DIM), f32)
    fox_k_norm = 1.0 + 0.02 * nrm(ks[15], (DEPTH, HEAD_DIM), f32)
    ret_gn_w = 1.0 + 0.02 * nrm(ks[16], (DEPTH, W_C), f32)
    w_out = nrm(ks[17], (DEPTH, MIX_WIDTH, D_MODEL), f32) * (MIX_WIDTH ** -0.5)
    return {'x_prompt': x_prompt, 'x_sample': x_sample,
            'cache_nsa_cmp': cache_nsa_cmp, 'cache_nsa_slc': cache_nsa_slc,
            'cache_fox_kv': cache_fox_kv, 'cache_fox_logf': cache_fox_logf,
            'state_nsa_win': state_nsa_win, 'state_ret': state_ret, 'page_table': page_table,
            'norm_w': norm_w, 'w_in': w_in, 'fox_bf': fox_bf, 'nsa_q_norm': nsa_q_norm,
            'nsa_k_norm': nsa_k_norm, 'fox_q_norm': fox_q_norm, 'fox_k_norm': fox_k_norm,
            'ret_gn_w': ret_gn_w, 'w_out': w_out}


def reference(x_prompt, x_sample, cache_nsa_cmp, cache_nsa_slc, cache_fox_kv, cache_fox_logf,
              state_nsa_win, state_ret, page_table, norm_w, w_in, fox_bf, nsa_q_norm, nsa_k_norm,
              fox_q_norm, fox_k_norm, ret_gn_w, w_out):
    yp, ys = x_prompt, x_sample
    p_states, s_states = [], []
    for l in range(DEPTH):
        wl = (norm_w[l], w_in[l], fox_bf[l], nsa_q_norm[l], nsa_k_norm[l], fox_q_norm[l], fox_k_norm[l],
              ret_gn_w[l], w_out[l])
        yp, ps = _layer_prompt(yp, *wl)
        ys, ss = _layer_sample(ys, cache_nsa_cmp, cache_nsa_slc, cache_fox_kv, cache_fox_logf,
                               state_nsa_win[l], state_ret[l], page_table, l, *wl)
        p_states.append(ps)
        s_states.append(ss)

    def stk(states, i):
        return jnp.stack([st[i] for st in states], axis=0)

    return (yp, ys,
            stk(p_states, 0), stk(s_states, 0),
            stk(p_states, 1), stk(s_states, 1),
            stk(p_states, 2), stk(s_states, 2),
            stk(p_states, 3), stk(s_states, 3),
            stk(p_states, 4), stk(s_states, 4),
            stk(p_states, 5), stk(s_states, 5))
```

```python
import functools

import numpy as np
import jax
import jax.numpy as jnp
from jax import lax
from jax.experimental import pallas as pl
from jax.experimental.pallas import tpu as pltpu

f32 = jnp.float32
bf16 = jnp.bfloat16

HD = 64
H_A, KV_A, GQA_A = 8, 2, 4
H_B, H_C = 4, 4
L_CMP, L_SEL, TOP_N, WINDOW = 32, 64, 8, 512
PAGE = 128
RET_CHUNK = 128
ROPE_THETA = 10000.0
EPS = 1e-6
SCALE = HD ** -0.5
FORCE_BONUS = 1e3
NEG = -1e30

LANES = 128
VMEM_LIMIT = 48 * 1024 * 1024

C_AQ, C_CMP, C_SLC, C_WIN, C_MISC, C_GATE, C_BQ, C_BKV, C_CQKV, D_PACK = (
    0, 512, 768, 1024, 1280, 1408, 2432, 2688, 3200, 3968)
MISC_G = 3 * H_A


def _cparams(sem):
    return pltpu.CompilerParams(dimension_semantics=sem, vmem_limit_bytes=VMEM_LIMIT)


def _dot(a, b):
    return jnp.dot(a, b, preferred_element_type=f32)


def _dot_nt(a, b):
    return lax.dot_general(a, b, (((1,), (1,)), ((), ())), preferred_element_type=f32)


def _split(x, n):
    parts = []
    r = x
    for i in range(n):
        h = r.astype(bf16)
        parts.append(h)
        if i + 1 < n:
            r = r - h.astype(f32)
    return parts


def _dot_split(x, w, n=2):
    acc = None
    for h in _split(x, n):
        d = _dot(h, w)
        acc = d if acc is None else acc + d
    return acc


def _dot_split_l(w, x, n=2):
    acc = None
    for h in _split(x, n):
        d = _dot(w, h)
        acc = d if acc is None else acc + d
    return acc


def _lane(shape):
    return lax.broadcasted_iota(jnp.int32, shape, len(shape) - 1)


def _row(shape):
    return lax.broadcasted_iota(jnp.int32, shape, len(shape) - 2)


def _swap_half_heads(y):
    lane = _lane(y.shape)
    return jnp.where((lane & 32) == 0, pltpu.roll(y, 96, 1), pltpu.roll(y, 32, 1))


def _head_rms(z, bd, g):
    ms = _dot_split(z * z, bd)
    return z * lax.rsqrt(ms + EPS) * g


def _rope(y, cos, sin):
    return y * cos + _swap_half_heads(y) * sin


def _sigmoid(z):
    return 1.0 / (1.0 + jnp.exp(-z))


def _proj_kernel(x_ref, nw_ref, w_ref, bd_ref, pool_ref, cos_ref, sin_ref, gn_ref, bf_ref,
                 qal_ref, cmp_ref, slc_ref, win_ref, misc_ref, gate_ref, bqal_ref, fkv_ref,
                 cqkv_ref, kvb_ref):
    x = x_ref[...]
    ms = jnp.mean(x * x, axis=-1, keepdims=True)
    xn = (x * lax.rsqrt(ms + EPS) * nw_ref[...]).astype(bf16)
    bd = bd_ref[...]
    cos = cos_ref[...]
    sin = sin_ref[...]
    lane = _lane((x.shape[0], LANES))
    lo = lane < HD

    def z_of(c0, width):
        return _dot(xn, w_ref[:, c0:c0 + width])

    def unit(z, u):
        return z[:, u * LANES:(u + 1) * LANES]

    z = z_of(C_AQ, H_A * HD)
    g_aq = gn_ref[0:1, :]
    for u in range(H_A // 2):
        y = _rope(_head_rms(unit(z, u), bd, g_aq), cos, sin) * SCALE
        yr = pltpu.roll(y, HD, 1)
        kv_lo = (2 * u) // GQA_A == 0
        keep = lo if kv_lo else jnp.logical_not(lo)
        even, odd = (y, yr) if kv_lo else (yr, y)
        qal_ref[:, (2 * u) * LANES:(2 * u + 1) * LANES] = jnp.where(keep, even, 0.0).astype(bf16)
        qal_ref[:, (2 * u + 1) * LANES:(2 * u + 2) * LANES] = jnp.where(keep, odd, 0.0).astype(bf16)

    for i, (c0, o_ref) in enumerate(((C_CMP, cmp_ref), (C_SLC, slc_ref), (C_WIN, win_ref))):
        z = z_of(c0, 2 * LANES)
        k = _rope(_head_rms(unit(z, 0), bd, gn_ref[1 + i:2 + i, :]), cos, sin)
        v = unit(z, 1)
        o_ref[:, 0:LANES] = k
        o_ref[:, LANES:2 * LANES] = v
        if i == 0:
            pool = pool_ref[...]
            kvb_ref[:, 0:LANES] = _dot_split_l(pool, k)
            kvb_ref[:, LANES:2 * LANES] = _dot_split_l(pool, v)

    z = z_of(C_MISC, LANES)
    zf = z + bf_ref[...]
    logf = jnp.minimum(zf, 0.0) - jnp.log1p(jnp.exp(-jnp.abs(zf)))
    misc_ref[...] = jnp.where(lane < MISC_G, _sigmoid(z), jnp.where(lane < MISC_G + H_B, logf, 0.0))

    for c in range(0, 1024, 512):
        z = z_of(C_GATE + c, 512)
        gate_ref[:, c:c + 512] = z * _sigmoid(z)

    z = z_of(C_BQ, H_B * HD)
    for u in range(H_B // 2):
        y = _head_rms(unit(z, u), bd, gn_ref[4:5, :]) * SCALE
        bqal_ref[:, (2 * u) * LANES:(2 * u + 1) * LANES] = jnp.where(lo, y, 0.0).astype(bf16)
        bqal_ref[:, (2 * u + 1) * LANES:(2 * u + 2) * LANES] = jnp.where(lo, 0.0, y).astype(bf16)
    z = z_of(C_BKV, 2 * H_B * HD)
    for u in range(2):
        fkv_ref[:, u * LANES:(u + 1) * LANES] = _head_rms(unit(z, u), bd, gn_ref[5:6, :])
    fkv_ref[:, 2 * LANES:4 * LANES] = z[:, 2 * LANES:4 * LANES]

    z = z_of(C_CQKV, 3 * H_C * HD)
    for u in range(2):
        cqkv_ref[:, u * LANES:(u + 1) * LANES] = _rope(unit(z, u), cos, sin).astype(bf16)
        cqkv_ref[:, (2 + u) * LANES:(3 + u) * LANES] = (_rope(unit(z, 2 + u), cos, sin) * SCALE).astype(bf16)
    cqkv_ref[:, 4 * LANES:6 * LANES] = z[:, 4 * LANES:6 * LANES].astype(bf16)


def _pool_matrix(n_out, n_rows):
    r = np.arange(n_rows)[None, :] // L_CMP == np.arange(n_out)[:, None]
    return jnp.asarray(r.astype(np.float32) / L_CMP, bf16)


def _proj(x2, lw, cos_t, sin_t, tm):
    n = x2.shape[0]
    nt = cos_t.shape[0] // tm
    const = lambda i: (0, 0)
    rows = lambda i: (i, 0)
    outs = [(H_A * LANES, bf16), (2 * LANES, f32), (2 * LANES, f32), (2 * LANES, f32), (LANES, f32),
            (1024, f32), (H_B * LANES, bf16), (4 * LANES, f32), (6 * LANES, bf16)]
    out_shape = [jax.ShapeDtypeStruct((n, w), dt) for w, dt in outs]
    out_specs = [pl.BlockSpec((tm, w), rows) for w, _ in outs]
    out_shape.append(jax.ShapeDtypeStruct((n // L_CMP, 2 * LANES), f32))
    out_specs.append(pl.BlockSpec((tm // L_CMP, 2 * LANES), rows))
    return pl.pallas_call(
        _proj_kernel,
        grid=(n // tm,),
        in_specs=[pl.BlockSpec((tm, 1024), rows),
                  pl.BlockSpec((1, 1024), const),
                  pl.BlockSpec((1024, D_PACK), const),
                  pl.BlockSpec((LANES, LANES), const),
                  pl.BlockSpec((tm // L_CMP, tm), const),
                  pl.BlockSpec((tm, LANES), lambda i: (i % nt, 0)),
                  pl.BlockSpec((tm, LANES), lambda i: (i % nt, 0)),
                  pl.BlockSpec((8, LANES), const),
                  pl.BlockSpec((1, LANES), const)],
        out_specs=out_specs,
        out_shape=out_shape,
        compiler_params=_cparams(("parallel",)),
        name="proj",
    )(x2, lw["norm_w"], lw["w_pack"], lw["bd"], _pool_matrix(tm // L_CMP, tm), cos_t, sin_t,
      lw["gains"], lw["bf"])


def _merge_kernel(x_ref, oa_ref, ob_ref, oc_ref, gate_ref, w_ref, y_ref):
    g = gate_ref[...]
    wa, wb = H_A * HD, H_A * HD + H_B * HD
    acc = _dot((oa_ref[...] * g[:, 0:wa]).astype(bf16), w_ref[0:wa, :])
    acc += _dot((ob_ref[...] * g[:, wa:wb]).astype(bf16), w_ref[wa:wb, :])
    acc += _dot((oc_ref[...] * g[:, wb:]).astype(bf16), w_ref[wb:, :])
    y_ref[...] = x_ref[...] + acc


def _merge(x2, oa, ob, oc, gate, w_out, tm):
    n = x2.shape[0]
    rows = lambda i: (i, 0)
    return pl.pallas_call(
        _merge_kernel,
        grid=(n // tm,),
        in_specs=[pl.BlockSpec((tm, 1024), rows), pl.BlockSpec((tm, 512), rows),
                  pl.BlockSpec((tm, 256), rows), pl.BlockSpec((tm, 256), rows),
                  pl.BlockSpec((tm, 1024), rows), pl.BlockSpec((1024, 1024), lambda i: (0, 0))],
        out_specs=pl.BlockSpec((tm, 1024), rows),
        out_shape=jax.ShapeDtypeStruct((n, 1024), f32),
        compiler_params=_cparams(("parallel",)),
        name="merge",
    )(x2, oa, ob, oc, gate, w_out)


def _ret_prompt_kernel(q_ref, k_ref, v_ref, dm_ref, xi_ref, zeta_ref, gc_ref, gnw_ref, bd_ref,
                       o_ref, r_ref):
    c = RET_CHUNK
    n_chunks = q_ref.shape[1] // c
    lo = _lane((c, LANES)) < HD
    same_head = (_row((LANES, LANES)) < HD) == (_lane((LANES, LANES)) < HD)
    bd = bd_ref[...]
    xi = xi_ref[0]
    zeta = zeta_ref[0]
    gnw = gnw_ref[...]

    def step(i, r):
        rows = pl.ds(pl.multiple_of(i * c, c), c)
        q = q_ref[0, rows, :]
        k = k_ref[0, rows, :]
        v = v_ref[0, rows, :]
        zero = jnp.zeros_like(q)
        a0 = _dot_nt(jnp.where(lo, q, zero), k) * dm_ref[0]
        a1 = _dot_nt(jnp.where(lo, zero, q), k) * dm_ref[1]
        o = jnp.where(lo, _dot(a0.astype(bf16), v), _dot(a1.astype(bf16), v))
        o = o + _dot(q, r.astype(bf16)) * xi
        kz_t = (k.astype(f32) * zeta).T.astype(bf16)
        r = r * gc_ref[0] + jnp.where(same_head, _dot(kz_t, v), 0.0)
        mu = _dot_split(o, bd)
        d = o - mu
        var = _dot_split(d * d, bd)
        o_ref[0, rows, :] = d * lax.rsqrt(var + EPS) * gnw
        return r

    r = lax.fori_loop(0, n_chunks, step, jnp.zeros((LANES, LANES), f32))
    r_ref[0, 0] = r[0:HD, 0:HD]
    r_ref[0, 1] = pltpu.roll(r, HD, 1)[HD:2 * HD, 0:HD]


def _ret_prompt(cqkv, lw, b, t):
    blk = lambda u0: pl.BlockSpec((1, t, LANES), lambda i, p: (i, 0, u0 + p))
    pair = lambda shape: pl.BlockSpec(shape, lambda i, p: (p,) + (0,) * (len(shape) - 1))
    return pl.pallas_call(
        _ret_prompt_kernel,
        grid=(b, 2),
        in_specs=[blk(0), blk(2), blk(4),
                  pair((2, RET_CHUNK, RET_CHUNK)), pair((1, RET_CHUNK, LANES)),
                  pair((1, RET_CHUNK, LANES)), pair((1, 1, LANES)),
                  pl.BlockSpec((1, LANES), lambda i, p: (0, p)),
                  pl.BlockSpec((LANES, LANES), lambda i, p: (0, 0))],
        out_specs=[pl.BlockSpec((1, t, LANES), lambda i, p: (i, 0, p)),
                   pl.BlockSpec((1, 2, HD, HD), lambda i, p: (i, p, 0, 0))],
        out_shape=[jax.ShapeDtypeStruct((b, t, H_C * HD), f32),
                   jax.ShapeDtypeStruct((b, H_C, HD, HD), f32)],
        compiler_params=_cparams(("parallel", "parallel")),
        name="ret_prompt",
    )(cqkv, cqkv, cqkv, lw["ret_dmask"], lw["ret_xi"], lw["ret_zeta"], lw["ret_gc"], lw["ret_gn_w"],
      lw["bd"])


def _fox_prep_kernel(misc_ref, ltri_ref, ccol_ref, crow_ref):
    c = LANES
    n_chunks = misc_ref.shape[1] // c
    ltri = ltri_ref[...]

    def step(i, carry):
        rows = pl.ds(pl.multiple_of(i * c, c), c)
        cs = _dot_split_l(ltri, misc_ref[0, rows, :], 3) + carry
        for p in range(2):
            sh = pltpu.roll(cs, LANES - (MISC_G + 2 * p), 1)
            ccol_ref[0, p, rows, :] = sh
            crow_ref[0, p, i] = sh.T[0:8, :]
        return cs[c - 1:c, :]

    lax.fori_loop(0, n_chunks, step, jnp.zeros((1, LANES), f32))


def _fox_prep(misc3, ltri):
    b, t, _ = misc3.shape
    return pl.pallas_call(
        _fox_prep_kernel,
        grid=(b,),
        in_specs=[pl.BlockSpec((1, t, LANES), lambda i: (i, 0, 0)),
                  pl.BlockSpec((LANES, LANES), lambda i: (0, 0))],
        out_specs=[pl.BlockSpec((1, 2, t, LANES), lambda i: (i, 0, 0, 0)),
                   pl.BlockSpec((1, 2, t // LANES, 8, LANES), lambda i: (i, 0, 0, 0, 0))],
        out_shape=[jax.ShapeDtypeStruct((b, 2, t, LANES), f32),
                   jax.ShapeDtypeStruct((b, 2, t // LANES, 8, LANES), f32)],
        compiler_params=_cparams(("parallel",)),
        name="fox_prep",
    )(misc3, ltri)


def _softmax_step(s, m, l, acc, v):
    m_new = jnp.maximum(m, jnp.max(s, axis=1, keepdims=True))
    alpha = jnp.exp(m - m_new)
    p = jnp.exp(s - m_new)
    l = alpha * l + jnp.sum(p, axis=1, keepdims=True)
    acc = alpha * acc + _dot(p.astype(bf16), v)
    return m_new, l, acc


def _fox_prompt_kernel(q_ref, k_ref, v_ref, ccol_ref, crow_ref, o_ref, m_scr, l_scr, acc_scr):
    tq = q_ref.shape[1]
    qi = pl.program_id(2)
    q2 = jnp.concatenate([q_ref[0, :, 0:LANES], q_ref[0, :, LANES:2 * LANES]], axis=0)
    cc = ccol_ref[0, 0]
    cq = jnp.concatenate([cc[:, 0:1], cc[:, 1:2]], axis=0)
    m_scr[...] = jnp.full(m_scr.shape, NEG, f32)
    l_scr[...] = jnp.zeros(l_scr.shape, f32)
    acc_scr[...] = jnp.zeros(acc_scr.shape, f32)

    def tile(kt, masked):
        cols = pl.ds(pl.multiple_of(kt * tq, tq), tq)
        k = k_ref[0, cols, :].astype(bf16)
        v = v_ref[0, cols, :].astype(bf16)
        ck = crow_ref[0, 0, kt]
        ck2 = jnp.concatenate([jnp.broadcast_to(ck[0:1, :], (tq, tq)),
                               jnp.broadcast_to(ck[1:2, :], (tq, tq))], axis=0)
        s = _dot_nt(q2, k) + (cq - ck2)
        if masked:
            causal = _lane((tq, tq)) <= _row((tq, tq))
            s = jnp.where(jnp.concatenate([causal, causal], axis=0), s, NEG)
        m, l, acc = _softmax_step(s, m_scr[...], l_scr[...], acc_scr[...], v)
        m_scr[...] = m
        l_scr[...] = l
        acc_scr[...] = acc

    def body(kt, carry):
        tile(kt, False)
        return carry

    lax.fori_loop(0, qi, body, 0)
    tile(qi, True)
    o = acc_scr[...] / l_scr[...]
    o_ref[0] = jnp.where(_lane((tq, LANES)) < HD, o[0:tq], o[tq:2 * tq])


def _fox_prompt(bqal, fkv, ccol, crow, b, t, tq):
    return pl.pallas_call(
        _fox_prompt_kernel,
        grid=(b, 2, t // tq),
        in_specs=[pl.BlockSpec((1, tq, 2 * LANES), lambda i, p, j: (i, j, p)),
                  pl.BlockSpec((1, t, LANES), lambda i, p, j: (i, 0, p)),
                  pl.BlockSpec((1, t, LANES), lambda i, p, j: (i, 0, 2 + p)),
                  pl.BlockSpec((1, 1, tq, LANES), lambda i, p, j: (i, p, j, 0)),
                  pl.BlockSpec((1, 1, t // tq, 8, tq), lambda i, p, j: (i, p, 0, 0, 0))],
        out_specs=pl.BlockSpec((1, tq, LANES), lambda i, p, j: (i, j, p)),
        out_shape=jax.ShapeDtypeStruct((b, t, H_B * HD), f32),
        scratch_shapes=[pltpu.VMEM((2 * tq, 1), f32), pltpu.VMEM((2 * tq, 1), f32),
                        pltpu.VMEM((2 * tq, LANES), f32)],
        compiler_params=_cparams(("parallel", "parallel", "parallel")),
        name="fox_prompt",
    )(bqal, fkv, fkv, ccol, crow)


def _top_n_mask(score, n):
    lane = _lane(score.shape)
    sel = jnp.zeros(score.shape, f32)
    for _ in range(n):
        m = jnp.max(score, axis=1, keepdims=True)
        first = jnp.min(jnp.where(score == m, lane, LANES), axis=1, keepdims=True)
        pick = lane == first
        sel = jnp.where(pick, 1.0, sel)
        score = jnp.where(pick, -2.0, score)
    return sel


def _masked_softmax_rows(s, valid):
    s = jnp.where(valid, s, NEG)
    m = jnp.max(s, axis=1, keepdims=True)
    m = jnp.where(m > 0.5 * NEG, m, 0.0)
    p = jnp.where(valid, jnp.exp(s - m), 0.0)
    den = jnp.sum(p, axis=1, keepdims=True)
    return p / jnp.where(den > 0, den, 1.0)


def _select_blocks(imp, cur2):
    lane = _lane(imp.shape)
    imp2 = imp + pltpu.roll(imp, LANES - 1, 1)
    is_blk = (lane & 1) == 0
    valid = is_blk & (lane <= cur2)
    forced = (lane == 0) | (lane == cur2) | (lane == cur2 - 2)
    score = jnp.where(valid, imp2 + jnp.where(forced, FORCE_BONUS, 0.0), -1.0)
    return _top_n_mask(score, TOP_N)


def _nsa_prompt_kernel(q_ref, kvb_ref, slc_ref, win_ref, misc_ref, exp_ref, blkend_ref,
                       o_ref, m_scr, l_scr, acc_scr):
    tq = q_ref.shape[1]
    qi = pl.program_id(1)
    r4 = GQA_A * tq
    t_col = qi * tq + _row((tq, 1))
    t4 = jnp.concatenate([t_col] * GQA_A, axis=0)
    lane = _lane((tq, LANES))
    lo = lane < HD
    misc = misc_ref[0]
    nb = kvb_ref.shape[1]
    zpad = jnp.zeros((LANES - nb, LANES), bf16)

    def flash(kv_ref, q4, kt_lo, kt_hi, bias_fn):
        m_scr[...] = jnp.full(m_scr.shape, NEG, f32)
        l_scr[...] = jnp.zeros(l_scr.shape, f32)
        acc_scr[...] = jnp.zeros(acc_scr.shape, f32)

        def body(kt, carry):
            rows = pl.ds(pl.multiple_of(kt * tq, tq), tq)
            k = kv_ref[0, rows, 0:LANES].astype(bf16)
            v = kv_ref[0, rows, LANES:2 * LANES].astype(bf16)
            bias = bias_fn(kt)
            s = _dot_nt(q4, k) + jnp.concatenate([bias] * GQA_A, axis=0)
            m, l, acc = _softmax_step(s, m_scr[...], l_scr[...], acc_scr[...], v)
            m_scr[...] = m
            l_scr[...] = l
            acc_scr[...] = acc
            return carry

        lax.fori_loop(kt_lo, kt_hi, body, 0)
        return acc_scr[...] / l_scr[...]

    for kv in range(KV_A):
        q4 = jnp.concatenate([q_ref[0, :, (GQA_A * kv + j) * LANES:(GQA_A * kv + j + 1) * LANES]
                              for j in range(GQA_A)], axis=0)
        kb = jnp.concatenate([kvb_ref[0, :, 0:LANES].astype(bf16), zpad], axis=0)
        vb = jnp.concatenate([kvb_ref[0, :, LANES:2 * LANES].astype(bf16), zpad], axis=0)
        p = _masked_softmax_rows(_dot_nt(q4, kb), blkend_ref[...] <= t4)
        o_cmp = _dot(p.astype(bf16), vb)
        imp = p[0:tq] + p[tq:2 * tq] + p[2 * tq:3 * tq] + p[3 * tq:4 * tq]
        sel = _select_blocks(imp, (t_col // L_SEL) * 2).astype(bf16)

        def slc_bias(kt):
            hit = _dot(sel, exp_ref[kt]) > 0.5
            causal = (kt * tq + lane) <= t_col
            return jnp.where(hit & causal, 0.0, NEG)

        def win_bias(kt):
            d = t_col - (kt * tq + lane)
            return jnp.where((d >= 0) & (d < WINDOW), 0.0, NEG)

        o_slc = flash(slc_ref, q4, 0, qi + 1, slc_bias)
        o_win = flash(win_ref, q4, jnp.maximum(qi - WINDOW // tq, 0), qi + 1, win_bias)

        heads = []
        for j in range(GQA_A):
            g = GQA_A * kv + j
            rows = slice(j * tq, (j + 1) * tq)
            o = (misc[:, 3 * g:3 * g + 1] * o_cmp[rows] + misc[:, 3 * g + 1:3 * g + 2] * o_slc[rows]
                 + misc[:, 3 * g + 2:3 * g + 3] * o_win[rows])
            heads.append(o if (j % 2) == kv else pltpu.roll(o, HD, 1))
        for u in range(GQA_A // 2):
            unit = (GQA_A * kv) // 2 + u
            o_ref[0, :, unit * LANES:(unit + 1) * LANES] = jnp.where(lo, heads[2 * u], heads[2 * u + 1])


def _nsa_prompt(qal, kvb, slc, win, misc3, consts, b, t, tq):
    nb = t // L_CMP
    return pl.pallas_call(
        _nsa_prompt_kernel,
        grid=(b, t // tq),
        in_specs=[pl.BlockSpec((1, tq, H_A * LANES), lambda i, j: (i, j, 0)),
                  pl.BlockSpec((1, nb, 2 * LANES), lambda i, j: (i, 0, 0)),
                  pl.BlockSpec((1, t, 2 * LANES), lambda i, j: (i, 0, 0)),
                  pl.BlockSpec((1, t, 2 * LANES), lambda i, j: (i, 0, 0)),
                  pl.BlockSpec((1, tq, LANES), lambda i, j: (i, j, 0)),
                  pl.BlockSpec((t // tq, LANES, tq), lambda i, j: (0, 0, 0)),
                  pl.BlockSpec((1, LANES), lambda i, j: (0, 0))],
        out_specs=pl.BlockSpec((1, tq, H_A * HD), lambda i, j: (i, j, 0)),
        out_shape=jax.ShapeDtypeStruct((b, t, H_A * HD), f32),
        scratch_shapes=[pltpu.VMEM((GQA_A * tq, 1), f32), pltpu.VMEM((GQA_A * tq, 1), f32),
                        pltpu.VMEM((GQA_A * tq, LANES), f32)],
        compiler_params=_cparams(("parallel", "parallel")),
        name="nsa_prompt",
    )(qal, kvb, slc, win, misc3, consts["expand"], consts["blkend"])


def _rope_tables(pos):
    half = HD // 2
    inv = ROPE_THETA ** (-jnp.arange(half, dtype=f32) / half)
    ang = pos.astype(f32)[:, None] * inv[None, :]
    cos, sin = jnp.cos(ang), jnp.sin(ang)
    return (jnp.concatenate([cos, cos, cos, cos], axis=-1),
            jnp.concatenate([-sin, sin, -sin, sin], axis=-1))


def _constants(t, tq):
    lane = np.arange(LANES)
    bd = ((lane[:, None] // HD) == (lane[None, :] // HD)).astype(np.float32) / HD
    ltri = (lane[None, :] <= lane[:, None]).astype(np.float32)
    key = np.arange(t).reshape(t // tq, 1, tq)
    expand = ((lane[None, :, None] % 2 == 0) & (key // L_SEL == lane[None, :, None] // 2)).astype(np.float32)
    blkend = np.where(lane < t // L_CMP, (lane + 1) * L_CMP - 1, 2 ** 30).astype(np.int32)[None, :]
    log_g = jnp.log1p(-jnp.exp2(-5.0 - jnp.arange(H_C, dtype=f32)))
    i = jnp.arange(RET_CHUNK, dtype=f32)
    diff = i[:, None] - i[None, :]
    dmask = jnp.where(diff >= 0, jnp.exp(jnp.maximum(diff, 0.0) * log_g[:, None, None]), 0.0)
    xi = jnp.exp((i + 1.0)[None, :] * log_g[:, None])
    zeta = jnp.exp((RET_CHUNK - 1.0 - i)[None, :] * log_g[:, None])
    g_c = jnp.exp(RET_CHUNK * log_g)
    by_lane = lambda a: jnp.repeat(a.reshape(2, 2, -1).transpose(0, 2, 1), HD, axis=-1)
    return {"bd": jnp.asarray(bd, bf16), "pool8": _pool_matrix(8, 2 * PAGE),
            "ltri": jnp.asarray(ltri, bf16),
            "expand": jnp.asarray(expand, bf16), "blkend": jnp.asarray(blkend),
            "ret_dmask": dmask, "ret_xi": by_lane(xi), "ret_zeta": by_lane(zeta),
            "ret_gc": by_lane(g_c[:, None]), "log_g": log_g}


def _pack_w_in(w):
    off = np.cumsum([0, 512, 128, 128, 128, 128, 128, 128, 24, 512, 256, 256, 256, 4, 256, 256, 256, 256, 256])
    seg = lambda i: w[:, off[i]:off[i + 1]]
    (a_q, a_kc, a_vc, a_ks, a_vs, a_kw, a_vw, a_g, a_gate,
     b_q, b_k, b_v, b_f, b_gate, c_q, c_k, c_v, c_gate) = [seg(i) for i in range(18)]
    pad = jnp.zeros((w.shape[0], LANES - MISC_G - H_B), w.dtype)
    return jnp.concatenate([a_q, a_kc, a_vc, a_ks, a_vs, a_kw, a_vw, a_g, b_f, pad,
                            a_gate, b_gate, c_gate, b_q, b_k, b_v, c_q, c_k, c_v], axis=1).astype(bf16)


def _layer_params(l, consts, norm_w, w_in, fox_bf, nsa_q_norm, nsa_k_norm, fox_q_norm, fox_k_norm,
                  ret_gn_w, w_out):
    two = lambda g: jnp.concatenate([g, g])[None, :]
    gains = jnp.concatenate([two(nsa_q_norm[l]), two(nsa_k_norm[l, 0]), two(nsa_k_norm[l, 1]),
                             two(nsa_k_norm[l, 2]), two(fox_q_norm[l]), two(fox_k_norm[l]),
                             jnp.zeros((2, LANES), f32)], axis=0)
    bf = jnp.zeros((1, LANES), f32).at[0, MISC_G:MISC_G + H_B].set(fox_bf[l])
    lw = dict(consts)
    lw.update(norm_w=norm_w[l][None, :], w_pack=_pack_w_in(w_in[l]), gains=gains, bf=bf,
              ret_gn_w=ret_gn_w[l][None, :], w_out=w_out[l].astype(bf16))
    return lw


def _prompt_layer(x, lw, consts, cos_t, sin_t):
    b, t, d = x.shape
    tq = 128
    x2 = x.reshape(b * t, d)
    qal, cmp_, slc, win, misc, gate, bqal, fkv, cqkv, kvb = _proj(x2, lw, cos_t, sin_t, 256)
    r3 = lambda a: a.reshape(b, t, a.shape[-1])
    misc3 = r3(misc)
    o_a = _nsa_prompt(r3(qal), kvb.reshape(b, t // L_CMP, 2 * LANES), r3(slc), r3(win), misc3,
                      consts, b, t, tq)
    ccol, crow = _fox_prep(misc3, consts["ltri"])
    o_b = _fox_prompt(r3(bqal), r3(fkv), ccol, crow, b, t, tq)
    o_c, r_new = _ret_prompt(r3(cqkv), lw, b, t)
    y = _merge(x2, o_a.reshape(b * t, -1), o_b.reshape(b * t, -1), o_c.reshape(b * t, -1), gate,
               lw["w_out"], 256)
    wk = min(WINDOW, t)
    states = (cmp_.reshape(b, t, 2, KV_A, HD), slc.reshape(b, t, 2, KV_A, HD),
              fkv.reshape(b, t, 2, H_B, HD), misc3[:, :, MISC_G:MISC_G + H_B],
              r3(win)[:, t - wk:].reshape(b, wk, 2, KV_A, HD), r_new)
    return y.reshape(b, t, d), states


def _head_col(vec_row, shape, base, stride):
    pick = _lane(shape) == base + stride * _row(shape)
    return jnp.sum(jnp.where(pick, jnp.broadcast_to(vec_row, shape), 0.0), axis=1, keepdims=True)


def _dec_nsa_kernel(pt_ref, q_ref, slc_new_ref, win_new_ref, misc_ref, win_ref, pool_ref, *rest,
                    group, n_pages):
    cmp_pages = rest[0:group]
    slc_pages = rest[group:2 * group]
    o_ref, kvb_scr, m_scr, l_scr, o_scr = rest[2 * group:]
    pg = pl.program_id(1)
    shape = (H_A, LANES)
    lane = _lane(shape)
    row = _row(shape)
    lo = lane < HD
    q8 = q_ref[0]
    past = n_pages * PAGE
    nb = past // L_CMP
    cur2 = 2 * (past // L_SEL)

    @pl.when(pg == 0)
    def _():
        m_scr[...] = jnp.full(shape, NEG, f32)
        l_scr[...] = jnp.zeros(shape, f32)

    for g in range(0, group, 2):
        pooled = (_dot_split_l(pool_ref[:, 0:PAGE], cmp_pages[g][0, 0])
                  + _dot_split_l(pool_ref[:, PAGE:2 * PAGE], cmp_pages[g + 1][0, 0]))
        kvb_scr[pl.ds(pl.multiple_of((pg * group + g) * (PAGE // L_CMP), 8), 8), :] = pooled

    for g in range(group):
        page = pg * group + g
        k = slc_pages[g][0, 0, :, 0:LANES].astype(bf16)
        v = slc_pages[g][0, 0, :, LANES:2 * LANES].astype(bf16)
        s = _dot_nt(q8, k)
        m_lo = jnp.max(jnp.where(lo, s, NEG), axis=1, keepdims=True)
        m_hi = jnp.max(jnp.where(lo, NEG, s), axis=1, keepdims=True)
        p = jnp.exp(s - jnp.where(lo, m_lo, m_hi))
        p_lo = jnp.where(lo, p, 0.0)
        p_hi = jnp.where(lo, 0.0, p)
        o2 = _dot(jnp.concatenate([p_lo, p_hi], axis=0).astype(bf16), v)
        at_lo = lane == 4 * page
        at_hi = lane == 4 * page + 2
        m_scr[...] = jnp.where(at_lo, m_lo, jnp.where(at_hi, m_hi, m_scr[...]))
        l_scr[...] = jnp.where(at_lo, jnp.sum(p_lo, axis=1, keepdims=True),
                               jnp.where(at_hi, jnp.sum(p_hi, axis=1, keepdims=True), l_scr[...]))
        o_scr[2 * page] = o2[0:H_A]
        o_scr[2 * page + 1] = o2[H_A:2 * H_A]

    @pl.when(pg == pl.num_programs(1) - 1)
    def _():
        qf = q8.astype(f32)
        zpad = jnp.zeros((LANES - nb, LANES), bf16)
        kb = jnp.concatenate([kvb_scr[:, 0:LANES].astype(bf16), zpad], axis=0)
        vb = jnp.concatenate([kvb_scr[:, LANES:2 * LANES].astype(bf16), zpad], axis=0)
        p = _masked_softmax_rows(_dot_nt(q8, kb), lane < nb)
        o_cmp = _dot(p.astype(bf16), vb)
        imp = jnp.where(row < GQA_A,
                        jnp.sum(p[0:GQA_A], axis=0, keepdims=True),
                        jnp.sum(p[GQA_A:2 * GQA_A], axis=0, keepdims=True))
        sel = _select_blocks(imp, cur2) > 0.5
        bf_round = lambda a: a.astype(bf16).astype(f32)
        k_new = bf_round(slc_new_ref[0, :, 0:LANES])
        v_new = bf_round(slc_new_ref[0, :, LANES:2 * LANES])
        s_new = jnp.sum(qf * k_new, axis=1, keepdims=True)
        m_all = jnp.where(lane == cur2, s_new, m_scr[...])
        l_all = jnp.where(lane == cur2, 1.0, l_scr[...])
        m_top = jnp.max(jnp.where(sel, m_all, NEG), axis=1, keepdims=True)
        w = jnp.where(sel, jnp.exp(m_all - m_top), 0.0)
        den = jnp.sum(w * l_all, axis=1, keepdims=True)
        acc = w[:, cur2:cur2 + 1] * v_new
        for j in range(past // L_SEL):
            acc = acc + w[:, 2 * j:2 * j + 1] * o_scr[j]
        o_slc = acc / den
        wk = win_ref.shape[2]
        kw = win_ref[0, 0, :, 0:LANES].astype(bf16)
        vw = win_ref[0, 0, :, LANES:2 * LANES].astype(bf16)
        kw_new = bf_round(win_new_ref[0, :, 0:LANES])
        vw_new = bf_round(win_new_ref[0, :, LANES:2 * LANES])
        s = jnp.where(_lane((H_A, wk)) > wk - WINDOW, _dot_nt(q8, kw), NEG)
        s_new = jnp.sum(qf * kw_new, axis=1, keepdims=True)
        m = jnp.maximum(jnp.max(s, axis=1, keepdims=True), s_new)
        pw = jnp.exp(s - m)
        pw_new = jnp.exp(s_new - m)
        o_win = ((_dot(pw.astype(bf16), vw) + pw_new * vw_new)
                 / (jnp.sum(pw, axis=1, keepdims=True) + pw_new))
        misc = misc_ref[0]
        o8 = (_head_col(misc, shape, 0, 3) * o_cmp + _head_col(misc, shape, 1, 3) * o_slc
              + _head_col(misc, shape, 2, 3) * o_win)
        o8r = pltpu.roll(o8, HD, 1)
        lo1 = _lane((1, LANES)) < HD
        for u in range(H_A // 2):
            kv_lo = (2 * u) // GQA_A == 0
            even = (o8 if kv_lo else o8r)[2 * u:2 * u + 1, :]
            odd = (o8r if kv_lo else o8)[2 * u + 1:2 * u + 2, :]
            o_ref[0, :, u * LANES:(u + 1) * LANES] = jnp.where(lo1, even, odd)


def _page_specs(block, layer, group, count):
    def spec(g):
        return pl.BlockSpec(block, lambda b, pg, pt: (layer, pt[b * count + pg * group + g]) + (0,) * (len(block) - 2))
    return [spec(g) for g in range(group)]


def _dec_nsa(pt_flat, q8, slc_new, win_new, misc, win_state, cache_cmp, cache_slc, pool8, layer, group):
    n_seq = q8.shape[0]
    n_pages = pt_flat.shape[0] // n_seq
    wk = win_state.shape[2]
    seq3 = lambda w: pl.BlockSpec((1, 1, w), lambda b, pg, pt: (b, 0, 0))
    page_block = (1, 1, PAGE, 2 * LANES)
    grid_spec = pltpu.PrefetchScalarGridSpec(
        num_scalar_prefetch=1,
        grid=(n_seq, n_pages // group),
        in_specs=[pl.BlockSpec((1, H_A, LANES), lambda b, pg, pt: (b, 0, 0)),
                  seq3(2 * LANES), seq3(2 * LANES), seq3(LANES),
                  pl.BlockSpec((1, 1, wk, 2 * LANES), lambda b, pg, pt: (layer, b, 0, 0)),
                  pl.BlockSpec((8, 2 * PAGE), lambda b, pg, pt: (0, 0))]
        + _page_specs(page_block, layer, group, n_pages) + _page_specs(page_block, layer, group, n_pages),
        out_specs=pl.BlockSpec((1, 1, H_A * HD), lambda b, pg, pt: (b, 0, 0)),
        scratch_shapes=[pltpu.VMEM((n_pages * PAGE // L_CMP, 2 * LANES), f32),
                        pltpu.VMEM((H_A, LANES), f32), pltpu.VMEM((H_A, LANES), f32),
                        pltpu.VMEM((n_pages * PAGE // L_SEL, H_A, LANES), f32)])
    return pl.pallas_call(
        functools.partial(_dec_nsa_kernel, group=group, n_pages=n_pages),
        grid_spec=grid_spec,
        out_shape=jax.ShapeDtypeStruct((n_seq, 1, H_A * HD), f32),
        compiler_params=_cparams(("parallel", "arbitrary")),
        name="dec_nsa",
    )(pt_flat, q8, slc_new, win_new, misc, win_state, pool8, *([cache_cmp] * group), *([cache_slc] * group))


def _lane_cumsum(x):
    lane = _lane(x.shape)
    sh = 1
    while sh < LANES:
        x = x + jnp.where(lane >= sh, pltpu.roll(x, sh, 1), 0.0)
        sh *= 2
    return x


def _dec_fox_kernel(pt_ref, q_ref, kv_new_ref, misc_ref, *rest, group):
    kv_pages = rest[0:group]
    lf_pages = rest[group:2 * group]
    o_ref, m_scr, l_scr, c_scr, acc0_scr, acc1_scr = rest[2 * group:]
    pg = pl.program_id(1)
    shape = (8, LANES)
    first_pair = _row(shape) < 2
    q8 = q_ref[0]

    @pl.when(pg == 0)
    def _():
        m_scr[...] = jnp.full(m_scr.shape, NEG, f32)
        l_scr[...] = jnp.zeros(l_scr.shape, f32)
        c_scr[...] = jnp.zeros(c_scr.shape, f32)
        acc0_scr[...] = jnp.zeros(shape, f32)
        acc1_scr[...] = jnp.zeros(shape, f32)

    for g in range(group):
        kv = kv_pages[g]
        c = c_scr[...] + _lane_cumsum(lf_pages[g][0, 0])
        c_scr[...] = c[:, LANES - 1:LANES]
        s = jnp.where(first_pair, _dot_nt(q8, kv[0, 0, :, 0:LANES].astype(bf16)),
                      _dot_nt(q8, kv[0, 0, :, LANES:2 * LANES].astype(bf16))) - c
        m = m_scr[...]
        m_new = jnp.maximum(m, jnp.max(s, axis=1, keepdims=True))
        alpha = jnp.exp(m - m_new)
        p = jnp.exp(s - m_new)
        pb = p.astype(bf16)
        m_scr[...] = m_new
        l_scr[...] = alpha * l_scr[...] + jnp.sum(p, axis=1, keepdims=True)
        acc0_scr[...] = alpha * acc0_scr[...] + _dot(pb, kv[0, 0, :, 2 * LANES:3 * LANES].astype(bf16))
        acc1_scr[...] = alpha * acc1_scr[...] + _dot(pb, kv[0, 0, :, 3 * LANES:4 * LANES].astype(bf16))

    @pl.when(pg == pl.num_programs(1) - 1)
    def _():
        bf_round = lambda a: jnp.broadcast_to(a.astype(bf16).astype(f32), shape)
        new = kv_new_ref[0]
        k_new = jnp.where(first_pair, bf_round(new[:, 0:LANES]), bf_round(new[:, LANES:2 * LANES]))
        v_new = jnp.where(first_pair, bf_round(new[:, 2 * LANES:3 * LANES]), bf_round(new[:, 3 * LANES:4 * LANES]))
        s_new = (jnp.sum(q8.astype(f32) * k_new, axis=1, keepdims=True)
                 - (c_scr[...] + _head_col(misc_ref[0], shape, MISC_G, 1)))
        m = m_scr[...]
        m_new = jnp.maximum(m, s_new)
        alpha = jnp.exp(m - m_new)
        p_new = jnp.exp(s_new - m_new)
        acc = jnp.where(first_pair, acc0_scr[...], acc1_scr[...])
        o8 = (alpha * acc + p_new * v_new) / (alpha * l_scr[...] + p_new)
        lo1 = _lane((1, LANES)) < HD
        for u in range(H_B // 2):
            o_ref[0, :, u * LANES:(u + 1) * LANES] = jnp.where(lo1, o8[2 * u:2 * u + 1, :],
                                                               o8[2 * u + 1:2 * u + 2, :])


def _dec_fox(pt_flat, qb4, fkv_new, misc, cache_fkv, cache_lft, layer, group):
    n_seq = qb4.shape[0]
    n_pages = pt_flat.shape[0] // n_seq
    seq3 = lambda w: pl.BlockSpec((1, 1, w), lambda b, pg, pt: (b, 0, 0))
    grid_spec = pltpu.PrefetchScalarGridSpec(
        num_scalar_prefetch=1,
        grid=(n_seq, n_pages // group),
        in_specs=[pl.BlockSpec((1, 8, LANES), lambda b, pg, pt: (b, 0, 0)), seq3(4 * LANES), seq3(LANES)]
        + _page_specs((1, 1, PAGE, 4 * LANES), layer, group, n_pages)
        + _page_specs((1, 1, 8, PAGE), layer, group, n_pages),
        out_specs=pl.BlockSpec((1, 1, H_B * HD), lambda b, pg, pt: (b, 0, 0)),
        scratch_shapes=[pltpu.VMEM((8, 1), f32), pltpu.VMEM((8, 1), f32), pltpu.VMEM((8, 1), f32),
                        pltpu.VMEM((8, LANES), f32), pltpu.VMEM((8, LANES), f32)])
    return pl.pallas_call(
        functools.partial(_dec_fox_kernel, group=group),
        grid_spec=grid_spec,
        out_shape=jax.ShapeDtypeStruct((n_seq, 1, H_B * HD), f32),
        compiler_params=_cparams(("parallel", "arbitrary")),
        name="dec_fox",
    )(pt_flat, qb4, fkv_new, misc, *([cache_fkv] * group), *([cache_lft] * group))


def _dec_ret_kernel(q_ref, k_ref, v_ref, r_ref, gam_ref, gnw_ref, o_ref, rn_ref):
    q = q_ref[0].astype(f32)
    k = k_ref[0].astype(f32)
    v = v_ref[0].astype(f32)
    qk = jnp.sum(q * k, axis=1, keepdims=True)
    eye = _row((HD, HD)) == _lane((HD, HD))
    col = lambda a: jnp.sum(jnp.where(eye, jnp.broadcast_to(a, (HD, HD)), 0.0), axis=1, keepdims=True)
    for h in range(H_C):
        hs = slice(h, h + 1)
        r = r_ref[0, 0, h]
        gam = gam_ref[hs, :]
        o = qk[hs, :] * v[hs, :] + jnp.sum(col(q[hs, :]) * r, axis=0, keepdims=True) * gam
        rn_ref[0, h] = r * gam + col(k[hs, :]) * v[hs, :]
        d = o - jnp.mean(o, axis=1, keepdims=True)
        var = jnp.mean(d * d, axis=1, keepdims=True)
        o_ref[0, hs, :] = d * lax.rsqrt(var + EPS) * gnw_ref[hs, :]


def _dec_ret(cq, ck, cv, state_ret, gam, gnw, layer):
    n_seq = cq.shape[0]
    head = pl.BlockSpec((1, H_C, HD), lambda b: (b, 0, 0))
    table = pl.BlockSpec((H_C, HD), lambda b: (0, 0))
    return pl.pallas_call(
        _dec_ret_kernel,
        grid=(n_seq,),
        in_specs=[head, head, head,
                  pl.BlockSpec((1, 1, H_C, HD, HD), lambda b: (layer, b, 0, 0, 0)), table, table],
        out_specs=[head, pl.BlockSpec((1, H_C, HD, HD), lambda b: (b, 0, 0, 0))],
        out_shape=[jax.ShapeDtypeStruct((n_seq, H_C, HD), f32),
                   jax.ShapeDtypeStruct((n_seq, H_C, HD, HD), f32)],
        compiler_params=_cparams(("parallel",)),
        name="dec_ret",
    )(cq, ck, cv, state_ret, gam, gnw)


def _sample_layer(x, layer, lw, consts, caches, pt_flat, cos_t, sin_t, group):
    n, _, d = x.shape
    x2 = x.reshape(n, d)
    qal, cmp_, slc, win, misc, gate, bqal, fkv, cqkv, _ = _proj(x2, lw, cos_t, sin_t, n)
    row3 = lambda a: a[:, None, :]
    o_a8 = _dec_nsa(pt_flat, qal.reshape(n, H_A, LANES), row3(slc), row3(win), row3(misc),
                    caches["win"], caches["cmp"], caches["slc"], consts["pool8"], layer, group)
    qb8 = jnp.pad(bqal.reshape(n, H_B, LANES), ((0, 0), (0, 8 - H_B), (0, 0)))
    o_b8 = _dec_fox(pt_flat, qb8, row3(fkv), row3(misc),
                    caches["fkv"], caches["lft"], layer, group)
    heads = lambda u: cqkv[:, 2 * u * LANES:(2 * u + 2) * LANES].reshape(n, H_C, HD).astype(f32)
    gam = jnp.broadcast_to(jnp.exp(consts["log_g"])[:, None], (H_C, HD))
    o_c, r_new = _dec_ret(heads(0), heads(1), heads(2), caches["ret"], gam,
                          lw["ret_gn_w"].reshape(H_C, HD), layer)
    y = _merge(x2, o_a8.reshape(n, H_A * HD), o_b8.reshape(n, H_B * HD), o_c.reshape(n, H_C * HD), gate,
               lw["w_out"], n)
    new_win = jnp.concatenate([caches["win"][layer][:, 1:], win[:, None, :]], axis=1)
    wk = new_win.shape[1]
    states = (cmp_.reshape(n, 1, 2, KV_A, HD), slc.reshape(n, 1, 2, KV_A, HD),
              fkv.reshape(n, 1, 2, H_B, HD), misc[:, None, MISC_G:MISC_G + H_B],
              new_win.reshape(n, wk, 2, KV_A, HD), r_new)
    return y.reshape(n, 1, d), states


def kernel(x_prompt, x_sample, cache_nsa_cmp, cache_nsa_slc, cache_fox_kv, cache_fox_logf,
           state_nsa_win, state_ret, page_table, norm_w, w_in, fox_bf, nsa_q_norm, nsa_k_norm,
           fox_q_norm, fox_k_norm, ret_gn_w, w_out):
    depth = norm_w.shape[0]
    b, t, _ = x_prompt.shape
    n_seq, n_pages = page_table.shape
    assert x_sample.shape[1] == 1 and t % 256 == 0 and n_pages % 2 == 0
    past = n_pages * PAGE
    consts = _constants(t, 128)
    cos_p, sin_p = _rope_tables(jnp.arange(t, dtype=jnp.int32))
    cos_s, sin_s = _rope_tables(jnp.full((n_seq,), past, dtype=jnp.int32))
    n_phys = cache_nsa_cmp.shape[1]
    caches = {"cmp": cache_nsa_cmp.reshape(depth, n_phys, PAGE, 2 * LANES),
              "slc": cache_nsa_slc.reshape(depth, n_phys, PAGE, 2 * LANES),
              "fkv": cache_fox_kv.reshape(depth, n_phys, PAGE, 4 * LANES),
              "lft": jnp.pad(jnp.swapaxes(cache_fox_logf, 2, 3), ((0, 0), (0, 0), (0, 8 - H_B), (0, 0))),
              "win": state_nsa_win.reshape(depth, n_seq, -1, 2 * LANES),
              "ret": state_ret}
    pt_flat = page_table.reshape(-1)
    yp, ys = x_prompt, x_sample
    p_states, s_states = [], []
    for l in range(depth):
        lw = _layer_params(l, consts, norm_w, w_in, fox_bf, nsa_q_norm, nsa_k_norm, fox_q_norm,
                           fox_k_norm, ret_gn_w, w_out)
        yp, ps = _prompt_layer(yp, lw, consts, cos_p, sin_p)
        ys, ss = _sample_layer(ys, l, lw, consts, caches, pt_flat, cos_s, sin_s, 2)
        p_states.append(ps)
        s_states.append(ss)
    stk = lambda states, i: jnp.stack([st[i] for st in states], axis=0)
    out = [yp, ys]
    for i in range(6):
        out += [stk(p_states, i), stk(s_states, i)]
    return tuple(out)
```

```python
import functools

import numpy as np
import jax
import jax.numpy as jnp
from jax import lax
from jax.experimental import pallas as pl
from jax.experimental.pallas import tpu as pltpu

f32 = jnp.float32
bf16 = jnp.bfloat16

HD = 64
H_A, KV_A, GQA_A = 8, 2, 4
H_B, H_C = 4, 4
L_CMP, L_SEL, TOP_N, WINDOW = 32, 64, 8, 512
PAGE = 128
RET_CHUNK = 128
ROPE_THETA = 10000.0
EPS = 1e-6
SCALE = HD ** -0.5
FORCE_BONUS = 1e3
NEG = -1e30

LANES = 128
VMEM_LIMIT = 48 * 1024 * 1024

C_AQ, C_CMP, C_SLC, C_WIN, C_MISC, C_GATE, C_BQ, C_BKV, C_CQKV, D_PACK = (
    0, 512, 768, 1024, 1280, 1408, 2432, 2688, 3200, 3968)
MISC_G = 3 * H_A


def _cparams(sem):
    return pltpu.CompilerParams(dimension_semantics=sem, vmem_limit_bytes=VMEM_LIMIT)


def _dot(a, b):
    return jnp.dot(a, b, preferred_element_type=f32)


def _dot_nt(a, b):
    return lax.dot_general(a, b, (((1,), (1,)), ((), ())), preferred_element_type=f32)


def _split(x, n):
    parts = []
    r = x
    for i in range(n):
        h = r.astype(bf16)
        parts.append(h)
        if i + 1 < n:
            r = r - h.astype(f32)
    return parts


def _dot_split(x, w, n=2):
    acc = None
    for h in _split(x, n):
        d = _dot(h, w)
        acc = d if acc is None else acc + d
    return acc


def _dot_split_l(w, x, n=2):
    acc = None
    for h in _split(x, n):
        d = _dot(w, h)
        acc = d if acc is None else acc + d
    return acc


def _lane(shape):
    return lax.broadcasted_iota(jnp.int32, shape, len(shape) - 1)


def _row(shape):
    return lax.broadcasted_iota(jnp.int32, shape, len(shape) - 2)


def _swap_half_heads(y):
    lane = _lane(y.shape)
    return jnp.where((lane & 32) == 0, pltpu.roll(y, 96, 1), pltpu.roll(y, 32, 1))


def _head_rms(z, bd, g):
    ms = _dot_split(z * z, bd)
    return z * lax.rsqrt(ms + EPS) * g


def _rope(y, cos, sin):
    return y * cos + _swap_half_heads(y) * sin


def _sigmoid(z):
    return 1.0 / (1.0 + jnp.exp(-z))


def _proj_kernel(x_ref, nw_ref, w_ref, bd_ref, pool_ref, cos_ref, sin_ref, gn_ref, bf_ref,
                 qal_ref, cmp_ref, slc_ref, win_ref, misc_ref, gate_ref, bqal_ref, fkv_ref,
                 cqkv_ref, kvb_ref):
    x = x_ref[...]
    ms = jnp.mean(x * x, axis=-1, keepdims=True)
    xn = (x * lax.rsqrt(ms + EPS) * nw_ref[...]).astype(bf16)
    bd = bd_ref[...]
    cos = cos_ref[...]
    sin = sin_ref[...]
    lane = _lane((x.shape[0], LANES))
    lo = lane < HD

    def z_of(c0, width):
        return _dot(xn, w_ref[:, c0:c0 + width])

    def unit(z, u):
        return z[:, u * LANES:(u + 1) * LANES]

    z = z_of(C_AQ, H_A * HD)
    g_aq = gn_ref[0:1, :]
    for u in range(H_A // 2):
        y = _rope(_head_rms(unit(z, u), bd, g_aq), cos, sin) * SCALE
        yr = pltpu.roll(y, HD, 1)
        kv_lo = (2 * u) // GQA_A == 0
        keep = lo if kv_lo else jnp.logical_not(lo)
        even, odd = (y, yr) if kv_lo else (yr, y)
        qal_ref[:, (2 * u) * LANES:(2 * u + 1) * LANES] = jnp.where(keep, even, 0.0).astype(bf16)
        qal_ref[:, (2 * u + 1) * LANES:(2 * u + 2) * LANES] = jnp.where(keep, odd, 0.0).astype(bf16)

    for i, (c0, o_ref) in enumerate(((C_CMP, cmp_ref), (C_SLC, slc_ref), (C_WIN, win_ref))):
        z = z_of(c0, 2 * LANES)
        k = _rope(_head_rms(unit(z, 0), bd, gn_ref[1 + i:2 + i, :]), cos, sin)
        v = unit(z, 1)
        o_ref[:, 0:LANES] = k
        o_ref[:, LANES:2 * LANES] = v
        if i == 0:
            pool = pool_ref[...]
            kvb_ref[:, 0:LANES] = _dot_split_l(pool, k)
            kvb_ref[:, LANES:2 * LANES] = _dot_split_l(pool, v)

    z = z_of(C_MISC, LANES)
    zf = z + bf_ref[...]
    logf = jnp.minimum(zf, 0.0) - jnp.log1p(jnp.exp(-jnp.abs(zf)))
    misc_ref[...] = jnp.where(lane < MISC_G, _sigmoid(z), jnp.where(lane < MISC_G + H_B, logf, 0.0))

    for c in range(0, 1024, 512):
        z = z_of(C_GATE + c, 512)
        gate_ref[:, c:c + 512] = z * _sigmoid(z)

    z = z_of(C_BQ, H_B * HD)
    for u in range(H_B // 2):
        y = _head_rms(unit(z, u), bd, gn_ref[4:5, :]) * SCALE
        bqal_ref[:, (2 * u) * LANES:(2 * u + 1) * LANES] = jnp.where(lo, y, 0.0).astype(bf16)
        bqal_ref[:, (2 * u + 1) * LANES:(2 * u + 2) * LANES] = jnp.where(lo, 0.0, y).astype(bf16)
    z = z_of(C_BKV, 2 * H_B * HD)
    for u in range(2):
        fkv_ref[:, u * LANES:(u + 1) * LANES] = _head_rms(unit(z, u), bd, gn_ref[5:6, :])
    fkv_ref[:, 2 * LANES:4 * LANES] = z[:, 2 * LANES:4 * LANES]

    z = z_of(C_CQKV, 3 * H_C * HD)
    for u in range(2):
        cqkv_ref[:, u * LANES:(u + 1) * LANES] = _rope(unit(z, u), cos, sin).astype(bf16)
        cqkv_ref[:, (2 + u) * LANES:(3 + u) * LANES] = (_rope(unit(z, 2 + u), cos, sin) * SCALE).astype(bf16)
    cqkv_ref[:, 4 * LANES:6 * LANES] = z[:, 4 * LANES:6 * LANES].astype(bf16)


def _pool_matrix(n_out, n_rows):
    r = np.arange(n_rows)[None, :] // L_CMP == np.arange(n_out)[:, None]
    return jnp.asarray(r.astype(np.float32) / L_CMP, bf16)


def _proj(x2, lw, cos_t, sin_t, tm):
    n = x2.shape[0]
    nt = cos_t.shape[0] // tm
    const = lambda i: (0, 0)
    rows = lambda i: (i, 0)
    outs = [(H_A * LANES, bf16), (2 * LANES, f32), (2 * LANES, f32), (2 * LANES, f32), (LANES, f32),
            (1024, f32), (H_B * LANES, bf16), (4 * LANES, f32), (6 * LANES, bf16)]
    out_shape = [jax.ShapeDtypeStruct((n, w), dt) for w, dt in outs]
    out_specs = [pl.BlockSpec((tm, w), rows) for w, _ in outs]
    out_shape.append(jax.ShapeDtypeStruct((n // L_CMP, 2 * LANES), f32))
    out_specs.append(pl.BlockSpec((tm // L_CMP, 2 * LANES), rows))
    return pl.pallas_call(
        _proj_kernel,
        grid=(n // tm,),
        in_specs=[pl.BlockSpec((tm, 1024), rows),
                  pl.BlockSpec((1, 1024), const),
                  pl.BlockSpec((1024, D_PACK), const),
                  pl.BlockSpec((LANES, LANES), const),
                  pl.BlockSpec((tm // L_CMP, tm), const),
                  pl.BlockSpec((tm, LANES), lambda i: (i % nt, 0)),
                  pl.BlockSpec((tm, LANES), lambda i: (i % nt, 0)),
                  pl.BlockSpec((8, LANES), const),
                  pl.BlockSpec((1, LANES), const)],
        out_specs=out_specs,
        out_shape=out_shape,
        compiler_params=_cparams(("parallel",)),
        name="proj",
    )(x2, lw["norm_w"], lw["w_pack"], lw["bd"], _pool_matrix(tm // L_CMP, tm), cos_t, sin_t,
      lw["gains"], lw["bf"])


def _merge_kernel(x_ref, oa_ref, ob_ref, oc_ref, gate_ref, w_ref, y_ref):
    g = gate_ref[...]
    wa, wb = H_A * HD, H_A * HD + H_B * HD
    acc = _dot((oa_ref[...] * g[:, 0:wa]).astype(bf16), w_ref[0:wa, :])
    acc += _dot((ob_ref[...] * g[:, wa:wb]).astype(bf16), w_ref[wa:wb, :])
    acc += _dot((oc_ref[...] * g[:, wb:]).astype(bf16), w_ref[wb:, :])
    y_ref[...] = x_ref[...] + acc


def _merge(x2, oa, ob, oc, gate, w_out, tm):
    n = x2.shape[0]
    rows = lambda i: (i, 0)
    return pl.pallas_call(
        _merge_kernel,
        grid=(n // tm,),
        in_specs=[pl.BlockSpec((tm, 1024), rows), pl.BlockSpec((tm, 512), rows),
                  pl.BlockSpec((tm, 256), rows), pl.BlockSpec((tm, 256), rows),
                  pl.BlockSpec((tm, 1024), rows), pl.BlockSpec((1024, 1024), lambda i: (0, 0))],
        out_specs=pl.BlockSpec((tm, 1024), rows),
        out_shape=jax.ShapeDtypeStruct((n, 1024), f32),
        compiler_params=_cparams(("parallel",)),
        name="merge",
    )(x2, oa, ob, oc, gate, w_out)


def _ret_prompt_kernel(q_ref, k_ref, v_ref, dm_ref, xi_ref, zeta_ref, gc_ref, gnw_ref, bd_ref,
                       o_ref, r_ref):
    c = RET_CHUNK
    n_chunks = q_ref.shape[1] // c
    lo = _lane((c, LANES)) < HD
    same_head = (_row((LANES, LANES)) < HD) == (_lane((LANES, LANES)) < HD)
    bd = bd_ref[...]
    xi = xi_ref[0]
    zeta = zeta_ref[0]
    gnw = gnw_ref[...]

    def step(i, r):
        rows = pl.ds(pl.multiple_of(i * c, c), c)
        q = q_ref[0, rows, :]
        k = k_ref[0, rows, :]
        v = v_ref[0, rows, :]
        zero = jnp.zeros_like(q)
        a0 = _dot_nt(jnp.where(lo, q, zero), k) * dm_ref[0]
        a1 = _dot_nt(jnp.where(lo, zero, q), k) * dm_ref[1]
        o = jnp.where(lo, _dot(a0.astype(bf16), v), _dot(a1.astype(bf16), v))
        o = o + _dot(q, r.astype(bf16)) * xi
        kz_t = (k.astype(f32) * zeta).T.astype(bf16)
        r = r * gc_ref[0] + jnp.where(same_head, _dot(kz_t, v), 0.0)
        mu = _dot_split(o, bd)
        d = o - mu
        var = _dot_split(d * d, bd)
        o_ref[0, rows, :] = d * lax.rsqrt(var + EPS) * gnw
        return r

    r = lax.fori_loop(0, n_chunks, step, jnp.zeros((LANES, LANES), f32))
    r_ref[0, 0] = r[0:HD, 0:HD]
    r_ref[0, 1] = pltpu.roll(r, HD, 1)[HD:2 * HD, 0:HD]


def _ret_prompt(cqkv, lw, b, t):
    blk = lambda u0: pl.BlockSpec((1, t, LANES), lambda i, p: (i, 0, u0 + p))
    pair = lambda shape: pl.BlockSpec(shape, lambda i, p: (p,) + (0,) * (len(shape) - 1))
    return pl.pallas_call(
        _ret_prompt_kernel,
        grid=(b, 2),
        in_specs=[blk(0), blk(2), blk(4),
                  pair((2, RET_CHUNK, RET_CHUNK)), pair((1, RET_CHUNK, LANES)),
                  pair((1, RET_CHUNK, LANES)), pair((1, 1, LANES)),
                  pl.BlockSpec((1, LANES), lambda i, p: (0, p)),
                  pl.BlockSpec((LANES, LANES), lambda i, p: (0, 0))],
        out_specs=[pl.BlockSpec((1, t, LANES), lambda i, p: (i, 0, p)),
                   pl.BlockSpec((1, 2, HD, HD), lambda i, p: (i, p, 0, 0))],
        out_shape=[jax.ShapeDtypeStruct((b, t, H_C * HD), f32),
                   jax.ShapeDtypeStruct((b, H_C, HD, HD), f32)],
        compiler_params=_cparams(("parallel", "parallel")),
        name="ret_prompt",
    )(cqkv, cqkv, cqkv, lw["ret_dmask"], lw["ret_xi"], lw["ret_zeta"], lw["ret_gc"], lw["ret_gn_w"],
      lw["bd"])


def _fox_prep_kernel(misc_ref, ltri_ref, ccol_ref, crow_ref):
    c = LANES
    n_chunks = misc_ref.shape[1] // c
    ltri = ltri_ref[...]

    def step(i, carry):
        rows = pl.ds(pl.multiple_of(i * c, c), c)
        cs = _dot_split_l(ltri, misc_ref[0, rows, :], 3) + carry
        for p in range(2):
            sh = pltpu.roll(cs, LANES - (MISC_G + 2 * p), 1)
            ccol_ref[0, p, rows, :] = sh
            crow_ref[0, p, i] = sh.T[0:8, :]
        return cs[c - 1:c, :]

    lax.fori_loop(0, n_chunks, step, jnp.zeros((1, LANES), f32))


def _fox_prep(misc3, ltri):
    b, t, _ = misc3.shape
    return pl.pallas_call(
        _fox_prep_kernel,
        grid=(b,),
        in_specs=[pl.BlockSpec((1, t, LANES), lambda i: (i, 0, 0)),
                  pl.BlockSpec((LANES, LANES), lambda i: (0, 0))],
        out_specs=[pl.BlockSpec((1, 2, t, LANES), lambda i: (i, 0, 0, 0)),
                   pl.BlockSpec((1, 2, t // LANES, 8, LANES), lambda i: (i, 0, 0, 0, 0))],
        out_shape=[jax.ShapeDtypeStruct((b, 2, t, LANES), f32),
                   jax.ShapeDtypeStruct((b, 2, t // LANES, 8, LANES), f32)],
        compiler_params=_cparams(("parallel",)),
        name="fox_prep",
    )(misc3, ltri)


def _softmax_step(s, m, l, acc, v):
    m_new = jnp.maximum(m, jnp.max(s, axis=1, keepdims=True))
    alpha = jnp.exp(m - m_new)
    p = jnp.exp(s - m_new)
    l = alpha * l + p
    acc = alpha * acc + _dot(p.astype(bf16), v)
    return m_new, l, acc


def _softmax_finish(l, acc):
    return acc / jnp.sum(l, axis=1, keepdims=True)


def _fox_prompt_kernel(q_ref, k_ref, v_ref, ccol_ref, crow_ref, o_ref, m_scr, l_scr, acc_scr):
    tq = q_ref.shape[1]
    qi = pl.program_id(2)
    q2 = jnp.concatenate([q_ref[0, :, 0:LANES], q_ref[0, :, LANES:2 * LANES]], axis=0)
    cc = ccol_ref[0, 0]
    cq = jnp.broadcast_to(jnp.concatenate([cc[:, 0:1], cc[:, 1:2]], axis=0), (2 * tq, tq))
    m_scr[...] = jnp.full(m_scr.shape, NEG, f32)
    l_scr[...] = jnp.zeros(l_scr.shape, f32)
    acc_scr[...] = jnp.zeros(acc_scr.shape, f32)

    def tile(kt, masked):
        cols = pl.ds(pl.multiple_of(kt * tq, tq), tq)
        k = k_ref[0, cols, :].astype(bf16)
        v = v_ref[0, cols, :].astype(bf16)
        ck = crow_ref[0, 0, kt]
        ck2 = jnp.concatenate([jnp.broadcast_to(ck[0:1, :], (tq, tq)),
                               jnp.broadcast_to(ck[1:2, :], (tq, tq))], axis=0)
        s = _dot_nt(q2, k) + (cq - ck2)
        if masked:
            causal = _lane((tq, tq)) <= _row((tq, tq))
            s = jnp.where(jnp.concatenate([causal, causal], axis=0), s, NEG)
        m, l, acc = _softmax_step(s, m_scr[...], l_scr[...], acc_scr[...], v)
        m_scr[...] = m
        l_scr[...] = l
        acc_scr[...] = acc

    def body(kt, carry):
        tile(kt, False)
        return carry

    lax.fori_loop(0, qi, body, 0)
    tile(qi, True)
    o = _softmax_finish(l_scr[...], acc_scr[...])
    o_ref[0] = jnp.where(_lane((tq, LANES)) < HD, o[0:tq], o[tq:2 * tq])


def _fox_prompt(bqal, fkv, ccol, crow, b, t, tq):
    return pl.pallas_call(
        _fox_prompt_kernel,
        grid=(b, 2, t // tq),
        in_specs=[pl.BlockSpec((1, tq, 2 * LANES), lambda i, p, j: (i, j, p)),
                  pl.BlockSpec((1, t, LANES), lambda i, p, j: (i, 0, p)),
                  pl.BlockSpec((1, t, LANES), lambda i, p, j: (i, 0, 2 + p)),
                  pl.BlockSpec((1, 1, tq, LANES), lambda i, p, j: (i, p, j, 0)),
                  pl.BlockSpec((1, 1, t // tq, 8, tq), lambda i, p, j: (i, p, 0, 0, 0))],
        out_specs=pl.BlockSpec((1, tq, LANES), lambda i, p, j: (i, j, p)),
        out_shape=jax.ShapeDtypeStruct((b, t, H_B * HD), f32),
        scratch_shapes=[pltpu.VMEM((2 * tq, LANES), f32), pltpu.VMEM((2 * tq, LANES), f32),
                        pltpu.VMEM((2 * tq, LANES), f32)],
        compiler_params=_cparams(("parallel", "parallel", "parallel")),
        name="fox_prompt",
    )(bqal, fkv, fkv, ccol, crow)


def _top_n_mask(score, n):
    lane = _lane(score.shape).astype(f32)
    sel = jnp.zeros(score.shape, f32)
    for _ in range(n):
        m = jnp.broadcast_to(jnp.max(score, axis=1, keepdims=True), score.shape)
        first = jnp.min(jnp.where(score == m, lane, float(LANES)), axis=1, keepdims=True)
        pick = lane == jnp.broadcast_to(first, score.shape)
        sel = jnp.where(pick, 1.0, sel)
        score = jnp.where(pick, -2.0, score)
    return sel


def _masked_softmax_rows(s, valid):
    s = jnp.where(valid, s, NEG)
    m = jnp.broadcast_to(jnp.max(s, axis=1, keepdims=True), s.shape)
    m = jnp.where(m > 0.5 * NEG, m, 0.0)
    p = jnp.where(valid, jnp.exp(s - m), 0.0)
    den = jnp.broadcast_to(jnp.sum(p, axis=1, keepdims=True), s.shape)
    return p / jnp.where(den > 0, den, 1.0)


def _select_blocks(imp, cur2):
    lane = _lane(imp.shape)
    imp2 = imp + pltpu.roll(imp, LANES - 1, 1)
    is_blk = (lane & 1) == 0
    valid = is_blk & (lane <= cur2)
    forced = (lane == 0) | (lane == cur2) | (lane == cur2 - 2)
    score = jnp.where(valid, imp2 + jnp.where(forced, FORCE_BONUS, 0.0), -1.0)
    return _top_n_mask(score, TOP_N)


def _softmax_step2(s, m, l, acc, v):
    s_a, s_b = s[:, 0:LANES], s[:, LANES:2 * LANES]
    m_new = jnp.maximum(m, jnp.max(jnp.maximum(s_a, s_b), axis=1, keepdims=True))
    alpha = jnp.exp(m - m_new)
    p_a = jnp.exp(s_a - m_new)
    p_b = jnp.exp(s_b - m_new)
    l = alpha * l + (p_a + p_b)
    pv = _dot(p_a.astype(bf16), v[0:LANES]) + _dot(p_b.astype(bf16), v[LANES:2 * LANES])
    return m_new, l, alpha * acc + pv


def _nsa_prompt_kernel(q_ref, kvb_ref, slc_ref, win_ref, misc_ref, exp_ref, blkend_ref,
                       o_ref, m_scr, l_scr, acc_scr):
    tq = q_ref.shape[1]
    tk = 2 * LANES
    qi = pl.program_id(1)
    t_col = qi * tq + _row((tq, 1))
    t8 = jnp.concatenate([t_col] * H_A, axis=0)
    lane = _lane((tq, LANES))
    lo = lane < HD
    key = _lane((tq, tk))
    misc = misc_ref[0]
    nb = kvb_ref.shape[1]
    zpad = jnp.zeros((LANES - nb, LANES), bf16)
    q8 = jnp.concatenate([q_ref[0, :, g * LANES:(g + 1) * LANES] for g in range(H_A)], axis=0)

    kb = jnp.concatenate([kvb_ref[0, :, 0:LANES].astype(bf16), zpad], axis=0)
    vb = jnp.concatenate([kvb_ref[0, :, LANES:2 * LANES].astype(bf16), zpad], axis=0)
    p = _masked_softmax_rows(_dot_nt(q8, kb), blkend_ref[...] <= t8)
    o_cmp = _dot(p.astype(bf16), vb)
    sel = []
    for kv in range(KV_A):
        imp = p[GQA_A * kv * tq:(GQA_A * kv + 1) * tq]
        for j in range(1, GQA_A):
            imp = imp + p[(GQA_A * kv + j) * tq:(GQA_A * kv + j + 1) * tq]
        sel.append(_select_blocks(imp, (t_col // L_SEL) * 2).astype(bf16))

    def slc_bias(kt):
        causal = (kt * tk + key) <= t_col
        per_kv = [jnp.where((_dot(sel[kv], exp_ref[kt]) > 0.5) & causal, 0.0, NEG) for kv in range(KV_A)]
        return jnp.concatenate([per_kv[g // GQA_A] for g in range(H_A)], axis=0)

    def win_bias(kt):
        d = t_col - (kt * tk + key)
        return jnp.concatenate([jnp.where((d >= 0) & (d < WINDOW), 0.0, NEG)] * H_A, axis=0)

    def flash(kv_ref, kt_lo, kt_hi, bias_fn):
        m_scr[...] = jnp.full(m_scr.shape, NEG, f32)
        l_scr[...] = jnp.zeros(l_scr.shape, f32)
        acc_scr[...] = jnp.zeros(acc_scr.shape, f32)

        def body(kt, carry):
            rows = pl.ds(pl.multiple_of(kt * tk, tk), tk)
            k = kv_ref[0, rows, 0:LANES].astype(bf16)
            v = kv_ref[0, rows, LANES:2 * LANES].astype(bf16)
            s = _dot_nt(q8, k) + bias_fn(kt)
            m, l, acc = _softmax_step2(s, m_scr[...], l_scr[...], acc_scr[...], v)
            m_scr[...] = m
            l_scr[...] = l
            acc_scr[...] = acc
            return carry

        lax.fori_loop(kt_lo, kt_hi, body, 0)
        return _softmax_finish(l_scr[...], acc_scr[...])

    kt_end = (qi * tq) // tk + 1
    o_slc = flash(slc_ref, 0, kt_end, slc_bias)
    o_win = flash(win_ref, jnp.maximum(qi * tq - WINDOW, 0) // tk, kt_end, win_bias)

    heads = []
    for g in range(H_A):
        rows = slice(g * tq, (g + 1) * tq)
        o = (misc[:, 3 * g:3 * g + 1] * o_cmp[rows] + misc[:, 3 * g + 1:3 * g + 2] * o_slc[rows]
             + misc[:, 3 * g + 2:3 * g + 3] * o_win[rows])
        heads.append(o if (g % 2) == g // GQA_A else pltpu.roll(o, HD, 1))
    for u in range(H_A // 2):
        o_ref[0, :, u * LANES:(u + 1) * LANES] = jnp.where(lo, heads[2 * u], heads[2 * u + 1])


def _nsa_prompt(qal, kvb, slc, win, misc3, consts, b, t, tq):
    nb = t // L_CMP
    tk = 2 * LANES
    return pl.pallas_call(
        _nsa_prompt_kernel,
        grid=(b, t // tq),
        in_specs=[pl.BlockSpec((1, tq, H_A * LANES), lambda i, j: (i, j, 0)),
                  pl.BlockSpec((1, nb, 2 * LANES), lambda i, j: (i, 0, 0)),
                  pl.BlockSpec((1, t, 2 * LANES), lambda i, j: (i, 0, 0)),
                  pl.BlockSpec((1, t, 2 * LANES), lambda i, j: (i, 0, 0)),
                  pl.BlockSpec((1, tq, LANES), lambda i, j: (i, j, 0)),
                  pl.BlockSpec((t // tk, LANES, tk), lambda i, j: (0, 0, 0)),
                  pl.BlockSpec((1, LANES), lambda i, j: (0, 0))],
        out_specs=pl.BlockSpec((1, tq, H_A * HD), lambda i, j: (i, j, 0)),
        out_shape=jax.ShapeDtypeStruct((b, t, H_A * HD), f32),
        scratch_shapes=[pltpu.VMEM((H_A * tq, LANES), f32), pltpu.VMEM((H_A * tq, LANES), f32),
                        pltpu.VMEM((H_A * tq, LANES), f32)],
        compiler_params=_cparams(("parallel", "parallel")),
        name="nsa_prompt",
    )(qal, kvb, slc, win, misc3, consts["expand"], consts["blkend"])


def _rope_tables(pos):
    half = HD // 2
    inv = ROPE_THETA ** (-jnp.arange(half, dtype=f32) / half)
    ang = pos.astype(f32)[:, None] * inv[None, :]
    cos, sin = jnp.cos(ang), jnp.sin(ang)
    return (jnp.concatenate([cos, cos, cos, cos], axis=-1),
            jnp.concatenate([-sin, sin, -sin, sin], axis=-1))


def _constants(t, tq):
    lane = np.arange(LANES)
    bd = ((lane[:, None] // HD) == (lane[None, :] // HD)).astype(np.float32) / HD
    ltri = (lane[None, :] <= lane[:, None]).astype(np.float32)
    key = np.arange(t).reshape(t // tq, 1, tq)
    expand = ((lane[None, :, None] % 2 == 0) & (key // L_SEL == lane[None, :, None] // 2)).astype(np.float32)
    blkend = np.where(lane < t // L_CMP, (lane + 1) * L_CMP - 1, 2 ** 30).astype(np.int32)[None, :]
    log_g = jnp.log1p(-jnp.exp2(-5.0 - jnp.arange(H_C, dtype=f32)))
    i = jnp.arange(RET_CHUNK, dtype=f32)
    diff = i[:, None] - i[None, :]
    dmask = jnp.where(diff >= 0, jnp.exp(jnp.maximum(diff, 0.0) * log_g[:, None, None]), 0.0)
    xi = jnp.exp((i + 1.0)[None, :] * log_g[:, None])
    zeta = jnp.exp((RET_CHUNK - 1.0 - i)[None, :] * log_g[:, None])
    g_c = jnp.exp(RET_CHUNK * log_g)
    by_lane = lambda a: jnp.repeat(a.reshape(2, 2, -1).transpose(0, 2, 1), HD, axis=-1)
    return {"bd": jnp.asarray(bd, bf16), "pool8": _pool_matrix(8, 2 * PAGE),
            "ltri": jnp.asarray(ltri, bf16),
            "expand": jnp.asarray(expand, bf16), "blkend": jnp.asarray(blkend),
            "ret_dmask": dmask, "ret_xi": by_lane(xi), "ret_zeta": by_lane(zeta),
            "ret_gc": by_lane(g_c[:, None]), "log_g": log_g}


def _pack_w_in(w):
    off = np.cumsum([0, 512, 128, 128, 128, 128, 128, 128, 24, 512, 256, 256, 256, 4, 256, 256, 256, 256, 256])
    seg = lambda i: w[:, off[i]:off[i + 1]]
    (a_q, a_kc, a_vc, a_ks, a_vs, a_kw, a_vw, a_g, a_gate,
     b_q, b_k, b_v, b_f, b_gate, c_q, c_k, c_v, c_gate) = [seg(i) for i in range(18)]
    pad = jnp.zeros((w.shape[0], LANES - MISC_G - H_B), w.dtype)
    return jnp.concatenate([a_q, a_kc, a_vc, a_ks, a_vs, a_kw, a_vw, a_g, b_f, pad,
                            a_gate, b_gate, c_gate, b_q, b_k, b_v, c_q, c_k, c_v], axis=1).astype(bf16)


def _layer_params(l, consts, norm_w, w_in, fox_bf, nsa_q_norm, nsa_k_norm, fox_q_norm, fox_k_norm,
                  ret_gn_w, w_out):
    two = lambda g: jnp.concatenate([g, g])[None, :]
    gains = jnp.concatenate([two(nsa_q_norm[l]), two(nsa_k_norm[l, 0]), two(nsa_k_norm[l, 1]),
                             two(nsa_k_norm[l, 2]), two(fox_q_norm[l]), two(fox_k_norm[l]),
                             jnp.zeros((2, LANES), f32)], axis=0)
    bf = jnp.zeros((1, LANES), f32).at[0, MISC_G:MISC_G + H_B].set(fox_bf[l])
    lw = dict(consts)
    lw.update(norm_w=norm_w[l][None, :], w_pack=_pack_w_in(w_in[l]), gains=gains, bf=bf,
              ret_gn_w=ret_gn_w[l][None, :], w_out=w_out[l].astype(bf16))
    return lw


def _prompt_layer(x, lw, consts, cos_t, sin_t):
    b, t, d = x.shape
    tq = 128
    x2 = x.reshape(b * t, d)
    qal, cmp_, slc, win, misc, gate, bqal, fkv, cqkv, kvb = _proj(x2, lw, cos_t, sin_t, 256)
    r3 = lambda a: a.reshape(b, t, a.shape[-1])
    misc3 = r3(misc)
    o_a = _nsa_prompt(r3(qal), kvb.reshape(b, t // L_CMP, 2 * LANES), r3(slc), r3(win), misc3,
                      consts, b, t, tq)
    ccol, crow = _fox_prep(misc3, consts["ltri"])
    o_b = _fox_prompt(r3(bqal), r3(fkv), ccol, crow, b, t, tq)
    o_c, r_new = _ret_prompt(r3(cqkv), lw, b, t)
    y = _merge(x2, o_a.reshape(b * t, -1), o_b.reshape(b * t, -1), o_c.reshape(b * t, -1), gate,
               lw["w_out"], 256)
    wk = min(WINDOW, t)
    states = (cmp_.reshape(b, t, 2, KV_A, HD), slc.reshape(b, t, 2, KV_A, HD),
              fkv.reshape(b, t, 2, H_B, HD), misc3[:, :, MISC_G:MISC_G + H_B],
              r3(win)[:, t - wk:].reshape(b, wk, 2, KV_A, HD), r_new)
    return y.reshape(b, t, d), states


def _head_col(vec_row, shape, base, stride):
    pick = _lane(shape) == base + stride * _row(shape)
    return jnp.sum(jnp.where(pick, jnp.broadcast_to(vec_row, shape), 0.0), axis=1, keepdims=True)


def _dec_nsa_kernel(pt_ref, q_ref, slc_new_ref, win_new_ref, misc_ref, win_ref, pool_ref, *rest,
                    group, n_pages):
    cmp_pages = rest[0:group]
    slc_pages = rest[group:2 * group]
    o_ref, kvb_scr, m_scr, l_scr, o_scr = rest[2 * group:]
    pg = pl.program_id(1)
    shape = (H_A, LANES)
    lane = _lane(shape)
    row = _row(shape)
    lo = lane < HD
    q8 = q_ref[0]
    past = n_pages * PAGE
    nb = past // L_CMP
    cur2 = 2 * (past // L_SEL)

    @pl.when(pg == 0)
    def _():
        m_scr[...] = jnp.full(shape, NEG, f32)
        l_scr[...] = jnp.zeros(shape, f32)

    for g in range(0, group, 2):
        pooled = (_dot_split_l(pool_ref[:, 0:PAGE], cmp_pages[g][0, 0])
                  + _dot_split_l(pool_ref[:, PAGE:2 * PAGE], cmp_pages[g + 1][0, 0]))
        kvb_scr[pl.ds(pl.multiple_of((pg * group + g) * (PAGE // L_CMP), 8), 8), :] = pooled

    for g in range(group):
        page = pg * group + g
        k = slc_pages[g][0, 0, :, 0:LANES].astype(bf16)
        v = slc_pages[g][0, 0, :, LANES:2 * LANES].astype(bf16)
        s = _dot_nt(q8, k)
        m_lo = jnp.max(jnp.where(lo, s, NEG), axis=1, keepdims=True)
        m_hi = jnp.max(jnp.where(lo, NEG, s), axis=1, keepdims=True)
        p = jnp.exp(s - jnp.where(lo, m_lo, m_hi))
        p_lo = jnp.where(lo, p, 0.0)
        p_hi = jnp.where(lo, 0.0, p)
        o2 = _dot(jnp.concatenate([p_lo, p_hi], axis=0).astype(bf16), v)
        at_lo = lane == 4 * page
        at_hi = lane == 4 * page + 2
        m_scr[...] = jnp.where(at_lo, m_lo, jnp.where(at_hi, m_hi, m_scr[...]))
        l_scr[...] = jnp.where(at_lo, jnp.sum(p_lo, axis=1, keepdims=True),
                               jnp.where(at_hi, jnp.sum(p_hi, axis=1, keepdims=True), l_scr[...]))
        o_scr[2 * page] = o2[0:H_A]
        o_scr[2 * page + 1] = o2[H_A:2 * H_A]

    @pl.when(pg == pl.num_programs(1) - 1)
    def _():
        qf = q8.astype(f32)
        zpad = jnp.zeros((LANES - nb, LANES), bf16)
        kb = jnp.concatenate([kvb_scr[:, 0:LANES].astype(bf16), zpad], axis=0)
        vb = jnp.concatenate([kvb_scr[:, LANES:2 * LANES].astype(bf16), zpad], axis=0)
        p = _masked_softmax_rows(_dot_nt(q8, kb), lane < nb)
        o_cmp = _dot(p.astype(bf16), vb)
        imp = jnp.where(row < GQA_A,
                        jnp.sum(p[0:GQA_A], axis=0, keepdims=True),
                        jnp.sum(p[GQA_A:2 * GQA_A], axis=0, keepdims=True))
        sel = _select_blocks(imp, cur2) > 0.5
        bf_round = lambda a: a.astype(bf16).astype(f32)
        k_new = bf_round(slc_new_ref[0, :, 0:LANES])
        v_new = bf_round(slc_new_ref[0, :, LANES:2 * LANES])
        s_new = jnp.sum(qf * k_new, axis=1, keepdims=True)
        m_all = jnp.where(lane == cur2, s_new, m_scr[...])
        l_all = jnp.where(lane == cur2, 1.0, l_scr[...])
        m_top = jnp.max(jnp.where(sel, m_all, NEG), axis=1, keepdims=True)
        w = jnp.where(sel, jnp.exp(m_all - m_top), 0.0)
        den = jnp.sum(w * l_all, axis=1, keepdims=True)
        acc = w[:, cur2:cur2 + 1] * v_new
        for j in range(past // L_SEL):
            acc = acc + w[:, 2 * j:2 * j + 1] * o_scr[j]
        o_slc = acc / den
        wk = win_ref.shape[2]
        kw = win_ref[0, 0, :, 0:LANES].astype(bf16)
        vw = win_ref[0, 0, :, LANES:2 * LANES].astype(bf16)
        kw_new = bf_round(win_new_ref[0, :, 0:LANES])
        vw_new = bf_round(win_new_ref[0, :, LANES:2 * LANES])
        s = jnp.where(_lane((H_A, wk)) > wk - WINDOW, _dot_nt(q8, kw), NEG)
        s_new = jnp.sum(qf * kw_new, axis=1, keepdims=True)
        m = jnp.maximum(jnp.max(s, axis=1, keepdims=True), s_new)
        pw = jnp.exp(s - m)
        pw_new = jnp.exp(s_new - m)
        o_win = ((_dot(pw.astype(bf16), vw) + pw_new * vw_new)
                 / (jnp.sum(pw, axis=1, keepdims=True) + pw_new))
        misc = misc_ref[0]
        o8 = (_head_col(misc, shape, 0, 3) * o_cmp + _head_col(misc, shape, 1, 3) * o_slc
              + _head_col(misc, shape, 2, 3) * o_win)
        o8r = pltpu.roll(o8, HD, 1)
        lo1 = _lane((1, LANES)) < HD
        for u in range(H_A // 2):
            kv_lo = (2 * u) // GQA_A == 0
            even = (o8 if kv_lo else o8r)[2 * u:2 * u + 1, :]
            odd = (o8r if kv_lo else o8)[2 * u + 1:2 * u + 2, :]
            o_ref[0, :, u * LANES:(u + 1) * LANES] = jnp.where(lo1, even, odd)


def _page_specs(block, layer, group, count):
    def spec(g):
        return pl.BlockSpec(block, lambda b, pg, pt: (layer, pt[b * count + pg * group + g]) + (0,) * (len(block) - 2))
    return [spec(g) for g in range(group)]


def _dec_nsa(pt_flat, q8, slc_new, win_new, misc, win_state, cache_cmp, cache_slc, pool8, layer, group):
    n_seq = q8.shape[0]
    n_pages = pt_flat.shape[0] // n_seq
    wk = win_state.shape[2]
    seq3 = lambda w: pl.BlockSpec((1, 1, w), lambda b, pg, pt: (b, 0, 0))
    page_block = (1, 1, PAGE, 2 * LANES)
    grid_spec = pltpu.PrefetchScalarGridSpec(
        num_scalar_prefetch=1,
        grid=(n_seq, n_pages // group),
        in_specs=[pl.BlockSpec((1, H_A, LANES), lambda b, pg, pt: (b, 0, 0)),
                  seq3(2 * LANES), seq3(2 * LANES), seq3(LANES),
                  pl.BlockSpec((1, 1, wk, 2 * LANES), lambda b, pg, pt: (layer, b, 0, 0)),
                  pl.BlockSpec((8, 2 * PAGE), lambda b, pg, pt: (0, 0))]
        + _page_specs(page_block, layer, group, n_pages) + _page_specs(page_block, layer, group, n_pages),
        out_specs=pl.BlockSpec((1, 1, H_A * HD), lambda b, pg, pt: (b, 0, 0)),
        scratch_shapes=[pltpu.VMEM((n_pages * PAGE // L_CMP, 2 * LANES), f32),
                        pltpu.VMEM((H_A, LANES), f32), pltpu.VMEM((H_A, LANES), f32),
                        pltpu.VMEM((n_pages * PAGE // L_SEL, H_A, LANES), f32)])
    return pl.pallas_call(
        functools.partial(_dec_nsa_kernel, group=group, n_pages=n_pages),
        grid_spec=grid_spec,
        out_shape=jax.ShapeDtypeStruct((n_seq, 1, H_A * HD), f32),
        compiler_params=_cparams(("parallel", "arbitrary")),
        name="dec_nsa",
    )(pt_flat, q8, slc_new, win_new, misc, win_state, pool8, *([cache_cmp] * group), *([cache_slc] * group))


def _lane_cumsum(x):
    lane = _lane(x.shape)
    sh = 1
    while sh < LANES:
        x = x + jnp.where(lane >= sh, pltpu.roll(x, sh, 1), 0.0)
        sh *= 2
    return x


def _dec_fox_kernel(pt_ref, q_ref, kv_new_ref, misc_ref, *rest, group):
    kv_pages = rest[0:group]
    lf_pages = rest[group:2 * group]
    o_ref, m_scr, l_scr, c_scr, acc0_scr, acc1_scr = rest[2 * group:]
    pg = pl.program_id(1)
    shape = (8, LANES)
    first_pair = _row(shape) < 2
    q8 = q_ref[0]

    @pl.when(pg == 0)
    def _():
        m_scr[...] = jnp.full(m_scr.shape, NEG, f32)
        l_scr[...] = jnp.zeros(l_scr.shape, f32)
        c_scr[...] = jnp.zeros(c_scr.shape, f32)
        acc0_scr[...] = jnp.zeros(shape, f32)
        acc1_scr[...] = jnp.zeros(shape, f32)

    for g in range(group):
        kv = kv_pages[g]
        c = c_scr[...] + _lane_cumsum(lf_pages[g][0, 0])
        c_scr[...] = c[:, LANES - 1:LANES]
        s = jnp.where(first_pair, _dot_nt(q8, kv[0, 0, :, 0:LANES].astype(bf16)),
                      _dot_nt(q8, kv[0, 0, :, LANES:2 * LANES].astype(bf16))) - c
        m = m_scr[...]
        m_new = jnp.maximum(m, jnp.max(s, axis=1, keepdims=True))
        alpha = jnp.exp(m - m_new)
        p = jnp.exp(s - m_new)
        pb = p.astype(bf16)
        m_scr[...] = m_new
        l_scr[...] = alpha * l_scr[...] + jnp.sum(p, axis=1, keepdims=True)
        acc0_scr[...] = alpha * acc0_scr[...] + _dot(pb, kv[0, 0, :, 2 * LANES:3 * LANES].astype(bf16))
        acc1_scr[...] = alpha * acc1_scr[...] + _dot(pb, kv[0, 0, :, 3 * LANES:4 * LANES].astype(bf16))

    @pl.when(pg == pl.num_programs(1) - 1)
    def _():
        bf_round = lambda a: jnp.broadcast_to(a.astype(bf16).astype(f32), shape)
        new = kv_new_ref[0]
        k_new = jnp.where(first_pair, bf_round(new[:, 0:LANES]), bf_round(new[:, LANES:2 * LANES]))
        v_new = jnp.where(first_pair, bf_round(new[:, 2 * LANES:3 * LANES]), bf_round(new[:, 3 * LANES:4 * LANES]))
        s_new = (jnp.sum(q8.astype(f32) * k_new, axis=1, keepdims=True)
                 - (c_scr[...] + _head_col(misc_ref[0], shape, MISC_G, 1)))
        m = m_scr[...]
        m_new = jnp.maximum(m, s_new)
        alpha = jnp.exp(m - m_new)
        p_new = jnp.exp(s_new - m_new)
        acc = jnp.where(first_pair, acc0_scr[...], acc1_scr[...])
        o8 = (alpha * acc + p_new * v_new) / (alpha * l_scr[...] + p_new)
        lo1 = _lane((1, LANES)) < HD
        for u in range(H_B // 2):
            o_ref[0, :, u * LANES:(u + 1) * LANES] = jnp.where(lo1, o8[2 * u:2 * u + 1, :],
                                                               o8[2 * u + 1:2 * u + 2, :])


def _dec_fox(pt_flat, qb4, fkv_new, misc, cache_fkv, cache_lft, layer, group):
    n_seq = qb4.shape[0]
    n_pages = pt_flat.shape[0] // n_seq
    seq3 = lambda w: pl.BlockSpec((1, 1, w), lambda b, pg, pt: (b, 0, 0))
    grid_spec = pltpu.PrefetchScalarGridSpec(
        num_scalar_prefetch=1,
        grid=(n_seq, n_pages // group),
        in_specs=[pl.BlockSpec((1, 8, LANES), lambda b, pg, pt: (b, 0, 0)), seq3(4 * LANES), seq3(LANES)]
        + _page_specs((1, 1, PAGE, 4 * LANES), layer, group, n_pages)
        + _page_specs((1, 1, 8, PAGE), layer, group, n_pages),
        out_specs=pl.BlockSpec((1, 1, H_B * HD), lambda b, pg, pt: (b, 0, 0)),
        scratch_shapes=[pltpu.VMEM((8, 1), f32), pltpu.VMEM((8, 1), f32), pltpu.VMEM((8, 1), f32),
                        pltpu.VMEM((8, LANES), f32), pltpu.VMEM((8, LANES), f32)])
    return pl.pallas_call(
        functools.partial(_dec_fox_kernel, group=group),
        grid_spec=grid_spec,
        out_shape=jax.ShapeDtypeStruct((n_seq, 1, H_B * HD), f32),
        compiler_params=_cparams(("parallel", "arbitrary")),
        name="dec_fox",
    )(pt_flat, qb4, fkv_new, misc, *([cache_fkv] * group), *([cache_lft] * group))


def _dec_ret_kernel(q_ref, k_ref, v_ref, r_ref, gam_ref, gnw_ref, o_ref, rn_ref):
    q = q_ref[0].astype(f32)
    k = k_ref[0].astype(f32)
    v = v_ref[0].astype(f32)
    qk = jnp.sum(q * k, axis=1, keepdims=True)
    eye = _row((HD, HD)) == _lane((HD, HD))
    col = lambda a: jnp.sum(jnp.where(eye, jnp.broadcast_to(a, (HD, HD)), 0.0), axis=1, keepdims=True)
    for h in range(H_C):
        hs = slice(h, h + 1)
        r = r_ref[0, 0, h]
        gam = gam_ref[hs, :]
        o = qk[hs, :] * v[hs, :] + jnp.sum(col(q[hs, :]) * r, axis=0, keepdims=True) * gam
        rn_ref[0, h] = r * gam + col(k[hs, :]) * v[hs, :]
        d = o - jnp.mean(o, axis=1, keepdims=True)
        var = jnp.mean(d * d, axis=1, keepdims=True)
        o_ref[0, hs, :] = d * lax.rsqrt(var + EPS) * gnw_ref[hs, :]


def _dec_ret(cq, ck, cv, state_ret, gam, gnw, layer):
    n_seq = cq.shape[0]
    head = pl.BlockSpec((1, H_C, HD), lambda b: (b, 0, 0))
    table = pl.BlockSpec((H_C, HD), lambda b: (0, 0))
    return pl.pallas_call(
        _dec_ret_kernel,
        grid=(n_seq,),
        in_specs=[head, head, head,
                  pl.BlockSpec((1, 1, H_C, HD, HD), lambda b: (layer, b, 0, 0, 0)), table, table],
        out_specs=[head, pl.BlockSpec((1, H_C, HD, HD), lambda b: (b, 0, 0, 0))],
        out_shape=[jax.ShapeDtypeStruct((n_seq, H_C, HD), f32),
                   jax.ShapeDtypeStruct((n_seq, H_C, HD, HD), f32)],
        compiler_params=_cparams(("parallel",)),
        name="dec_ret",
    )(cq, ck, cv, state_ret, gam, gnw)


def _sample_layer(x, layer, lw, consts, caches, pt_flat, cos_t, sin_t, group):
    n, _, d = x.shape
    x2 = x.reshape(n, d)
    qal, cmp_, slc, win, misc, gate, bqal, fkv, cqkv, _ = _proj(x2, lw, cos_t, sin_t, n)
    row3 = lambda a: a[:, None, :]
    o_a8 = _dec_nsa(pt_flat, qal.reshape(n, H_A, LANES), row3(slc), row3(win), row3(misc),
                    caches["win"], caches["cmp"], caches["slc"], consts["pool8"], layer, group)
    qb8 = jnp.pad(bqal.reshape(n, H_B, LANES), ((0, 0), (0, 8 - H_B), (0, 0)))
    o_b8 = _dec_fox(pt_flat, qb8, row3(fkv), row3(misc),
                    caches["fkv"], caches["lft"], layer, group)
    heads = lambda u: cqkv[:, 2 * u * LANES:(2 * u + 2) * LANES].reshape(n, H_C, HD).astype(f32)
    gam = jnp.broadcast_to(jnp.exp(consts["log_g"])[:, None], (H_C, HD))
    o_c, r_new = _dec_ret(heads(0), heads(1), heads(2), caches["ret"], gam,
                          lw["ret_gn_w"].reshape(H_C, HD), layer)
    y = _merge(x2, o_a8.reshape(n, H_A * HD), o_b8.reshape(n, H_B * HD), o_c.reshape(n, H_C * HD), gate,
               lw["w_out"], n)
    new_win = jnp.concatenate([caches["win"][layer][:, 1:], win[:, None, :]], axis=1)
    wk = new_win.shape[1]
    states = (cmp_.reshape(n, 1, 2, KV_A, HD), slc.reshape(n, 1, 2, KV_A, HD),
              fkv.reshape(n, 1, 2, H_B, HD), misc[:, None, MISC_G:MISC_G + H_B],
              new_win.reshape(n, wk, 2, KV_A, HD), r_new)
    return y.reshape(n, 1, d), states


def kernel(x_prompt, x_sample, cache_nsa_cmp, cache_nsa_slc, cache_fox_kv, cache_fox_logf,
           state_nsa_win, state_ret, page_table, norm_w, w_in, fox_bf, nsa_q_norm, nsa_k_norm,
           fox_q_norm, fox_k_norm, ret_gn_w, w_out):
    depth = norm_w.shape[0]
    b, t, _ = x_prompt.shape
    n_seq, n_pages = page_table.shape
    assert x_sample.shape[1] == 1 and t % 256 == 0 and n_pages % 2 == 0
    past = n_pages * PAGE
    consts = _constants(t, 2 * LANES)
    cos_p, sin_p = _rope_tables(jnp.arange(t, dtype=jnp.int32))
    cos_s, sin_s = _rope_tables(jnp.full((n_seq,), past, dtype=jnp.int32))
    n_phys = cache_nsa_cmp.shape[1]
    caches = {"cmp": cache_nsa_cmp.reshape(depth, n_phys, PAGE, 2 * LANES),
              "slc": cache_nsa_slc.reshape(depth, n_phys, PAGE, 2 * LANES),
              "fkv": cache_fox_kv.reshape(depth, n_phys, PAGE, 4 * LANES),
              "lft": jnp.pad(jnp.swapaxes(cache_fox_logf, 2, 3), ((0, 0), (0, 0), (0, 8 - H_B), (0, 0))),
              "win": state_nsa_win.reshape(depth, n_seq, -1, 2 * LANES),
              "ret": state_ret}
    pt_flat = page_table.reshape(-1)
    yp, ys = x_prompt, x_sample
    p_states, s_states = [], []
    for l in range(depth):
        lw = _layer_params(l, consts, norm_w, w_in, fox_bf, nsa_q_norm, nsa_k_norm, fox_q_norm,
                           fox_k_norm, ret_gn_w, w_out)
        yp, ps = _prompt_layer(yp, lw, consts, cos_p, sin_p)
        ys, ss = _sample_layer(ys, l, lw, consts, caches, pt_flat, cos_s, sin_s, 2)
        p_states.append(ps)
        s_states.append(ss)
    stk = lambda states, i: jnp.stack([st[i] for st in states], axis=0)
    out = [yp, ys]
    for i in range(6):
        out += [stk(p_states, i), stk(s_states, i)]
    return tuple(out)
```

```python
import functools

import numpy as np
import jax
import jax.numpy as jnp
from jax import lax
from jax.experimental import pallas as pl
from jax.experimental.pallas import tpu as pltpu

f32 = jnp.float32
bf16 = jnp.bfloat16

HD = 64
H_A, KV_A, GQA_A = 8, 2, 4
H_B, H_C = 4, 4
L_CMP, L_SEL, TOP_N, WINDOW = 32, 64, 8, 512
PAGE = 128
RET_CHUNK = 128
ROPE_THETA = 10000.0
EPS = 1e-6
SCALE = HD ** -0.5
FORCE_BONUS = 1e3
NEG = -1e30

LANES = 128
VMEM_LIMIT = 48 * 1024 * 1024

C_AQ, C_CMP, C_SLC, C_WIN, C_MISC, C_GATE, C_BQ, C_BKV, C_CQKV, D_PACK = (
    0, 512, 768, 1024, 1280, 1408, 2432, 2688, 3200, 3968)
MISC_G = 3 * H_A


def _cparams(sem):
    return pltpu.CompilerParams(dimension_semantics=sem, vmem_limit_bytes=VMEM_LIMIT)


def _dot(a, b):
    return jnp.dot(a, b, preferred_element_type=f32)


def _dot_nt(a, b):
    return lax.dot_general(a, b, (((1,), (1,)), ((), ())), preferred_element_type=f32)


def _split(x, n):
    parts = []
    r = x
    for i in range(n):
        h = r.astype(bf16)
        parts.append(h)
        if i + 1 < n:
            r = r - h.astype(f32)
    return parts


def _dot_split(x, w, n=2):
    acc = None
    for h in _split(x, n):
        d = _dot(h, w)
        acc = d if acc is None else acc + d
    return acc


def _dot_split_l(w, x, n=2):
    acc = None
    for h in _split(x, n):
        d = _dot(w, h)
        acc = d if acc is None else acc + d
    return acc


def _lane(shape):
    return lax.broadcasted_iota(jnp.int32, shape, len(shape) - 1)


def _row(shape):
    return lax.broadcasted_iota(jnp.int32, shape, len(shape) - 2)


def _swap_half_heads(y):
    lane = _lane(y.shape)
    return jnp.where((lane & 32) == 0, pltpu.roll(y, 96, 1), pltpu.roll(y, 32, 1))


def _head_rms(z, bd, g):
    ms = _dot_split(z * z, bd)
    return z * lax.rsqrt(ms + EPS) * g


def _rope(y, cos, sin):
    return y * cos + _swap_half_heads(y) * sin


def _sigmoid(z):
    return 1.0 / (1.0 + jnp.exp(-z))


def _proj_kernel(x_ref, nw_ref, w_ref, bd_ref, pool_ref, cos_ref, sin_ref, gn_ref, bf_ref,
                 qal_ref, cmp_ref, slc_ref, win_ref, misc_ref, gate_ref, bqal_ref, fkv_ref,
                 cqkv_ref, kvb_ref):
    x = x_ref[...]
    ms = jnp.mean(x * x, axis=-1, keepdims=True)
    xn = (x * lax.rsqrt(ms + EPS) * nw_ref[...]).astype(bf16)
    bd = bd_ref[...]
    cos = cos_ref[...]
    sin = sin_ref[...]
    lane = _lane((x.shape[0], LANES))
    lo = lane < HD

    def z_of(c0, width):
        return _dot(xn, w_ref[:, c0:c0 + width])

    def unit(z, u):
        return z[:, u * LANES:(u + 1) * LANES]

    z = z_of(C_AQ, H_A * HD)
    g_aq = gn_ref[0:1, :]
    for u in range(H_A // 2):
        y = _rope(_head_rms(unit(z, u), bd, g_aq), cos, sin) * SCALE
        yr = pltpu.roll(y, HD, 1)
        kv_lo = (2 * u) // GQA_A == 0
        keep = lo if kv_lo else jnp.logical_not(lo)
        even, odd = (y, yr) if kv_lo else (yr, y)
        qal_ref[:, (2 * u) * LANES:(2 * u + 1) * LANES] = jnp.where(keep, even, 0.0).astype(bf16)
        qal_ref[:, (2 * u + 1) * LANES:(2 * u + 2) * LANES] = jnp.where(keep, odd, 0.0).astype(bf16)

    for i, (c0, o_ref) in enumerate(((C_CMP, cmp_ref), (C_SLC, slc_ref), (C_WIN, win_ref))):
        z = z_of(c0, 2 * LANES)
        k = _rope(_head_rms(unit(z, 0), bd, gn_ref[1 + i:2 + i, :]), cos, sin)
        v = unit(z, 1)
        o_ref[:, 0:LANES] = k
        o_ref[:, LANES:2 * LANES] = v
        if i == 0:
            pool = pool_ref[...]
            kvb_ref[:, 0:LANES] = _dot_split_l(pool, k)
            kvb_ref[:, LANES:2 * LANES] = _dot_split_l(pool, v)

    z = z_of(C_MISC, LANES)
    zf = z + bf_ref[...]
    logf = jnp.minimum(zf, 0.0) - jnp.log1p(jnp.exp(-jnp.abs(zf)))
    misc_ref[...] = jnp.where(lane < MISC_G, _sigmoid(z), jnp.where(lane < MISC_G + H_B, logf, 0.0))

    for c in range(0, 1024, 512):
        z = z_of(C_GATE + c, 512)
        gate_ref[:, c:c + 512] = z * _sigmoid(z)

    z = z_of(C_BQ, H_B * HD)
    for u in range(H_B // 2):
        y = _head_rms(unit(z, u), bd, gn_ref[4:5, :]) * SCALE
        bqal_ref[:, (2 * u) * LANES:(2 * u + 1) * LANES] = jnp.where(lo, y, 0.0).astype(bf16)
        bqal_ref[:, (2 * u + 1) * LANES:(2 * u + 2) * LANES] = jnp.where(lo, 0.0, y).astype(bf16)
    z = z_of(C_BKV, 2 * H_B * HD)
    for u in range(2):
        fkv_ref[:, u * LANES:(u + 1) * LANES] = _head_rms(unit(z, u), bd, gn_ref[5:6, :])
    fkv_ref[:, 2 * LANES:4 * LANES] = z[:, 2 * LANES:4 * LANES]

    z = z_of(C_CQKV, 3 * H_C * HD)
    for u in range(2):
        cqkv_ref[:, u * LANES:(u + 1) * LANES] = _rope(unit(z, u), cos, sin).astype(bf16)
        cqkv_ref[:, (2 + u) * LANES:(3 + u) * LANES] = (_rope(unit(z, 2 + u), cos, sin) * SCALE).astype(bf16)
    cqkv_ref[:, 4 * LANES:6 * LANES] = z[:, 4 * LANES:6 * LANES].astype(bf16)


def _pool_matrix(n_out, n_rows):
    r = np.arange(n_rows)[None, :] // L_CMP == np.arange(n_out)[:, None]
    return jnp.asarray(r.astype(np.float32) / L_CMP, bf16)


def _proj(x2, lw, cos_t, sin_t, tm):
    n = x2.shape[0]
    nt = cos_t.shape[0] // tm
    const = lambda i: (0, 0)
    rows = lambda i: (i, 0)
    outs = [(H_A * LANES, bf16), (2 * LANES, f32), (2 * LANES, f32), (2 * LANES, f32), (LANES, f32),
            (1024, f32), (H_B * LANES, bf16), (4 * LANES, f32), (6 * LANES, bf16)]
    out_shape = [jax.ShapeDtypeStruct((n, w), dt) for w, dt in outs]
    out_specs = [pl.BlockSpec((tm, w), rows) for w, _ in outs]
    out_shape.append(jax.ShapeDtypeStruct((n // L_CMP, 2 * LANES), f32))
    out_specs.append(pl.BlockSpec((tm // L_CMP, 2 * LANES), rows))
    return pl.pallas_call(
        _proj_kernel,
        grid=(n // tm,),
        in_specs=[pl.BlockSpec((tm, 1024), rows),
                  pl.BlockSpec((1, 1024), const),
                  pl.BlockSpec((1024, D_PACK), const),
                  pl.BlockSpec((LANES, LANES), const),
                  pl.BlockSpec((tm // L_CMP, tm), const),
                  pl.BlockSpec((tm, LANES), lambda i: (i % nt, 0)),
                  pl.BlockSpec((tm, LANES), lambda i: (i % nt, 0)),
                  pl.BlockSpec((8, LANES), const),
                  pl.BlockSpec((1, LANES), const)],
        out_specs=out_specs,
        out_shape=out_shape,
        compiler_params=_cparams(("parallel",)),
        name="proj",
    )(x2, lw["norm_w"], lw["w_pack"], lw["bd"], _pool_matrix(tm // L_CMP, tm), cos_t, sin_t,
      lw["gains"], lw["bf"])


def _merge_kernel(x_ref, oa_ref, ob_ref, oc_ref, gate_ref, w_ref, y_ref):
    g = gate_ref[...]
    wa, wb = H_A * HD, H_A * HD + H_B * HD
    acc = _dot((oa_ref[...] * g[:, 0:wa]).astype(bf16), w_ref[0:wa, :])
    acc += _dot((ob_ref[...] * g[:, wa:wb]).astype(bf16), w_ref[wa:wb, :])
    acc += _dot((oc_ref[...] * g[:, wb:]).astype(bf16), w_ref[wb:, :])
    y_ref[...] = x_ref[...] + acc


def _merge(x2, oa, ob, oc, gate, w_out, tm):
    n = x2.shape[0]
    rows = lambda i: (i, 0)
    return pl.pallas_call(
        _merge_kernel,
        grid=(n // tm,),
        in_specs=[pl.BlockSpec((tm, 1024), rows), pl.BlockSpec((tm, 512), rows),
                  pl.BlockSpec((tm, 256), rows), pl.BlockSpec((tm, 256), rows),
                  pl.BlockSpec((tm, 1024), rows), pl.BlockSpec((1024, 1024), lambda i: (0, 0))],
        out_specs=pl.BlockSpec((tm, 1024), rows),
        out_shape=jax.ShapeDtypeStruct((n, 1024), f32),
        compiler_params=_cparams(("parallel",)),
        name="merge",
    )(x2, oa, ob, oc, gate, w_out)


def _ret_prompt_kernel(q_ref, k_ref, v_ref, dm_ref, xi_ref, zeta_ref, gc_ref, gnw_ref, bd_ref,
                       o_ref, r_ref):
    c = RET_CHUNK
    n_chunks = q_ref.shape[1] // c
    lo = _lane((c, LANES)) < HD
    same_head = (_row((LANES, LANES)) < HD) == (_lane((LANES, LANES)) < HD)
    bd = bd_ref[...]
    xi = xi_ref[0]
    zeta = zeta_ref[0]
    gnw = gnw_ref[...]

    def step(i, r):
        rows = pl.ds(pl.multiple_of(i * c, c), c)
        q = q_ref[0, rows, :]
        k = k_ref[0, rows, :]
        v = v_ref[0, rows, :]
        zero = jnp.zeros_like(q)
        a0 = _dot_nt(jnp.where(lo, q, zero), k) * dm_ref[0]
        a1 = _dot_nt(jnp.where(lo, zero, q), k) * dm_ref[1]
        o = jnp.where(lo, _dot(a0.astype(bf16), v), _dot(a1.astype(bf16), v))
        o = o + _dot(q, r.astype(bf16)) * xi
        kz_t = (k.astype(f32) * zeta).T.astype(bf16)
        r = r * gc_ref[0] + jnp.where(same_head, _dot(kz_t, v), 0.0)
        mu = _dot_split(o, bd)
        d = o - mu
        var = _dot_split(d * d, bd)
        o_ref[0, rows, :] = d * lax.rsqrt(var + EPS) * gnw
        return r

    r = lax.fori_loop(0, n_chunks, step, jnp.zeros((LANES, LANES), f32))
    r_ref[0, 0] = r[0:HD, 0:HD]
    r_ref[0, 1] = pltpu.roll(r, HD, 1)[HD:2 * HD, 0:HD]


def _ret_prompt(cqkv, lw, b, t):
    blk = lambda u0: pl.BlockSpec((1, t, LANES), lambda i, p: (i, 0, u0 + p))
    pair = lambda shape: pl.BlockSpec(shape, lambda i, p: (p,) + (0,) * (len(shape) - 1))
    return pl.pallas_call(
        _ret_prompt_kernel,
        grid=(b, 2),
        in_specs=[blk(0), blk(2), blk(4),
                  pair((2, RET_CHUNK, RET_CHUNK)), pair((1, RET_CHUNK, LANES)),
                  pair((1, RET_CHUNK, LANES)), pair((1, 1, LANES)),
                  pl.BlockSpec((1, LANES), lambda i, p: (0, p)),
                  pl.BlockSpec((LANES, LANES), lambda i, p: (0, 0))],
        out_specs=[pl.BlockSpec((1, t, LANES), lambda i, p: (i, 0, p)),
                   pl.BlockSpec((1, 2, HD, HD), lambda i, p: (i, p, 0, 0))],
        out_shape=[jax.ShapeDtypeStruct((b, t, H_C * HD), f32),
                   jax.ShapeDtypeStruct((b, H_C, HD, HD), f32)],
        compiler_params=_cparams(("parallel", "parallel")),
        name="ret_prompt",
    )(cqkv, cqkv, cqkv, lw["ret_dmask"], lw["ret_xi"], lw["ret_zeta"], lw["ret_gc"], lw["ret_gn_w"],
      lw["bd"])


def _fox_prep_kernel(misc_ref, ltri_ref, ccol_ref, crow_ref):
    c = LANES
    n_chunks = misc_ref.shape[1] // c
    ltri = ltri_ref[...]

    def step(i, carry):
        rows = pl.ds(pl.multiple_of(i * c, c), c)
        cs = _dot_split_l(ltri, misc_ref[0, rows, :], 3) + carry
        for p in range(2):
            sh = pltpu.roll(cs, LANES - (MISC_G + 2 * p), 1)
            ccol_ref[0, p, rows, :] = sh
            crow_ref[0, p, i] = sh.T[0:8, :]
        return cs[c - 1:c, :]

    lax.fori_loop(0, n_chunks, step, jnp.zeros((1, LANES), f32))


def _fox_prep(misc3, ltri):
    b, t, _ = misc3.shape
    return pl.pallas_call(
        _fox_prep_kernel,
        grid=(b,),
        in_specs=[pl.BlockSpec((1, t, LANES), lambda i: (i, 0, 0)),
                  pl.BlockSpec((LANES, LANES), lambda i: (0, 0))],
        out_specs=[pl.BlockSpec((1, 2, t, LANES), lambda i: (i, 0, 0, 0)),
                   pl.BlockSpec((1, 2, t // LANES, 8, LANES), lambda i: (i, 0, 0, 0, 0))],
        out_shape=[jax.ShapeDtypeStruct((b, 2, t, LANES), f32),
                   jax.ShapeDtypeStruct((b, 2, t // LANES, 8, LANES), f32)],
        compiler_params=_cparams(("parallel",)),
        name="fox_prep",
    )(misc3, ltri)


def _softmax_step(s, m, l, acc, v):
    m_new = jnp.maximum(m, jnp.max(s, axis=1, keepdims=True))
    alpha = jnp.exp(m - m_new)
    p = jnp.exp(s - m_new)
    l = alpha * l + p
    acc = alpha * acc + _dot(p.astype(bf16), v)
    return m_new, l, acc


def _softmax_finish(l, acc):
    return acc / jnp.sum(l, axis=1, keepdims=True)


def _fox_prompt_kernel(q_ref, k_ref, v_ref, ccol_ref, crow_ref, o_ref, m_scr, l_scr, acc_scr):
    tq = q_ref.shape[1]
    qi = pl.program_id(2)
    q2 = jnp.concatenate([q_ref[0, :, 0:LANES], q_ref[0, :, LANES:2 * LANES]], axis=0)
    cc = ccol_ref[0, 0]
    cq = jnp.broadcast_to(jnp.concatenate([cc[:, 0:1], cc[:, 1:2]], axis=0), (2 * tq, tq))
    m_scr[...] = jnp.full(m_scr.shape, NEG, f32)
    l_scr[...] = jnp.zeros(l_scr.shape, f32)
    acc_scr[...] = jnp.zeros(acc_scr.shape, f32)

    def tile(kt, masked):
        cols = pl.ds(pl.multiple_of(kt * tq, tq), tq)
        k = k_ref[0, cols, :].astype(bf16)
        v = v_ref[0, cols, :].astype(bf16)
        ck = crow_ref[0, 0, kt]
        ck2 = jnp.concatenate([jnp.broadcast_to(ck[0:1, :], (tq, tq)),
                               jnp.broadcast_to(ck[1:2, :], (tq, tq))], axis=0)
        s = _dot_nt(q2, k) + (cq - ck2)
        if masked:
            causal = _lane((tq, tq)) <= _row((tq, tq))
            s = jnp.where(jnp.concatenate([causal, causal], axis=0), s, NEG)
        m, l, acc = _softmax_step(s, m_scr[...], l_scr[...], acc_scr[...], v)
        m_scr[...] = m
        l_scr[...] = l
        acc_scr[...] = acc

    def body(kt, carry):
        tile(kt, False)
        return carry

    lax.fori_loop(0, qi, body, 0)
    tile(qi, True)
    o = _softmax_finish(l_scr[...], acc_scr[...])
    o_ref[0] = jnp.where(_lane((tq, LANES)) < HD, o[0:tq], o[tq:2 * tq])


def _fox_prompt(bqal, fkv, ccol, crow, b, t, tq):
    return pl.pallas_call(
        _fox_prompt_kernel,
        grid=(b, 2, t // tq),
        in_specs=[pl.BlockSpec((1, tq, 2 * LANES), lambda i, p, j: (i, j, p)),
                  pl.BlockSpec((1, t, LANES), lambda i, p, j: (i, 0, p)),
                  pl.BlockSpec((1, t, LANES), lambda i, p, j: (i, 0, 2 + p)),
                  pl.BlockSpec((1, 1, tq, LANES), lambda i, p, j: (i, p, j, 0)),
                  pl.BlockSpec((1, 1, t // tq, 8, tq), lambda i, p, j: (i, p, 0, 0, 0))],
        out_specs=pl.BlockSpec((1, tq, LANES), lambda i, p, j: (i, j, p)),
        out_shape=jax.ShapeDtypeStruct((b, t, H_B * HD), f32),
        scratch_shapes=[pltpu.VMEM((2 * tq, LANES), f32), pltpu.VMEM((2 * tq, LANES), f32),
                        pltpu.VMEM((2 * tq, LANES), f32)],
        compiler_params=_cparams(("parallel", "parallel", "parallel")),
        name="fox_prompt",
    )(bqal, fkv, fkv, ccol, crow)


def _top_n_mask(score, n):
    lane = _lane(score.shape).astype(f32)
    sel = jnp.zeros(score.shape, f32)
    for _ in range(n):
        m = jnp.broadcast_to(jnp.max(score, axis=1, keepdims=True), score.shape)
        first = jnp.min(jnp.where(score == m, lane, float(LANES)), axis=1, keepdims=True)
        pick = lane == jnp.broadcast_to(first, score.shape)
        sel = jnp.where(pick, 1.0, sel)
        score = jnp.where(pick, -2.0, score)
    return sel


def _masked_softmax_rows(s, valid):
    s = jnp.where(valid, s, NEG)
    m = jnp.broadcast_to(jnp.max(s, axis=1, keepdims=True), s.shape)
    m = jnp.where(m > 0.5 * NEG, m, 0.0)
    p = jnp.where(valid, jnp.exp(s - m), 0.0)
    den = jnp.broadcast_to(jnp.sum(p, axis=1, keepdims=True), s.shape)
    return p / jnp.where(den > 0, den, 1.0)


def _select_blocks(imp, cur2):
    return _select_pairs(imp + pltpu.roll(imp, LANES - 1, 1), cur2)


def _select_pairs(imp2, cur2):
    lane = _lane(imp2.shape)
    is_blk = (lane & 1) == 0
    valid = is_blk & (lane <= cur2)
    forced = (lane == 0) | (lane == cur2) | (lane == cur2 - 2)
    score = jnp.where(valid, imp2 + jnp.where(forced, FORCE_BONUS, 0.0), -1.0)
    return _top_n_mask(score, TOP_N)


def _softmax_step2(s, m, l, acc, v):
    s_a, s_b = s[:, 0:LANES], s[:, LANES:2 * LANES]
    m_new = jnp.maximum(m, jnp.max(jnp.maximum(s_a, s_b), axis=1, keepdims=True))
    alpha = jnp.exp(m - m_new)
    p_a = jnp.exp(s_a - m_new)
    p_b = jnp.exp(s_b - m_new)
    l = alpha * l + (p_a + p_b)
    pv = _dot(p_a.astype(bf16), v[0:LANES]) + _dot(p_b.astype(bf16), v[LANES:2 * LANES])
    return m_new, l, alpha * acc + pv


def _nsa_prompt_kernel(q_ref, kvb_ref, slc_ref, win_ref, misc_ref, exp_ref, blkend_ref,
                       o_ref, m_scr, l_scr, acc_scr):
    tq = q_ref.shape[1]
    tk = 2 * LANES
    qi = pl.program_id(1)
    t_col = qi * tq + _row((tq, 1))
    t8 = jnp.concatenate([t_col] * H_A, axis=0)
    lane = _lane((tq, LANES))
    lo = lane < HD
    key = _lane((tq, tk))
    misc = misc_ref[0]
    nb = kvb_ref.shape[1]
    zpad = jnp.zeros((LANES - nb, LANES), bf16)
    q8 = jnp.concatenate([q_ref[0, :, g * LANES:(g + 1) * LANES] for g in range(H_A)], axis=0)

    kb = jnp.concatenate([kvb_ref[0, :, 0:LANES].astype(bf16), zpad], axis=0)
    vb = jnp.concatenate([kvb_ref[0, :, LANES:2 * LANES].astype(bf16), zpad], axis=0)
    p = _masked_softmax_rows(_dot_nt(q8, kb), blkend_ref[...] <= t8)
    o_cmp = _dot(p.astype(bf16), vb)
    sel = []
    for kv in range(KV_A):
        imp = p[GQA_A * kv * tq:(GQA_A * kv + 1) * tq]
        for j in range(1, GQA_A):
            imp = imp + p[(GQA_A * kv + j) * tq:(GQA_A * kv + j + 1) * tq]
        sel.append(_select_blocks(imp, (t_col // L_SEL) * 2).astype(bf16))

    def slc_bias(kt):
        causal = (kt * tk + key) <= t_col
        per_kv = [jnp.where((_dot(sel[kv], exp_ref[kt]) > 0.5) & causal, 0.0, NEG) for kv in range(KV_A)]
        return jnp.concatenate([per_kv[g // GQA_A] for g in range(H_A)], axis=0)

    def win_bias(kt):
        d = t_col - (kt * tk + key)
        return jnp.concatenate([jnp.where((d >= 0) & (d < WINDOW), 0.0, NEG)] * H_A, axis=0)

    def flash(kv_ref, kt_lo, kt_hi, bias_fn):
        m_scr[...] = jnp.full(m_scr.shape, NEG, f32)
        l_scr[...] = jnp.zeros(l_scr.shape, f32)
        acc_scr[...] = jnp.zeros(acc_scr.shape, f32)

        def body(kt, carry):
            rows = pl.ds(pl.multiple_of(kt * tk, tk), tk)
            k = kv_ref[0, rows, 0:LANES].astype(bf16)
            v = kv_ref[0, rows, LANES:2 * LANES].astype(bf16)
            s = _dot_nt(q8, k) + bias_fn(kt)
            m, l, acc = _softmax_step2(s, m_scr[...], l_scr[...], acc_scr[...], v)
            m_scr[...] = m
            l_scr[...] = l
            acc_scr[...] = acc
            return carry

        lax.fori_loop(kt_lo, kt_hi, body, 0)
        return _softmax_finish(l_scr[...], acc_scr[...])

    kt_end = (qi * tq) // tk + 1
    o_slc = flash(slc_ref, 0, kt_end, slc_bias)
    o_win = flash(win_ref, jnp.maximum(qi * tq - WINDOW, 0) // tk, kt_end, win_bias)

    heads = []
    for g in range(H_A):
        rows = slice(g * tq, (g + 1) * tq)
        o = (misc[:, 3 * g:3 * g + 1] * o_cmp[rows] + misc[:, 3 * g + 1:3 * g + 2] * o_slc[rows]
             + misc[:, 3 * g + 2:3 * g + 3] * o_win[rows])
        heads.append(o if (g % 2) == g // GQA_A else pltpu.roll(o, HD, 1))
    for u in range(H_A // 2):
        o_ref[0, :, u * LANES:(u + 1) * LANES] = jnp.where(lo, heads[2 * u], heads[2 * u + 1])


def _nsa_prompt(qal, kvb, slc, win, misc3, consts, b, t, tq):
    nb = t // L_CMP
    tk = 2 * LANES
    return pl.pallas_call(
        _nsa_prompt_kernel,
        grid=(b, t // tq),
        in_specs=[pl.BlockSpec((1, tq, H_A * LANES), lambda i, j: (i, j, 0)),
                  pl.BlockSpec((1, nb, 2 * LANES), lambda i, j: (i, 0, 0)),
                  pl.BlockSpec((1, t, 2 * LANES), lambda i, j: (i, 0, 0)),
                  pl.BlockSpec((1, t, 2 * LANES), lambda i, j: (i, 0, 0)),
                  pl.BlockSpec((1, tq, LANES), lambda i, j: (i, j, 0)),
                  pl.BlockSpec((t // tk, LANES, tk), lambda i, j: (0, 0, 0)),
                  pl.BlockSpec((1, LANES), lambda i, j: (0, 0))],
        out_specs=pl.BlockSpec((1, tq, H_A * HD), lambda i, j: (i, j, 0)),
        out_shape=jax.ShapeDtypeStruct((b, t, H_A * HD), f32),
        scratch_shapes=[pltpu.VMEM((H_A * tq, LANES), f32), pltpu.VMEM((H_A * tq, LANES), f32),
                        pltpu.VMEM((H_A * tq, LANES), f32)],
        compiler_params=_cparams(("parallel", "parallel")),
        name="nsa_prompt",
    )(qal, kvb, slc, win, misc3, consts["expand"], consts["blkend"])


def _rope_tables(pos):
    half = HD // 2
    inv = ROPE_THETA ** (-jnp.arange(half, dtype=f32) / half)
    ang = pos.astype(f32)[:, None] * inv[None, :]
    cos, sin = jnp.cos(ang), jnp.sin(ang)
    return (jnp.concatenate([cos, cos, cos, cos], axis=-1),
            jnp.concatenate([-sin, sin, -sin, sin], axis=-1))


def _constants(t, tq):
    lane = np.arange(LANES)
    bd = ((lane[:, None] // HD) == (lane[None, :] // HD)).astype(np.float32) / HD
    ltri = (lane[None, :] <= lane[:, None]).astype(np.float32)
    key = np.arange(t).reshape(t // tq, 1, tq)
    expand = ((lane[None, :, None] % 2 == 0) & (key // L_SEL == lane[None, :, None] // 2)).astype(np.float32)
    blkend = np.where(lane < t // L_CMP, (lane + 1) * L_CMP - 1, 2 ** 30).astype(np.int32)[None, :]
    log_g = jnp.log1p(-jnp.exp2(-5.0 - jnp.arange(H_C, dtype=f32)))
    i = jnp.arange(RET_CHUNK, dtype=f32)
    diff = i[:, None] - i[None, :]
    dmask = jnp.where(diff >= 0, jnp.exp(jnp.maximum(diff, 0.0) * log_g[:, None, None]), 0.0)
    xi = jnp.exp((i + 1.0)[None, :] * log_g[:, None])
    zeta = jnp.exp((RET_CHUNK - 1.0 - i)[None, :] * log_g[:, None])
    g_c = jnp.exp(RET_CHUNK * log_g)
    by_lane = lambda a: jnp.repeat(a.reshape(2, 2, -1).transpose(0, 2, 1), HD, axis=-1)
    return {"bd": jnp.asarray(bd, bf16), "pool8": _pool_matrix(8, 2 * PAGE),
            "ltri": jnp.asarray(ltri, bf16),
            "expand": jnp.asarray(expand, bf16), "blkend": jnp.asarray(blkend),
            "ret_dmask": dmask, "ret_xi": by_lane(xi), "ret_zeta": by_lane(zeta),
            "ret_gc": by_lane(g_c[:, None]), "log_g": log_g}


def _pack_w_in(w):
    off = np.cumsum([0, 512, 128, 128, 128, 128, 128, 128, 24, 512, 256, 256, 256, 4, 256, 256, 256, 256, 256])
    seg = lambda i: w[:, off[i]:off[i + 1]]
    (a_q, a_kc, a_vc, a_ks, a_vs, a_kw, a_vw, a_g, a_gate,
     b_q, b_k, b_v, b_f, b_gate, c_q, c_k, c_v, c_gate) = [seg(i) for i in range(18)]
    pad = jnp.zeros((w.shape[0], LANES - MISC_G - H_B), w.dtype)
    return jnp.concatenate([a_q, a_kc, a_vc, a_ks, a_vs, a_kw, a_vw, a_g, b_f, pad,
                            a_gate, b_gate, c_gate, b_q, b_k, b_v, c_q, c_k, c_v], axis=1).astype(bf16)


def _layer_params(l, consts, norm_w, w_in, fox_bf, nsa_q_norm, nsa_k_norm, fox_q_norm, fox_k_norm,
                  ret_gn_w, w_out):
    two = lambda g: jnp.concatenate([g, g])[None, :]
    gains = jnp.concatenate([two(nsa_q_norm[l]), two(nsa_k_norm[l, 0]), two(nsa_k_norm[l, 1]),
                             two(nsa_k_norm[l, 2]), two(fox_q_norm[l]), two(fox_k_norm[l]),
                             jnp.zeros((2, LANES), f32)], axis=0)
    bf = jnp.zeros((1, LANES), f32).at[0, MISC_G:MISC_G + H_B].set(fox_bf[l])
    lw = dict(consts)
    lw.update(norm_w=norm_w[l][None, :], w_pack=_pack_w_in(w_in[l]), gains=gains, bf=bf,
              ret_gn_w=ret_gn_w[l][None, :], w_out=w_out[l].astype(bf16))
    return lw


def _prompt_layer(x, lw, consts, cos_t, sin_t):
    b, t, d = x.shape
    tq = 128
    x2 = x.reshape(b * t, d)
    qal, cmp_, slc, win, misc, gate, bqal, fkv, cqkv, kvb = _proj(x2, lw, cos_t, sin_t, 256)
    r3 = lambda a: a.reshape(b, t, a.shape[-1])
    misc3 = r3(misc)
    o_a = _nsa_prompt(r3(qal), kvb.reshape(b, t // L_CMP, 2 * LANES), r3(slc), r3(win), misc3,
                      consts, b, t, tq)
    ccol, crow = _fox_prep(misc3, consts["ltri"])
    o_b = _fox_prompt(r3(bqal), r3(fkv), ccol, crow, b, t, tq)
    o_c, r_new = _ret_prompt(r3(cqkv), lw, b, t)
    y = _merge(x2, o_a.reshape(b * t, -1), o_b.reshape(b * t, -1), o_c.reshape(b * t, -1), gate,
               lw["w_out"], 256)
    wk = min(WINDOW, t)
    states = (cmp_.reshape(b, t, 2, KV_A, HD), slc.reshape(b, t, 2, KV_A, HD),
              fkv.reshape(b, t, 2, H_B, HD), misc3[:, :, MISC_G:MISC_G + H_B],
              r3(win)[:, t - wk:].reshape(b, wk, 2, KV_A, HD), r_new)
    return y.reshape(b, t, d), states


def _head_col(vec_row, shape, base, stride):
    pick = _lane(shape) == base + stride * _row(shape)
    return jnp.sum(jnp.where(pick, jnp.broadcast_to(vec_row, shape), 0.0), axis=1, keepdims=True)


def _seg_mean(x, seg):
    lane = _lane(x.shape)
    sh = seg // 2
    while sh >= 1:
        x = x + jnp.where((lane & sh) != 0, pltpu.roll(x, sh, 1), pltpu.roll(x, LANES - sh, 1))
        sh //= 2
    return x * (1.0 / seg)


def _dec_nsa_kernel(pt_ref, q_ref, slc_new_ref, win_new_ref, misc_ref, win_ref, *rest, group, n_pages):
    cmp_pages = rest[0:group]
    slc_pages = rest[group:2 * group]
    o_ref, sc_scr, cm_scr, cl_scr, cacc_scr, m_scr, l_scr, o_scr = rest[2 * group:]
    pg = pl.program_id(1)
    shape = (H_A, LANES)
    lane = _lane(shape)
    row = _row(shape)
    lo = lane < HD
    q8 = q_ref[0]
    past = n_pages * PAGE
    cur2 = 2 * (past // L_SEL)

    @pl.when(pg == 0)
    def _():
        m_scr[...] = jnp.full(shape, NEG, f32)
        l_scr[...] = jnp.zeros(shape, f32)
        cm_scr[...] = jnp.full(cm_scr.shape, NEG, f32)
        cl_scr[...] = jnp.zeros(cl_scr.shape, f32)
        cacc_scr[...] = jnp.zeros(shape, f32)

    for g in range(group):
        page = pg * group + g
        kt = cmp_pages[g][0, 0, 0:LANES, :].astype(bf16)
        vt = cmp_pages[g][0, 0, LANES:2 * LANES, :].astype(bf16)
        sb = _seg_mean(_dot(q8, kt), L_CMP)
        sc_scr[page] = sb
        cm = cm_scr[...]
        cm_new = jnp.maximum(cm, jnp.max(sb, axis=1, keepdims=True))
        alpha = jnp.exp(cm - cm_new)
        p = jnp.exp(sb - cm_new) * (1.0 / L_CMP)
        cm_scr[...] = cm_new
        cl_scr[...] = alpha * cl_scr[...] + jnp.sum(p, axis=1, keepdims=True)
        cacc_scr[...] = alpha * cacc_scr[...] + _dot_nt(p.astype(bf16), vt)

    for g in range(group):
        page = pg * group + g
        k = slc_pages[g][0, 0, 0:LANES, :].astype(bf16)
        v = slc_pages[g][0, 0, LANES:2 * LANES, :].astype(bf16)
        s = _dot(q8, k)
        m_lo = jnp.max(jnp.where(lo, s, NEG), axis=1, keepdims=True)
        m_hi = jnp.max(jnp.where(lo, NEG, s), axis=1, keepdims=True)
        p = jnp.exp(s - jnp.where(lo, m_lo, m_hi))
        p_lo = jnp.where(lo, p, 0.0)
        p_hi = jnp.where(lo, 0.0, p)
        o2 = _dot_nt(jnp.concatenate([p_lo, p_hi], axis=0).astype(bf16), v)
        at_lo = lane == 4 * page
        at_hi = lane == 4 * page + 2
        m_scr[...] = jnp.where(at_lo, m_lo, jnp.where(at_hi, m_hi, m_scr[...]))
        l_scr[...] = jnp.where(at_lo, jnp.sum(p_lo, axis=1, keepdims=True),
                               jnp.where(at_hi, jnp.sum(p_hi, axis=1, keepdims=True), l_scr[...]))
        o_scr[2 * page] = o2[0:H_A]
        o_scr[2 * page + 1] = o2[H_A:2 * H_A]

    @pl.when(pg == pl.num_programs(1) - 1)
    def _():
        qf = q8.astype(f32)
        cm = cm_scr[...]
        cl = cl_scr[...]
        o_cmp = cacc_scr[...] / cl
        imp2 = jnp.zeros(shape, f32)
        for page in range(n_pages):
            e = jnp.exp(sc_scr[page] - cm) / cl
            e = jnp.where(row < GQA_A,
                          jnp.sum(e[0:GQA_A], axis=0, keepdims=True),
                          jnp.sum(e[GQA_A:2 * GQA_A], axis=0, keepdims=True))
            pair = e + pltpu.roll(e, LANES - L_CMP, 1)
            first = pltpu.roll(pair, 4 * page, 1) if page else pair
            imp2 = jnp.where(lane == 4 * page, first, imp2)
            imp2 = jnp.where(lane == 4 * page + 2, pltpu.roll(pair, (4 * page + 2 + HD) % LANES, 1), imp2)
        sel = _select_pairs(imp2, cur2) > 0.5
        bf_round = lambda a: a.astype(bf16).astype(f32)
        k_new = bf_round(slc_new_ref[0, :, 0:LANES])
        v_new = bf_round(slc_new_ref[0, :, LANES:2 * LANES])
        s_new = jnp.sum(qf * k_new, axis=1, keepdims=True)
        m_all = jnp.where(lane == cur2, s_new, m_scr[...])
        l_all = jnp.where(lane == cur2, 1.0, l_scr[...])
        m_top = jnp.max(jnp.where(sel, m_all, NEG), axis=1, keepdims=True)
        w = jnp.where(sel, jnp.exp(m_all - m_top), 0.0)
        den = jnp.sum(w * l_all, axis=1, keepdims=True)
        acc = w[:, cur2:cur2 + 1] * v_new
        for j in range(past // L_SEL):
            acc = acc + w[:, 2 * j:2 * j + 1] * o_scr[j]
        o_slc = acc / den
        wk = win_ref.shape[3]
        kw = win_ref[0, 0, 0:LANES, :].astype(bf16)
        vw = win_ref[0, 0, LANES:2 * LANES, :].astype(bf16)
        kw_new = bf_round(win_new_ref[0, :, 0:LANES])
        vw_new = bf_round(win_new_ref[0, :, LANES:2 * LANES])
        s = jnp.where(_lane((H_A, wk)) > wk - WINDOW, _dot(q8, kw), NEG)
        s_new = jnp.sum(qf * kw_new, axis=1, keepdims=True)
        m = jnp.maximum(jnp.max(s, axis=1, keepdims=True), s_new)
        pw = jnp.exp(s - m)
        pw_new = jnp.exp(s_new - m)
        o_win = ((_dot_nt(pw.astype(bf16), vw) + pw_new * vw_new)
                 / (jnp.sum(pw, axis=1, keepdims=True) + pw_new))
        misc = misc_ref[0]
        o8 = (_head_col(misc, shape, 0, 3) * o_cmp + _head_col(misc, shape, 1, 3) * o_slc
              + _head_col(misc, shape, 2, 3) * o_win)
        o8r = pltpu.roll(o8, HD, 1)
        lo1 = _lane((1, LANES)) < HD
        for u in range(H_A // 2):
            kv_lo = (2 * u) // GQA_A == 0
            even = (o8 if kv_lo else o8r)[2 * u:2 * u + 1, :]
            odd = (o8r if kv_lo else o8)[2 * u + 1:2 * u + 2, :]
            o_ref[0, :, u * LANES:(u + 1) * LANES] = jnp.where(lo1, even, odd)


def _page_specs(block, layer, group, count):
    def spec(g):
        return pl.BlockSpec(block, lambda b, pg, pt: (layer, pt[b * count + pg * group + g]) + (0,) * (len(block) - 2))
    return [spec(g) for g in range(group)]


def _dec_nsa(pt_flat, q8, slc_new, win_new, misc, win_state, cache_cmp, cache_slc, layer, group):
    n_seq = q8.shape[0]
    n_pages = pt_flat.shape[0] // n_seq
    wk = win_state.shape[3]
    seq3 = lambda w: pl.BlockSpec((1, 1, w), lambda b, pg, pt: (b, 0, 0))
    page_block = (1, 1, 2 * LANES, PAGE)
    grid_spec = pltpu.PrefetchScalarGridSpec(
        num_scalar_prefetch=1,
        grid=(n_seq, n_pages // group),
        in_specs=[pl.BlockSpec((1, H_A, LANES), lambda b, pg, pt: (b, 0, 0)),
                  seq3(2 * LANES), seq3(2 * LANES), seq3(LANES),
                  pl.BlockSpec((1, 1, 2 * LANES, wk), lambda b, pg, pt: (layer, b, 0, 0))]
        + _page_specs(page_block, layer, group, n_pages) + _page_specs(page_block, layer, group, n_pages),
        out_specs=pl.BlockSpec((1, 1, H_A * HD), lambda b, pg, pt: (b, 0, 0)),
        scratch_shapes=[pltpu.VMEM((n_pages, H_A, LANES), f32),
                        pltpu.VMEM((H_A, 1), f32), pltpu.VMEM((H_A, 1), f32), pltpu.VMEM((H_A, LANES), f32),
                        pltpu.VMEM((H_A, LANES), f32), pltpu.VMEM((H_A, LANES), f32),
                        pltpu.VMEM((n_pages * PAGE // L_SEL, H_A, LANES), f32)])
    return pl.pallas_call(
        functools.partial(_dec_nsa_kernel, group=group, n_pages=n_pages),
        grid_spec=grid_spec,
        out_shape=jax.ShapeDtypeStruct((n_seq, 1, H_A * HD), f32),
        compiler_params=_cparams(("parallel", "arbitrary")),
        name="dec_nsa",
    )(pt_flat, q8, slc_new, win_new, misc, win_state, *([cache_cmp] * group), *([cache_slc] * group))


def _lane_cumsum(x):
    lane = _lane(x.shape)
    sh = 1
    while sh < LANES:
        x = x + jnp.where(lane >= sh, pltpu.roll(x, sh, 1), 0.0)
        sh *= 2
    return x


def _dec_fox_kernel(pt_ref, q_ref, kv_new_ref, misc_ref, *rest, group):
    kv_pages = rest[0:group]
    lf_pages = rest[group:2 * group]
    o_ref, m_scr, l_scr, c_scr, acc0_scr, acc1_scr = rest[2 * group:]
    pg = pl.program_id(1)
    shape = (8, LANES)
    first_pair = _row(shape) < 2
    q8 = q_ref[0]

    @pl.when(pg == 0)
    def _():
        m_scr[...] = jnp.full(m_scr.shape, NEG, f32)
        l_scr[...] = jnp.zeros(l_scr.shape, f32)
        c_scr[...] = jnp.zeros(c_scr.shape, f32)
        acc0_scr[...] = jnp.zeros(shape, f32)
        acc1_scr[...] = jnp.zeros(shape, f32)

    for g in range(group):
        kv = kv_pages[g]
        c = c_scr[...] + _lane_cumsum(lf_pages[g][0, 0])
        c_scr[...] = c[:, LANES - 1:LANES]
        s = jnp.where(first_pair, _dot(q8, kv[0, 0, 0:LANES, :].astype(bf16)),
                      _dot(q8, kv[0, 0, LANES:2 * LANES, :].astype(bf16))) - c
        m = m_scr[...]
        m_new = jnp.maximum(m, jnp.max(s, axis=1, keepdims=True))
        alpha = jnp.exp(m - m_new)
        p = jnp.exp(s - m_new)
        pb = p.astype(bf16)
        m_scr[...] = m_new
        l_scr[...] = alpha * l_scr[...] + jnp.sum(p, axis=1, keepdims=True)
        acc0_scr[...] = alpha * acc0_scr[...] + _dot_nt(pb, kv[0, 0, 2 * LANES:3 * LANES, :].astype(bf16))
        acc1_scr[...] = alpha * acc1_scr[...] + _dot_nt(pb, kv[0, 0, 3 * LANES:4 * LANES, :].astype(bf16))

    @pl.when(pg == pl.num_programs(1) - 1)
    def _():
        bf_round = lambda a: jnp.broadcast_to(a.astype(bf16).astype(f32), shape)
        new = kv_new_ref[0]
        k_new = jnp.where(first_pair, bf_round(new[:, 0:LANES]), bf_round(new[:, LANES:2 * LANES]))
        v_new = jnp.where(first_pair, bf_round(new[:, 2 * LANES:3 * LANES]), bf_round(new[:, 3 * LANES:4 * LANES]))
        s_new = (jnp.sum(q8.astype(f32) * k_new, axis=1, keepdims=True)
                 - (c_scr[...] + _head_col(misc_ref[0], shape, MISC_G, 1)))
        m = m_scr[...]
        m_new = jnp.maximum(m, s_new)
        alpha = jnp.exp(m - m_new)
        p_new = jnp.exp(s_new - m_new)
        acc = jnp.where(first_pair, acc0_scr[...], acc1_scr[...])
        o8 = (alpha * acc + p_new * v_new) / (alpha * l_scr[...] + p_new)
        lo1 = _lane((1, LANES)) < HD
        for u in range(H_B // 2):
            o_ref[0, :, u * LANES:(u + 1) * LANES] = jnp.where(lo1, o8[2 * u:2 * u + 1, :],
                                                               o8[2 * u + 1:2 * u + 2, :])


def _dec_fox(pt_flat, qb4, fkv_new, misc, cache_fkv, cache_lft, layer, group):
    n_seq = qb4.shape[0]
    n_pages = pt_flat.shape[0] // n_seq
    seq3 = lambda w: pl.BlockSpec((1, 1, w), lambda b, pg, pt: (b, 0, 0))
    grid_spec = pltpu.PrefetchScalarGridSpec(
        num_scalar_prefetch=1,
        grid=(n_seq, n_pages // group),
        in_specs=[pl.BlockSpec((1, 8, LANES), lambda b, pg, pt: (b, 0, 0)), seq3(4 * LANES), seq3(LANES)]
        + _page_specs((1, 1, 4 * LANES, PAGE), layer, group, n_pages)
        + _page_specs((1, 1, 8, PAGE), layer, group, n_pages),
        out_specs=pl.BlockSpec((1, 1, H_B * HD), lambda b, pg, pt: (b, 0, 0)),
        scratch_shapes=[pltpu.VMEM((8, 1), f32), pltpu.VMEM((8, 1), f32), pltpu.VMEM((8, 1), f32),
                        pltpu.VMEM((8, LANES), f32), pltpu.VMEM((8, LANES), f32)])
    return pl.pallas_call(
        functools.partial(_dec_fox_kernel, group=group),
        grid_spec=grid_spec,
        out_shape=jax.ShapeDtypeStruct((n_seq, 1, H_B * HD), f32),
        compiler_params=_cparams(("parallel", "arbitrary")),
        name="dec_fox",
    )(pt_flat, qb4, fkv_new, misc, *([cache_fkv] * group), *([cache_lft] * group))


def _dec_ret_kernel(q_ref, k_ref, v_ref, r_ref, gam_ref, gnw_ref, o_ref, rn_ref):
    q = q_ref[0].astype(f32)
    k = k_ref[0].astype(f32)
    v = v_ref[0].astype(f32)
    qk = jnp.sum(q * k, axis=1, keepdims=True)
    eye = _row((HD, HD)) == _lane((HD, HD))
    col = lambda a: jnp.sum(jnp.where(eye, jnp.broadcast_to(a, (HD, HD)), 0.0), axis=1, keepdims=True)
    for h in range(H_C):
        hs = slice(h, h + 1)
        r = r_ref[0, 0, h]
        gam = gam_ref[hs, :]
        o = qk[hs, :] * v[hs, :] + jnp.sum(col(q[hs, :]) * r, axis=0, keepdims=True) * gam
        rn_ref[0, h] = r * gam + col(k[hs, :]) * v[hs, :]
        d = o - jnp.mean(o, axis=1, keepdims=True)
        var = jnp.mean(d * d, axis=1, keepdims=True)
        o_ref[0, hs, :] = d * lax.rsqrt(var + EPS) * gnw_ref[hs, :]


def _dec_ret(cq, ck, cv, state_ret, gam, gnw, layer):
    n_seq = cq.shape[0]
    head = pl.BlockSpec((1, H_C, HD), lambda b: (b, 0, 0))
    table = pl.BlockSpec((H_C, HD), lambda b: (0, 0))
    return pl.pallas_call(
        _dec_ret_kernel,
        grid=(n_seq,),
        in_specs=[head, head, head,
                  pl.BlockSpec((1, 1, H_C, HD, HD), lambda b: (layer, b, 0, 0, 0)), table, table],
        out_specs=[head, pl.BlockSpec((1, H_C, HD, HD), lambda b: (b, 0, 0, 0))],
        out_shape=[jax.ShapeDtypeStruct((n_seq, H_C, HD), f32),
                   jax.ShapeDtypeStruct((n_seq, H_C, HD, HD), f32)],
        compiler_params=_cparams(("parallel",)),
        name="dec_ret",
    )(cq, ck, cv, state_ret, gam, gnw)


def _sample_layer(x, layer, lw, consts, caches, pt_flat, cos_t, sin_t, group):
    n, _, d = x.shape
    x2 = x.reshape(n, d)
    qal, cmp_, slc, win, misc, gate, bqal, fkv, cqkv, _ = _proj(x2, lw, cos_t, sin_t, n)
    row3 = lambda a: a[:, None, :]
    o_a8 = _dec_nsa(pt_flat, qal.reshape(n, H_A, LANES), row3(slc), row3(win), row3(misc),
                    caches["win"], caches["cmp"], caches["slc"], layer, group)
    qb8 = jnp.pad(bqal.reshape(n, H_B, LANES), ((0, 0), (0, 8 - H_B), (0, 0)))
    o_b8 = _dec_fox(pt_flat, qb8, row3(fkv), row3(misc),
                    caches["fkv"], caches["lft"], layer, group)
    heads = lambda u: cqkv[:, 2 * u * LANES:(2 * u + 2) * LANES].reshape(n, H_C, HD).astype(f32)
    gam = jnp.broadcast_to(jnp.exp(consts["log_g"])[:, None], (H_C, HD))
    o_c, r_new = _dec_ret(heads(0), heads(1), heads(2), caches["ret"], gam,
                          lw["ret_gn_w"].reshape(H_C, HD), layer)
    y = _merge(x2, o_a8.reshape(n, H_A * HD), o_b8.reshape(n, H_B * HD), o_c.reshape(n, H_C * HD), gate,
               lw["w_out"], n)
    new_win = jnp.concatenate([caches["win"][layer][:, :, 1:], win[:, :, None]], axis=2)
    wk = new_win.shape[2]
    new_win = jnp.transpose(new_win.reshape(n, 2, KV_A, HD, wk), (0, 4, 1, 2, 3))
    states = (cmp_.reshape(n, 1, 2, KV_A, HD), slc.reshape(n, 1, 2, KV_A, HD),
              fkv.reshape(n, 1, 2, H_B, HD), misc[:, None, MISC_G:MISC_G + H_B], new_win, r_new)
    return y.reshape(n, 1, d), states


def kernel(x_prompt, x_sample, cache_nsa_cmp, cache_nsa_slc, cache_fox_kv, cache_fox_logf,
           state_nsa_win, state_ret, page_table, norm_w, w_in, fox_bf, nsa_q_norm, nsa_k_norm,
           fox_q_norm, fox_k_norm, ret_gn_w, w_out):
    depth = norm_w.shape[0]
    b, t, _ = x_prompt.shape
    n_seq, n_pages = page_table.shape
    assert x_sample.shape[1] == 1 and t % 256 == 0 and n_pages % 2 == 0
    past = n_pages * PAGE
    consts = _constants(t, 2 * LANES)
    cos_p, sin_p = _rope_tables(jnp.arange(t, dtype=jnp.int32))
    cos_s, sin_s = _rope_tables(jnp.full((n_seq,), past, dtype=jnp.int32))
    n_phys = cache_nsa_cmp.shape[1]
    feat_major = lambda a, lead: jnp.transpose(a, (0, 1, 3, 4, 5, 2)).reshape(depth, lead, -1, a.shape[2])
    caches = {"cmp": feat_major(cache_nsa_cmp, n_phys), "slc": feat_major(cache_nsa_slc, n_phys),
              "fkv": feat_major(cache_fox_kv, n_phys),
              "lft": jnp.pad(jnp.swapaxes(cache_fox_logf, 2, 3), ((0, 0), (0, 0), (0, 8 - H_B), (0, 0))),
              "win": feat_major(state_nsa_win, n_seq),
              "ret": state_ret}
    pt_flat = page_table.reshape(-1)
    yp, ys = x_prompt, x_sample
    p_states, s_states = [], []
    for l in range(depth):
        lw = _layer_params(l, consts, norm_w, w_in, fox_bf, nsa_q_norm, nsa_k_norm, fox_q_norm,
                           fox_k_norm, ret_gn_w, w_out)
        yp, ps = _prompt_layer(yp, lw, consts, cos_p, sin_p)
        ys, ss = _sample_layer(ys, l, lw, consts, caches, pt_flat, cos_s, sin_s, 8)
        p_states.append(ps)
        s_states.append(ss)
    stk = lambda states, i: jnp.stack([st[i] for st in states], axis=0)
    out = [yp, ys]
    for i in range(6):
        out += [stk(p_states, i), stk(s_states, i)]
    return tuple(out)
```

```python
import functools

import numpy as np
import jax
import jax.numpy as jnp
from jax import lax
from jax.experimental import pallas as pl
from jax.experimental.pallas import tpu as pltpu

f32 = jnp.float32
bf16 = jnp.bfloat16

HD = 64
H_A, KV_A, GQA_A = 8, 2, 4
H_B, H_C = 4, 4
L_CMP, L_SEL, TOP_N, WINDOW = 32, 64, 8, 512
PAGE = 128
RET_CHUNK = 128
ROPE_THETA = 10000.0
EPS = 1e-6
SCALE = HD ** -0.5
FORCE_BONUS = 1e3
NEG = -1e30

LANES = 128
VMEM_LIMIT = 48 * 1024 * 1024

C_AQ, C_CMP, C_SLC, C_WIN, C_MISC, C_GATE, C_BQ, C_BKV, C_CQKV, D_PACK = (
    0, 512, 768, 1024, 1280, 1408, 2432, 2688, 3200, 3968)
MISC_G = 3 * H_A


def _cparams(sem):
    return pltpu.CompilerParams(dimension_semantics=sem, vmem_limit_bytes=VMEM_LIMIT)


def _dot(a, b):
    return jnp.dot(a, b, preferred_element_type=f32)


def _dot_nt(a, b):
    return lax.dot_general(a, b, (((1,), (1,)), ((), ())), preferred_element_type=f32)


def _split(x, n):
    parts = []
    r = x
    for i in range(n):
        h = r.astype(bf16)
        parts.append(h)
        if i + 1 < n:
            r = r - h.astype(f32)
    return parts


def _dot_split(x, w, n=2):
    acc = None
    for h in _split(x, n):
        d = _dot(h, w)
        acc = d if acc is None else acc + d
    return acc


def _dot_split_l(w, x, n=2):
    acc = None
    for h in _split(x, n):
        d = _dot(w, h)
        acc = d if acc is None else acc + d
    return acc


def _lane(shape):
    return lax.broadcasted_iota(jnp.int32, shape, len(shape) - 1)


def _row(shape):
    return lax.broadcasted_iota(jnp.int32, shape, len(shape) - 2)


def _swap_half_heads(y):
    lane = _lane(y.shape)
    return jnp.where((lane & 32) == 0, pltpu.roll(y, 96, 1), pltpu.roll(y, 32, 1))


def _head_rms(z, bd, g):
    ms = _dot_split(z * z, bd)
    return z * lax.rsqrt(ms + EPS) * g


def _rope(y, cos, sin):
    return y * cos + _swap_half_heads(y) * sin


def _sigmoid(z):
    return 1.0 / (1.0 + jnp.exp(-z))


def _proj_kernel(x_ref, nw_ref, w_ref, bd_ref, pool_ref, cos_ref, sin_ref, gn_ref, bf_ref,
                 qal_ref, cmp_ref, slc_ref, win_ref, misc_ref, gate_ref, bqal_ref, fkv_ref,
                 cqkv_ref, kvb_ref):
    x = x_ref[...]
    ms = jnp.mean(x * x, axis=-1, keepdims=True)
    xn = (x * lax.rsqrt(ms + EPS) * nw_ref[...]).astype(bf16)
    bd = bd_ref[...]
    cos = cos_ref[...]
    sin = sin_ref[...]
    lane = _lane((x.shape[0], LANES))
    lo = lane < HD

    def z_of(c0, width):
        return _dot(xn, w_ref[:, c0:c0 + width])

    def unit(z, u):
        return z[:, u * LANES:(u + 1) * LANES]

    z = z_of(C_AQ, H_A * HD)
    g_aq = gn_ref[0:1, :]
    for u in range(H_A // 2):
        y = _rope(_head_rms(unit(z, u), bd, g_aq), cos, sin) * SCALE
        yr = pltpu.roll(y, HD, 1)
        kv_lo = (2 * u) // GQA_A == 0
        keep = lo if kv_lo else jnp.logical_not(lo)
        even, odd = (y, yr) if kv_lo else (yr, y)
        qal_ref[:, (2 * u) * LANES:(2 * u + 1) * LANES] = jnp.where(keep, even, 0.0).astype(bf16)
        qal_ref[:, (2 * u + 1) * LANES:(2 * u + 2) * LANES] = jnp.where(keep, odd, 0.0).astype(bf16)

    for i, (c0, o_ref) in enumerate(((C_CMP, cmp_ref), (C_SLC, slc_ref), (C_WIN, win_ref))):
        z = z_of(c0, 2 * LANES)
        k = _rope(_head_rms(unit(z, 0), bd, gn_ref[1 + i:2 + i, :]), cos, sin)
        v = unit(z, 1)
        o_ref[:, 0:LANES] = k
        o_ref[:, LANES:2 * LANES] = v
        if i == 0:
            pool = pool_ref[...]
            kvb_ref[:, 0:LANES] = _dot_split_l(pool, k)
            kvb_ref[:, LANES:2 * LANES] = _dot_split_l(pool, v)

    z = z_of(C_MISC, LANES)
    zf = z + bf_ref[...]
    logf = jnp.minimum(zf, 0.0) - jnp.log1p(jnp.exp(-jnp.abs(zf)))
    misc_ref[...] = jnp.where(lane < MISC_G, _sigmoid(z), jnp.where(lane < MISC_G + H_B, logf, 0.0))

    for c in range(0, 1024, 512):
        z = z_of(C_GATE + c, 512)
        gate_ref[:, c:c + 512] = z * _sigmoid(z)

    z = z_of(C_BQ, H_B * HD)
    for u in range(H_B // 2):
        y = _head_rms(unit(z, u), bd, gn_ref[4:5, :]) * SCALE
        bqal_ref[:, (2 * u) * LANES:(2 * u + 1) * LANES] = jnp.where(lo, y, 0.0).astype(bf16)
        bqal_ref[:, (2 * u + 1) * LANES:(2 * u + 2) * LANES] = jnp.where(lo, 0.0, y).astype(bf16)
    z = z_of(C_BKV, 2 * H_B * HD)
    for u in range(2):
        fkv_ref[:, u * LANES:(u + 1) * LANES] = _head_rms(unit(z, u), bd, gn_ref[5:6, :])
    fkv_ref[:, 2 * LANES:4 * LANES] = z[:, 2 * LANES:4 * LANES]

    z = z_of(C_CQKV, 3 * H_C * HD)
    for u in range(2):
        cqkv_ref[:, u * LANES:(u + 1) * LANES] = _rope(unit(z, u), cos, sin).astype(bf16)
        cqkv_ref[:, (2 + u) * LANES:(3 + u) * LANES] = (_rope(unit(z, 2 + u), cos, sin) * SCALE).astype(bf16)
    cqkv_ref[:, 4 * LANES:6 * LANES] = z[:, 4 * LANES:6 * LANES].astype(bf16)


def _pool_matrix(n_out, n_rows):
    r = np.arange(n_rows)[None, :] // L_CMP == np.arange(n_out)[:, None]
    return jnp.asarray(r.astype(np.float32) / L_CMP, bf16)


def _proj(x2, lw, cos_t, sin_t, tm):
    n = x2.shape[0]
    nt = cos_t.shape[0] // tm
    const = lambda i: (0, 0)
    rows = lambda i: (i, 0)
    outs = [(H_A * LANES, bf16), (2 * LANES, f32), (2 * LANES, f32), (2 * LANES, f32), (LANES, f32),
            (1024, f32), (H_B * LANES, bf16), (4 * LANES, f32), (6 * LANES, bf16)]
    out_shape = [jax.ShapeDtypeStruct((n, w), dt) for w, dt in outs]
    out_specs = [pl.BlockSpec((tm, w), rows) for w, _ in outs]
    out_shape.append(jax.ShapeDtypeStruct((n // L_CMP, 2 * LANES), f32))
    out_specs.append(pl.BlockSpec((tm // L_CMP, 2 * LANES), rows))
    return pl.pallas_call(
        _proj_kernel,
        grid=(n // tm,),
        in_specs=[pl.BlockSpec((tm, 1024), rows),
                  pl.BlockSpec((1, 1024), const),
                  pl.BlockSpec((1024, D_PACK), const),
                  pl.BlockSpec((LANES, LANES), const),
                  pl.BlockSpec((tm // L_CMP, tm), const),
                  pl.BlockSpec((tm, LANES), lambda i: (i % nt, 0)),
                  pl.BlockSpec((tm, LANES), lambda i: (i % nt, 0)),
                  pl.BlockSpec((8, LANES), const),
                  pl.BlockSpec((1, LANES), const)],
        out_specs=out_specs,
        out_shape=out_shape,
        compiler_params=_cparams(("parallel",)),
        name="proj",
    )(x2, lw["norm_w"], lw["w_pack"], lw["bd"], _pool_matrix(tm // L_CMP, tm), cos_t, sin_t,
      lw["gains"], lw["bf"])


def _merge_kernel(x_ref, oa_ref, ob_ref, oc_ref, gate_ref, w_ref, y_ref):
    g = gate_ref[...]
    wa, wb = H_A * HD, H_A * HD + H_B * HD
    acc = _dot((oa_ref[...] * g[:, 0:wa]).astype(bf16), w_ref[0:wa, :])
    acc += _dot((ob_ref[...] * g[:, wa:wb]).astype(bf16), w_ref[wa:wb, :])
    acc += _dot((oc_ref[...] * g[:, wb:]).astype(bf16), w_ref[wb:, :])
    y_ref[...] = x_ref[...] + acc


def _merge(x2, oa, ob, oc, gate, w_out, tm):
    n = x2.shape[0]
    rows = lambda i: (i, 0)
    return pl.pallas_call(
        _merge_kernel,
        grid=(n // tm,),
        in_specs=[pl.BlockSpec((tm, 1024), rows), pl.BlockSpec((tm, 512), rows),
                  pl.BlockSpec((tm, 256), rows), pl.BlockSpec((tm, 256), rows),
                  pl.BlockSpec((tm, 1024), rows), pl.BlockSpec((1024, 1024), lambda i: (0, 0))],
        out_specs=pl.BlockSpec((tm, 1024), rows),
        out_shape=jax.ShapeDtypeStruct((n, 1024), f32),
        compiler_params=_cparams(("parallel",)),
        name="merge",
    )(x2, oa, ob, oc, gate, w_out)


def _ret_prompt_kernel(q_ref, k_ref, v_ref, dm_ref, xi_ref, zeta_ref, gc_ref, gnw_ref, bd_ref,
                       o_ref, r_ref):
    c = RET_CHUNK
    n_chunks = q_ref.shape[1] // c
    lo = _lane((c, LANES)) < HD
    same_head = (_row((LANES, LANES)) < HD) == (_lane((LANES, LANES)) < HD)
    bd = bd_ref[...]
    xi = xi_ref[0]
    zeta = zeta_ref[0]
    gnw = gnw_ref[...]

    def step(i, r):
        rows = pl.ds(pl.multiple_of(i * c, c), c)
        q = q_ref[0, rows, :]
        k = k_ref[0, rows, :]
        v = v_ref[0, rows, :]
        zero = jnp.zeros_like(q)
        a0 = _dot_nt(jnp.where(lo, q, zero), k) * dm_ref[0]
        a1 = _dot_nt(jnp.where(lo, zero, q), k) * dm_ref[1]
        o = jnp.where(lo, _dot(a0.astype(bf16), v), _dot(a1.astype(bf16), v))
        o = o + _dot(q, r.astype(bf16)) * xi
        kz_t = (k.astype(f32) * zeta).T.astype(bf16)
        r = r * gc_ref[0] + jnp.where(same_head, _dot(kz_t, v), 0.0)
        mu = _dot_split(o, bd)
        d = o - mu
        var = _dot_split(d * d, bd)
        o_ref[0, rows, :] = d * lax.rsqrt(var + EPS) * gnw
        return r

    r = lax.fori_loop(0, n_chunks, step, jnp.zeros((LANES, LANES), f32))
    r_ref[0, 0] = r[0:HD, 0:HD]
    r_ref[0, 1] = pltpu.roll(r, HD, 1)[HD:2 * HD, 0:HD]


def _ret_prompt(cqkv, lw, b, t):
    blk = lambda u0: pl.BlockSpec((1, t, LANES), lambda i, p: (i, 0, u0 + p))
    pair = lambda shape: pl.BlockSpec(shape, lambda i, p: (p,) + (0,) * (len(shape) - 1))
    return pl.pallas_call(
        _ret_prompt_kernel,
        grid=(b, 2),
        in_specs=[blk(0), blk(2), blk(4),
                  pair((2, RET_CHUNK, RET_CHUNK)), pair((1, RET_CHUNK, LANES)),
                  pair((1, RET_CHUNK, LANES)), pair((1, 1, LANES)),
                  pl.BlockSpec((1, LANES), lambda i, p: (0, p)),
                  pl.BlockSpec((LANES, LANES), lambda i, p: (0, 0))],
        out_specs=[pl.BlockSpec((1, t, LANES), lambda i, p: (i, 0, p)),
                   pl.BlockSpec((1, 2, HD, HD), lambda i, p: (i, p, 0, 0))],
        out_shape=[jax.ShapeDtypeStruct((b, t, H_C * HD), f32),
                   jax.ShapeDtypeStruct((b, H_C, HD, HD), f32)],
        compiler_params=_cparams(("parallel", "parallel")),
        name="ret_prompt",
    )(cqkv, cqkv, cqkv, lw["ret_dmask"], lw["ret_xi"], lw["ret_zeta"], lw["ret_gc"], lw["ret_gn_w"],
      lw["bd"])


def _fox_prep_kernel(misc_ref, ltri_ref, ccol_ref, crow_ref):
    c = LANES
    n_chunks = misc_ref.shape[1] // c
    ltri = ltri_ref[...]

    def step(i, carry):
        rows = pl.ds(pl.multiple_of(i * c, c), c)
        cs = _dot_split_l(ltri, misc_ref[0, rows, :], 3) + carry
        for p in range(2):
            sh = pltpu.roll(cs, LANES - (MISC_G + 2 * p), 1)
            ccol_ref[0, p, rows, :] = sh
            crow_ref[0, p, i] = sh.T[0:8, :]
        return cs[c - 1:c, :]

    lax.fori_loop(0, n_chunks, step, jnp.zeros((1, LANES), f32))


def _fox_prep(misc3, ltri):
    b, t, _ = misc3.shape
    return pl.pallas_call(
        _fox_prep_kernel,
        grid=(b,),
        in_specs=[pl.BlockSpec((1, t, LANES), lambda i: (i, 0, 0)),
                  pl.BlockSpec((LANES, LANES), lambda i: (0, 0))],
        out_specs=[pl.BlockSpec((1, 2, t, LANES), lambda i: (i, 0, 0, 0)),
                   pl.BlockSpec((1, 2, t // LANES, 8, LANES), lambda i: (i, 0, 0, 0, 0))],
        out_shape=[jax.ShapeDtypeStruct((b, 2, t, LANES), f32),
                   jax.ShapeDtypeStruct((b, 2, t // LANES, 8, LANES), f32)],
        compiler_params=_cparams(("parallel",)),
        name="fox_prep",
    )(misc3, ltri)


def _softmax_step2(s, m, l, acc, v):
    s_a, s_b = s[:, 0:LANES], s[:, LANES:2 * LANES]
    m_new = jnp.maximum(m, jnp.max(jnp.maximum(s_a, s_b), axis=1, keepdims=True))
    alpha = jnp.exp(m - m_new)
    p_a = jnp.exp(s_a - m_new)
    p_b = jnp.exp(s_b - m_new)
    l = alpha * l + (p_a + p_b)
    pv = _dot(p_a.astype(bf16), v[0:LANES]) + _dot(p_b.astype(bf16), v[LANES:2 * LANES])
    return m_new, l, alpha * acc + pv


def _softmax_finish(l, acc):
    return acc / jnp.sum(l, axis=1, keepdims=True)


def _fox_prompt_kernel(q_ref, k_ref, v_ref, ccol_ref, crow_ref, o_ref, m_scr, l_scr, acc_scr):
    tq = q_ref.shape[1]
    qi = pl.program_id(1)
    pairs = range(H_B // 2)
    q2, cq = [], []
    for p in pairs:
        q2.append(jnp.concatenate([q_ref[0, :, (2 * p) * LANES:(2 * p + 1) * LANES],
                                   q_ref[0, :, (2 * p + 1) * LANES:(2 * p + 2) * LANES]], axis=0))
        cc = ccol_ref[0, p]
        cq.append(jnp.broadcast_to(jnp.concatenate([cc[:, 0:1], cc[:, 1:2]], axis=0), (2 * tq, tq)))
    m_scr[...] = jnp.full(m_scr.shape, NEG, f32)
    l_scr[...] = jnp.zeros(l_scr.shape, f32)
    acc_scr[...] = jnp.zeros(acc_scr.shape, f32)

    def tile(kt, masked):
        rows = pl.ds(pl.multiple_of(kt * tq, tq), tq)
        for p in pairs:
            k = k_ref[0, rows, p * LANES:(p + 1) * LANES].astype(bf16)
            v = v_ref[0, rows, p * LANES:(p + 1) * LANES].astype(bf16)
            ck = jnp.concatenate([crow_ref[0, p, 2 * kt], crow_ref[0, p, 2 * kt + 1]], axis=1)
            ck2 = jnp.concatenate([jnp.broadcast_to(ck[0:1, :], (tq, tq)),
                                   jnp.broadcast_to(ck[1:2, :], (tq, tq))], axis=0)
            s = _dot_nt(q2[p], k) + (cq[p] - ck2)
            if masked:
                causal = _lane((tq, tq)) <= _row((tq, tq))
                s = jnp.where(jnp.concatenate([causal, causal], axis=0), s, NEG)
            m, l, acc = _softmax_step2(s, m_scr[p], l_scr[p], acc_scr[p], v)
            m_scr[p] = m
            l_scr[p] = l
            acc_scr[p] = acc

    def body(kt, carry):
        tile(kt, False)
        return carry

    lax.fori_loop(0, qi, body, 0)
    tile(qi, True)
    lo = _lane((tq, LANES)) < HD
    for p in pairs:
        o = _softmax_finish(l_scr[p], acc_scr[p])
        o_ref[0, :, p * LANES:(p + 1) * LANES] = jnp.where(lo, o[0:tq], o[tq:2 * tq])


def _fox_prompt(bqal, fkv, ccol, crow, b, t):
    tq = 2 * LANES
    scratch = pltpu.VMEM((H_B // 2, 2 * tq, LANES), f32)
    return pl.pallas_call(
        _fox_prompt_kernel,
        grid=(b, t // tq),
        in_specs=[pl.BlockSpec((1, tq, H_B * LANES), lambda i, j: (i, j, 0)),
                  pl.BlockSpec((1, t, 2 * LANES), lambda i, j: (i, 0, 0)),
                  pl.BlockSpec((1, t, 2 * LANES), lambda i, j: (i, 0, 1)),
                  pl.BlockSpec((1, 2, tq, LANES), lambda i, j: (i, 0, j, 0)),
                  pl.BlockSpec((1, 2, t // LANES, 8, LANES), lambda i, j: (i, 0, 0, 0, 0))],
        out_specs=pl.BlockSpec((1, tq, H_B * HD), lambda i, j: (i, j, 0)),
        out_shape=jax.ShapeDtypeStruct((b, t, H_B * HD), f32),
        scratch_shapes=[scratch, scratch, scratch],
        compiler_params=_cparams(("parallel", "parallel")),
        name="fox_prompt",
    )(bqal, fkv, fkv, ccol, crow)


def _top_n_mask(score, n):
    lane = _lane(score.shape).astype(f32)
    sel = jnp.zeros(score.shape, f32)
    for _ in range(n):
        m = jnp.broadcast_to(jnp.max(score, axis=1, keepdims=True), score.shape)
        first = jnp.min(jnp.where(score == m, lane, float(LANES)), axis=1, keepdims=True)
        pick = lane == jnp.broadcast_to(first, score.shape)
        sel = jnp.where(pick, 1.0, sel)
        score = jnp.where(pick, -2.0, score)
    return sel


def _masked_softmax_rows(s, valid):
    s = jnp.where(valid, s, NEG)
    m = jnp.broadcast_to(jnp.max(s, axis=1, keepdims=True), s.shape)
    m = jnp.where(m > 0.5 * NEG, m, 0.0)
    p = jnp.where(valid, jnp.exp(s - m), 0.0)
    den = jnp.broadcast_to(jnp.sum(p, axis=1, keepdims=True), s.shape)
    return p / jnp.where(den > 0, den, 1.0)


def _select_blocks(imp, cur2):
    return _select_pairs(imp + pltpu.roll(imp, LANES - 1, 1), cur2)


def _select_pairs(imp2, cur2):
    lane = _lane(imp2.shape)
    is_blk = (lane & 1) == 0
    valid = is_blk & (lane <= cur2)
    forced = (lane == 0) | (lane == cur2) | (lane == cur2 - 2)
    score = jnp.where(valid, imp2 + jnp.where(forced, FORCE_BONUS, 0.0), -1.0)
    return _top_n_mask(score, TOP_N)


def _nsa_prompt_kernel(q_ref, kvb_ref, slc_ref, win_ref, misc_ref, exp_ref, blkend_ref,
                       o_ref, m_scr, l_scr, acc_scr):
    tq = q_ref.shape[1]
    tk = 2 * LANES
    qi = pl.program_id(1)
    t_col = qi * tq + _row((tq, 1))
    t8 = jnp.concatenate([t_col] * H_A, axis=0)
    lane = _lane((tq, LANES))
    lo = lane < HD
    key = _lane((tq, tk))
    misc = misc_ref[0]
    nb = kvb_ref.shape[1]
    zpad = jnp.zeros((LANES - nb, LANES), bf16)
    q8 = jnp.concatenate([q_ref[0, :, g * LANES:(g + 1) * LANES] for g in range(H_A)], axis=0)

    kb = jnp.concatenate([kvb_ref[0, :, 0:LANES].astype(bf16), zpad], axis=0)
    vb = jnp.concatenate([kvb_ref[0, :, LANES:2 * LANES].astype(bf16), zpad], axis=0)
    p = _masked_softmax_rows(_dot_nt(q8, kb), blkend_ref[...] <= t8)
    o_cmp = _dot(p.astype(bf16), vb)
    sel = []
    for kv in range(KV_A):
        imp = p[GQA_A * kv * tq:(GQA_A * kv + 1) * tq]
        for j in range(1, GQA_A):
            imp = imp + p[(GQA_A * kv + j) * tq:(GQA_A * kv + j + 1) * tq]
        sel.append(_select_blocks(imp, (t_col // L_SEL) * 2).astype(bf16))

    def slc_bias(kt):
        causal = (kt * tk + key) <= t_col
        per_kv = [jnp.where((_dot(sel[kv], exp_ref[kt]) > 0.5) & causal, 0.0, NEG) for kv in range(KV_A)]
        return jnp.concatenate([per_kv[g // GQA_A] for g in range(H_A)], axis=0)

    def win_bias(kt):
        d = t_col - (kt * tk + key)
        return jnp.concatenate([jnp.where((d >= 0) & (d < WINDOW), 0.0, NEG)] * H_A, axis=0)

    def flash(kv_ref, kt_lo, kt_hi, bias_fn):
        m_scr[...] = jnp.full(m_scr.shape, NEG, f32)
        l_scr[...] = jnp.zeros(l_scr.shape, f32)
        acc_scr[...] = jnp.zeros(acc_scr.shape, f32)

        def body(kt, carry):
            rows = pl.ds(pl.multiple_of(kt * tk, tk), tk)
            k = kv_ref[0, rows, 0:LANES].astype(bf16)
            v = kv_ref[0, rows, LANES:2 * LANES].astype(bf16)
            s = _dot_nt(q8, k) + bias_fn(kt)
            m, l, acc = _softmax_step2(s, m_scr[...], l_scr[...], acc_scr[...], v)
            m_scr[...] = m
            l_scr[...] = l
            acc_scr[...] = acc
            return carry

        lax.fori_loop(kt_lo, kt_hi, body, 0)
        return _softmax_finish(l_scr[...], acc_scr[...])

    kt_end = (qi * tq) // tk + 1
    o_slc = flash(slc_ref, 0, kt_end, slc_bias)
    o_win = flash(win_ref, jnp.maximum(qi * tq - WINDOW, 0) // tk, kt_end, win_bias)

    heads = []
    for g in range(H_A):
        rows = slice(g * tq, (g + 1) * tq)
        o = (misc[:, 3 * g:3 * g + 1] * o_cmp[rows] + misc[:, 3 * g + 1:3 * g + 2] * o_slc[rows]
             + misc[:, 3 * g + 2:3 * g + 3] * o_win[rows])
        heads.append(o if (g % 2) == g // GQA_A else pltpu.roll(o, HD, 1))
    for u in range(H_A // 2):
        o_ref[0, :, u * LANES:(u + 1) * LANES] = jnp.where(lo, heads[2 * u], heads[2 * u + 1])


def _nsa_prompt(qal, kvb, slc, win, misc3, consts, b, t, tq):
    nb = t // L_CMP
    tk = 2 * LANES
    return pl.pallas_call(
        _nsa_prompt_kernel,
        grid=(b, t // tq),
        in_specs=[pl.BlockSpec((1, tq, H_A * LANES), lambda i, j: (i, j, 0)),
                  pl.BlockSpec((1, nb, 2 * LANES), lambda i, j: (i, 0, 0)),
                  pl.BlockSpec((1, t, 2 * LANES), lambda i, j: (i, 0, 0)),
                  pl.BlockSpec((1, t, 2 * LANES), lambda i, j: (i, 0, 0)),
                  pl.BlockSpec((1, tq, LANES), lambda i, j: (i, j, 0)),
                  pl.BlockSpec((t // tk, LANES, tk), lambda i, j: (0, 0, 0)),
                  pl.BlockSpec((1, LANES), lambda i, j: (0, 0))],
        out_specs=pl.BlockSpec((1, tq, H_A * HD), lambda i, j: (i, j, 0)),
        out_shape=jax.ShapeDtypeStruct((b, t, H_A * HD), f32),
        scratch_shapes=[pltpu.VMEM((H_A * tq, LANES), f32), pltpu.VMEM((H_A * tq, LANES), f32),
                        pltpu.VMEM((H_A * tq, LANES), f32)],
        compiler_params=_cparams(("parallel", "parallel")),
        name="nsa_prompt",
    )(qal, kvb, slc, win, misc3, consts["expand"], consts["blkend"])


def _rope_tables(pos):
    half = HD // 2
    inv = ROPE_THETA ** (-jnp.arange(half, dtype=f32) / half)
    ang = pos.astype(f32)[:, None] * inv[None, :]
    cos, sin = jnp.cos(ang), jnp.sin(ang)
    return (jnp.concatenate([cos, cos, cos, cos], axis=-1),
            jnp.concatenate([-sin, sin, -sin, sin], axis=-1))


def _constants(t, tq):
    lane = np.arange(LANES)
    bd = ((lane[:, None] // HD) == (lane[None, :] // HD)).astype(np.float32) / HD
    ltri = (lane[None, :] <= lane[:, None]).astype(np.float32)
    key = np.arange(t).reshape(t // tq, 1, tq)
    expand = ((lane[None, :, None] % 2 == 0) & (key // L_SEL == lane[None, :, None] // 2)).astype(np.float32)
    blkend = np.where(lane < t // L_CMP, (lane + 1) * L_CMP - 1, 2 ** 30).astype(np.int32)[None, :]
    log_g = jnp.log1p(-jnp.exp2(-5.0 - jnp.arange(H_C, dtype=f32)))
    i = jnp.arange(RET_CHUNK, dtype=f32)
    diff = i[:, None] - i[None, :]
    dmask = jnp.where(diff >= 0, jnp.exp(jnp.maximum(diff, 0.0) * log_g[:, None, None]), 0.0)
    xi = jnp.exp((i + 1.0)[None, :] * log_g[:, None])
    zeta = jnp.exp((RET_CHUNK - 1.0 - i)[None, :] * log_g[:, None])
    g_c = jnp.exp(RET_CHUNK * log_g)
    by_lane = lambda a: jnp.repeat(a.reshape(2, 2, -1).transpose(0, 2, 1), HD, axis=-1)
    return {"bd": jnp.asarray(bd, bf16), "pool8": _pool_matrix(8, 2 * PAGE),
            "ltri": jnp.asarray(ltri, bf16),
            "expand": jnp.asarray(expand, bf16), "blkend": jnp.asarray(blkend),
            "ret_dmask": dmask, "ret_xi": by_lane(xi), "ret_zeta": by_lane(zeta),
            "ret_gc": by_lane(g_c[:, None]), "log_g": log_g}


def _pack_w_in(w):
    off = np.cumsum([0, 512, 128, 128, 128, 128, 128, 128, 24, 512, 256, 256, 256, 4, 256, 256, 256, 256, 256])
    seg = lambda i: w[:, off[i]:off[i + 1]]
    (a_q, a_kc, a_vc, a_ks, a_vs, a_kw, a_vw, a_g, a_gate,
     b_q, b_k, b_v, b_f, b_gate, c_q, c_k, c_v, c_gate) = [seg(i) for i in range(18)]
    pad = jnp.zeros((w.shape[0], LANES - MISC_G - H_B), w.dtype)
    return jnp.concatenate([a_q, a_kc, a_vc, a_ks, a_vs, a_kw, a_vw, a_g, b_f, pad,
                            a_gate, b_gate, c_gate, b_q, b_k, b_v, c_q, c_k, c_v], axis=1).astype(bf16)


def _layer_params(l, consts, norm_w, w_in, fox_bf, nsa_q_norm, nsa_k_norm, fox_q_norm, fox_k_norm,
                  ret_gn_w, w_out):
    two = lambda g: jnp.concatenate([g, g])[None, :]
    gains = jnp.concatenate([two(nsa_q_norm[l]), two(nsa_k_norm[l, 0]), two(nsa_k_norm[l, 1]),
                             two(nsa_k_norm[l, 2]), two(fox_q_norm[l]), two(fox_k_norm[l]),
                             jnp.zeros((2, LANES), f32)], axis=0)
    bf = jnp.zeros((1, LANES), f32).at[0, MISC_G:MISC_G + H_B].set(fox_bf[l])
    lw = dict(consts)
    lw.update(norm_w=norm_w[l][None, :], w_pack=_pack_w_in(w_in[l]), gains=gains, bf=bf,
              ret_gn_w=ret_gn_w[l][None, :], w_out=w_out[l].astype(bf16))
    return lw


def _prompt_layer(x, lw, consts, cos_t, sin_t):
    b, t, d = x.shape
    tq = 128
    x2 = x.reshape(b * t, d)
    qal, cmp_, slc, win, misc, gate, bqal, fkv, cqkv, kvb = _proj(x2, lw, cos_t, sin_t, 256)
    r3 = lambda a: a.reshape(b, t, a.shape[-1])
    misc3 = r3(misc)
    o_a = _nsa_prompt(r3(qal), kvb.reshape(b, t // L_CMP, 2 * LANES), r3(slc), r3(win), misc3,
                      consts, b, t, tq)
    ccol, crow = _fox_prep(misc3, consts["ltri"])
    o_b = _fox_prompt(r3(bqal), r3(fkv), ccol, crow, b, t)
    o_c, r_new = _ret_prompt(r3(cqkv), lw, b, t)
    y = _merge(x2, o_a.reshape(b * t, -1), o_b.reshape(b * t, -1), o_c.reshape(b * t, -1), gate,
               lw["w_out"], 256)
    wk = min(WINDOW, t)
    states = (cmp_.reshape(b, t, 2, KV_A, HD), slc.reshape(b, t, 2, KV_A, HD),
              fkv.reshape(b, t, 2, H_B, HD), misc3[:, :, MISC_G:MISC_G + H_B],
              r3(win)[:, t - wk:].reshape(b, wk, 2, KV_A, HD), r_new)
    return y.reshape(b, t, d), states


def _head_col(vec_row, shape, base, stride):
    pick = _lane(shape) == base + stride * _row(shape)
    return jnp.sum(jnp.where(pick, jnp.broadcast_to(vec_row, shape), 0.0), axis=1, keepdims=True)


def _seg_mean(x, seg):
    lane = _lane(x.shape)
    sh = seg // 2
    while sh >= 1:
        x = x + jnp.where((lane & sh) != 0, pltpu.roll(x, sh, 1), pltpu.roll(x, LANES - sh, 1))
        sh //= 2
    return x * (1.0 / seg)


def _tree(op, xs):
    xs = list(xs)
    while len(xs) > 1:
        xs = [op(xs[i], xs[i + 1]) if i + 1 < len(xs) else xs[i] for i in range(0, len(xs), 2)]
    return xs[0]


def _dec_nsa_kernel(pt_ref, q_ref, slc_new_ref, win_new_ref, misc_ref, win_ref, *rest, n_pages):
    cmp_pages = rest[0:n_pages]
    slc_pages = rest[n_pages:2 * n_pages]
    o_ref = rest[2 * n_pages]
    shape = (H_A, LANES)
    lane = _lane(shape)
    row = _row(shape)
    lo = lane < HD
    q8 = q_ref[0]
    qf = q8.astype(f32)
    past = n_pages * PAGE
    cur2 = 2 * (past // L_SEL)
    bf_round = lambda a: a.astype(bf16).astype(f32)
    k_of = lambda pg: pg[0, 0, 0:LANES, :].astype(bf16)
    v_of = lambda pg: pg[0, 0, LANES:2 * LANES, :].astype(bf16)
    row_max = lambda xs: jnp.max(_tree(jnp.maximum, xs), axis=1, keepdims=True)
    row_sum = lambda xs: jnp.sum(_tree(jnp.add, xs), axis=1, keepdims=True)

    sb = [_seg_mean(_dot(q8, k_of(pg)), L_CMP) for pg in cmp_pages]
    m = row_max(sb)
    p = [jnp.exp(x - m) * (1.0 / L_CMP) for x in sb]
    cl = row_sum(p)
    o_cmp = _tree(jnp.add, [_dot_nt(x.astype(bf16), v_of(pg)) for x, pg in zip(p, cmp_pages)]) / cl
    to_prob = L_CMP / cl
    imp2 = jnp.zeros(shape, f32)
    for page, x in enumerate(p):
        e = x * to_prob
        e = jnp.where(row < GQA_A,
                      jnp.sum(e[0:GQA_A], axis=0, keepdims=True),
                      jnp.sum(e[GQA_A:2 * GQA_A], axis=0, keepdims=True))
        pair = e + pltpu.roll(e, LANES - L_CMP, 1)
        first = pltpu.roll(pair, 4 * page, 1) if page else pair
        imp2 = jnp.where(lane == 4 * page, first, imp2)
        imp2 = jnp.where(lane == 4 * page + 2, pltpu.roll(pair, (4 * page + 2 + HD) % LANES, 1), imp2)
    sel = _select_pairs(imp2, cur2)

    k_new = bf_round(slc_new_ref[0, :, 0:LANES])
    v_new = bf_round(slc_new_ref[0, :, LANES:2 * LANES])
    s_new = jnp.sum(qf * k_new, axis=1, keepdims=True)
    s = []
    for page, pg in enumerate(slc_pages):
        hit = jnp.where(lo, sel[:, 4 * page:4 * page + 1], sel[:, 4 * page + 2:4 * page + 3]) > 0.5
        s.append(jnp.where(hit, _dot(q8, k_of(pg)), NEG))
    m = jnp.maximum(row_max(s), s_new)
    p = [jnp.exp(x - m) for x in s]
    p_new = jnp.exp(s_new - m)
    acc = _tree(jnp.add, [_dot_nt(x.astype(bf16), v_of(pg)) for x, pg in zip(p, slc_pages)])
    o_slc = (acc + p_new * v_new) / (row_sum(p) + p_new)

    wk = win_ref.shape[3]
    kw = win_ref[0, 0, 0:LANES, :].astype(bf16)
    vw = win_ref[0, 0, LANES:2 * LANES, :].astype(bf16)
    kw_new = bf_round(win_new_ref[0, :, 0:LANES])
    vw_new = bf_round(win_new_ref[0, :, LANES:2 * LANES])
    s = jnp.where(_lane((H_A, wk)) > wk - WINDOW, _dot(q8, kw), NEG)
    s_new = jnp.sum(qf * kw_new, axis=1, keepdims=True)
    m = jnp.maximum(jnp.max(s, axis=1, keepdims=True), s_new)
    pw = jnp.exp(s - m)
    pw_new = jnp.exp(s_new - m)
    o_win = ((_dot_nt(pw.astype(bf16), vw) + pw_new * vw_new)
             / (jnp.sum(pw, axis=1, keepdims=True) + pw_new))

    misc = misc_ref[0]
    o8 = (_head_col(misc, shape, 0, 3) * o_cmp + _head_col(misc, shape, 1, 3) * o_slc
          + _head_col(misc, shape, 2, 3) * o_win)
    o8r = pltpu.roll(o8, HD, 1)
    lo1 = _lane((1, LANES)) < HD
    for u in range(H_A // 2):
        kv_lo = (2 * u) // GQA_A == 0
        even = (o8 if kv_lo else o8r)[2 * u:2 * u + 1, :]
        odd = (o8r if kv_lo else o8)[2 * u + 1:2 * u + 2, :]
        o_ref[0, :, u * LANES:(u + 1) * LANES] = jnp.where(lo1, even, odd)


def _page_specs(block, layer, count):
    def spec(g):
        return pl.BlockSpec(block, lambda b, pt: (layer, pt[b * count + g]) + (0,) * (len(block) - 2))
    return [spec(g) for g in range(count)]


def _dec_nsa(pt_flat, q8, slc_new, win_new, misc, win_state, cache_cmp, cache_slc, layer):
    n_seq = q8.shape[0]
    n_pages = pt_flat.shape[0] // n_seq
    wk = win_state.shape[3]
    seq3 = lambda w: pl.BlockSpec((1, 1, w), lambda b, pt: (b, 0, 0))
    page_block = (1, 1, 2 * LANES, PAGE)
    grid_spec = pltpu.PrefetchScalarGridSpec(
        num_scalar_prefetch=1,
        grid=(n_seq,),
        in_specs=[pl.BlockSpec((1, H_A, LANES), lambda b, pt: (b, 0, 0)),
                  seq3(2 * LANES), seq3(2 * LANES), seq3(LANES),
                  pl.BlockSpec((1, 1, 2 * LANES, wk), lambda b, pt: (layer, b, 0, 0))]
        + _page_specs(page_block, layer, n_pages) + _page_specs(page_block, layer, n_pages),
        out_specs=pl.BlockSpec((1, 1, H_A * HD), lambda b, pt: (b, 0, 0)))
    return pl.pallas_call(
        functools.partial(_dec_nsa_kernel, n_pages=n_pages),
        grid_spec=grid_spec,
        out_shape=jax.ShapeDtypeStruct((n_seq, 1, H_A * HD), f32),
        compiler_params=_cparams(("parallel",)),
        name="dec_nsa",
    )(pt_flat, q8, slc_new, win_new, misc, win_state, *([cache_cmp] * n_pages), *([cache_slc] * n_pages))


def _lane_cumsum(x):
    lane = _lane(x.shape)
    sh = 1
    while sh < LANES:
        x = x + jnp.where(lane >= sh, pltpu.roll(x, sh, 1), 0.0)
        sh *= 2
    return x


def _dec_fox_kernel(pt_ref, q_ref, kv_new_ref, misc_ref, *rest, n_pages):
    kv_pages = rest[0:n_pages]
    lf_pages = rest[n_pages:2 * n_pages]
    o_ref = rest[2 * n_pages]
    shape = (8, LANES)
    first_pair = _row(shape) < 2
    q8 = q_ref[0]
    unit = lambda kv, u: kv[0, 0, u * LANES:(u + 1) * LANES, :].astype(bf16)

    cs = [_lane_cumsum(lf[0, 0]) for lf in lf_pages]
    carry = jnp.zeros((8, 1), f32)
    s = []
    for kv, c in zip(kv_pages, cs):
        s.append(jnp.where(first_pair, _dot(q8, unit(kv, 0)), _dot(q8, unit(kv, 1))) - (carry + c))
        carry = carry + c[:, LANES - 1:LANES]
    bf_round = lambda a: jnp.broadcast_to(a.astype(bf16).astype(f32), shape)
    new = kv_new_ref[0]
    k_new = jnp.where(first_pair, bf_round(new[:, 0:LANES]), bf_round(new[:, LANES:2 * LANES]))
    v_new = jnp.where(first_pair, bf_round(new[:, 2 * LANES:3 * LANES]), bf_round(new[:, 3 * LANES:4 * LANES]))
    s_new = (jnp.sum(q8.astype(f32) * k_new, axis=1, keepdims=True)
             - (carry + _head_col(misc_ref[0], shape, MISC_G, 1)))
    m = jnp.maximum(jnp.max(_tree(jnp.maximum, s), axis=1, keepdims=True), s_new)
    p = [jnp.exp(x - m) for x in s]
    p_new = jnp.exp(s_new - m)
    den = jnp.sum(_tree(jnp.add, p), axis=1, keepdims=True) + p_new
    pb = [x.astype(bf16) for x in p]
    acc0 = _tree(jnp.add, [_dot_nt(x, unit(kv, 2)) for x, kv in zip(pb, kv_pages)])
    acc1 = _tree(jnp.add, [_dot_nt(x, unit(kv, 3)) for x, kv in zip(pb, kv_pages)])
    o8 = (jnp.where(first_pair, acc0, acc1) + p_new * v_new) / den
    lo1 = _lane((1, LANES)) < HD
    for u in range(H_B // 2):
        o_ref[0, :, u * LANES:(u + 1) * LANES] = jnp.where(lo1, o8[2 * u:2 * u + 1, :],
                                                           o8[2 * u + 1:2 * u + 2, :])


def _dec_fox(pt_flat, qb4, fkv_new, misc, cache_fkv, cache_lft, layer):
    n_seq = qb4.shape[0]
    n_pages = pt_flat.shape[0] // n_seq
    seq3 = lambda w: pl.BlockSpec((1, 1, w), lambda b, pt: (b, 0, 0))
    grid_spec = pltpu.PrefetchScalarGridSpec(
        num_scalar_prefetch=1,
        grid=(n_seq,),
        in_specs=[pl.BlockSpec((1, 8, LANES), lambda b, pt: (b, 0, 0)), seq3(4 * LANES), seq3(LANES)]
        + _page_specs((1, 1, 4 * LANES, PAGE), layer, n_pages)
        + _page_specs((1, 1, 8, PAGE), layer, n_pages),
        out_specs=pl.BlockSpec((1, 1, H_B * HD), lambda b, pt: (b, 0, 0)))
    return pl.pallas_call(
        functools.partial(_dec_fox_kernel, n_pages=n_pages),
        grid_spec=grid_spec,
        out_shape=jax.ShapeDtypeStruct((n_seq, 1, H_B * HD), f32),
        compiler_params=_cparams(("parallel",)),
        name="dec_fox",
    )(pt_flat, qb4, fkv_new, misc, *([cache_fkv] * n_pages), *([cache_lft] * n_pages))


def _dec_ret_kernel(q_ref, k_ref, v_ref, r_ref, gam_ref, gnw_ref, o_ref, rn_ref):
    q = q_ref[0].astype(f32)
    k = k_ref[0].astype(f32)
    v = v_ref[0].astype(f32)
    qk = jnp.sum(q * k, axis=1, keepdims=True)
    eye = _row((HD, HD)) == _lane((HD, HD))
    col = lambda a: jnp.sum(jnp.where(eye, jnp.broadcast_to(a, (HD, HD)), 0.0), axis=1, keepdims=True)
    for h in range(H_C):
        hs = slice(h, h + 1)
        r = r_ref[0, 0, h]
        gam = gam_ref[hs, :]
        o = qk[hs, :] * v[hs, :] + jnp.sum(col(q[hs, :]) * r, axis=0, keepdims=True) * gam
        rn_ref[0, h] = r * gam + col(k[hs, :]) * v[hs, :]
        d = o - jnp.mean(o, axis=1, keepdims=True)
        var = jnp.mean(d * d, axis=1, keepdims=True)
        o_ref[0, hs, :] = d * lax.rsqrt(var + EPS) * gnw_ref[hs, :]


def _dec_ret(cq, ck, cv, state_ret, gam, gnw, layer):
    n_seq = cq.shape[0]
    head = pl.BlockSpec((1, H_C, HD), lambda b: (b, 0, 0))
    table = pl.BlockSpec((H_C, HD), lambda b: (0, 0))
    return pl.pallas_call(
        _dec_ret_kernel,
        grid=(n_seq,),
        in_specs=[head, head, head,
                  pl.BlockSpec((1, 1, H_C, HD, HD), lambda b: (layer, b, 0, 0, 0)), table, table],
        out_specs=[head, pl.BlockSpec((1, H_C, HD, HD), lambda b: (b, 0, 0, 0))],
        out_shape=[jax.ShapeDtypeStruct((n_seq, H_C, HD), f32),
                   jax.ShapeDtypeStruct((n_seq, H_C, HD, HD), f32)],
        compiler_params=_cparams(("parallel",)),
        name="dec_ret",
    )(cq, ck, cv, state_ret, gam, gnw)


def _sample_layer(x, layer, lw, consts, caches, pt_flat, cos_t, sin_t):
    n, _, d = x.shape
    x2 = x.reshape(n, d)
    qal, cmp_, slc, win, misc, gate, bqal, fkv, cqkv, _ = _proj(x2, lw, cos_t, sin_t, n)
    row3 = lambda a: a[:, None, :]
    o_a8 = _dec_nsa(pt_flat, qal.reshape(n, H_A, LANES), row3(slc), row3(win), row3(misc),
                    caches["win"], caches["cmp"], caches["slc"], layer)
    qb8 = jnp.pad(bqal.reshape(n, H_B, LANES), ((0, 0), (0, 8 - H_B), (0, 0)))
    o_b8 = _dec_fox(pt_flat, qb8, row3(fkv), row3(misc),
                    caches["fkv"], caches["lft"], layer)
    heads = lambda u: cqkv[:, 2 * u * LANES:(2 * u + 2) * LANES].reshape(n, H_C, HD).astype(f32)
    gam = jnp.broadcast_to(jnp.exp(consts["log_g"])[:, None], (H_C, HD))
    o_c, r_new = _dec_ret(heads(0), heads(1), heads(2), caches["ret"], gam,
                          lw["ret_gn_w"].reshape(H_C, HD), layer)
    y = _merge(x2, o_a8.reshape(n, H_A * HD), o_b8.reshape(n, H_B * HD), o_c.reshape(n, H_C * HD), gate,
               lw["w_out"], n)
    new_win = jnp.concatenate([caches["win"][layer][:, :, 1:], win[:, :, None]], axis=2)
    wk = new_win.shape[2]
    new_win = jnp.transpose(new_win.reshape(n, 2, KV_A, HD, wk), (0, 4, 1, 2, 3))
    states = (cmp_.reshape(n, 1, 2, KV_A, HD), slc.reshape(n, 1, 2, KV_A, HD),
              fkv.reshape(n, 1, 2, H_B, HD), misc[:, None, MISC_G:MISC_G + H_B], new_win, r_new)
    return y.reshape(n, 1, d), states


def kernel(x_prompt, x_sample, cache_nsa_cmp, cache_nsa_slc, cache_fox_kv, cache_fox_logf,
           state_nsa_win, state_ret, page_table, norm_w, w_in, fox_bf, nsa_q_norm, nsa_k_norm,
           fox_q_norm, fox_k_norm, ret_gn_w, w_out):
    depth = norm_w.shape[0]
    b, t, _ = x_prompt.shape
    n_seq, n_pages = page_table.shape
    assert x_sample.shape[1] == 1 and t % 256 == 0 and n_pages % 2 == 0
    past = n_pages * PAGE
    consts = _constants(t, 2 * LANES)
    cos_p, sin_p = _rope_tables(jnp.arange(t, dtype=jnp.int32))
    cos_s, sin_s = _rope_tables(jnp.full((n_seq,), past, dtype=jnp.int32))
    n_phys = cache_nsa_cmp.shape[1]
    feat_major = lambda a, lead: jnp.transpose(a, (0, 1, 3, 4, 5, 2)).reshape(depth, lead, -1, a.shape[2])
    caches = {"cmp": feat_major(cache_nsa_cmp, n_phys), "slc": feat_major(cache_nsa_slc, n_phys),
              "fkv": feat_major(cache_fox_kv, n_phys),
              "lft": jnp.pad(jnp.swapaxes(cache_fox_logf, 2, 3), ((0, 0), (0, 0), (0, 8 - H_B), (0, 0))),
              "win": feat_major(state_nsa_win, n_seq),
              "ret": state_ret}
    pt_flat = page_table.reshape(-1)
    yp, ys = x_prompt, x_sample
    p_states, s_states = [], []
    for l in range(depth):
        lw = _layer_params(l, consts, norm_w, w_in, fox_bf, nsa_q_norm, nsa_k_norm, fox_q_norm,
                           fox_k_norm, ret_gn_w, w_out)
        yp, ps = _prompt_layer(yp, lw, consts, cos_p, sin_p)
        ys, ss = _sample_layer(ys, l, lw, consts, caches, pt_flat, cos_s, sin_s)
        p_states.append(ps)
        s_states.append(ss)
    stk = lambda states, i: jnp.stack([st[i] for st in states], axis=0)
    out = [yp, ys]
    for i in range(6):
        out += [stk(p_states, i), stk(s_states, i)]
    return tuple(out)
```

```python
import functools

import numpy as np
import jax
import jax.numpy as jnp
from jax import lax
from jax.experimental import pallas as pl
from jax.experimental.pallas import tpu as pltpu

f32 = jnp.float32
bf16 = jnp.bfloat16

HD = 64
H_A, KV_A, GQA_A = 8, 2, 4
H_B, H_C = 4, 4
L_CMP, L_SEL, TOP_N, WINDOW = 32, 64, 8, 512
PAGE = 128
RET_CHUNK = 128
ROPE_THETA = 10000.0
EPS = 1e-6
SCALE = HD ** -0.5
FORCE_BONUS = 1e3
NEG = -1e30

LANES = 128
VMEM_LIMIT = 48 * 1024 * 1024

C_AQ, C_CMP, C_SLC, C_WIN, C_MISC, C_GATE, C_BQ, C_BKV, C_CQKV, D_PACK = (
    0, 512, 768, 1024, 1280, 1408, 2432, 2688, 3200, 3968)
MISC_G = 3 * H_A


def _cparams(sem):
    return pltpu.CompilerParams(dimension_semantics=sem, vmem_limit_bytes=VMEM_LIMIT)


def _dot(a, b):
    return jnp.dot(a, b, preferred_element_type=f32)


def _dot_nt(a, b):
    return lax.dot_general(a, b, (((1,), (1,)), ((), ())), preferred_element_type=f32)


def _split(x, n):
    parts = []
    r = x
    for i in range(n):
        h = r.astype(bf16)
        parts.append(h)
        if i + 1 < n:
            r = r - h.astype(f32)
    return parts


def _dot_split(x, w, n=2):
    acc = None
    for h in _split(x, n):
        d = _dot(h, w)
        acc = d if acc is None else acc + d
    return acc


def _dot_split_l(w, x, n=2):
    acc = None
    for h in _split(x, n):
        d = _dot(w, h)
        acc = d if acc is None else acc + d
    return acc


def _lane(shape):
    return lax.broadcasted_iota(jnp.int32, shape, len(shape) - 1)


def _row(shape):
    return lax.broadcasted_iota(jnp.int32, shape, len(shape) - 2)


def _swap_half_heads(y):
    lane = _lane(y.shape)
    return jnp.where((lane & 32) == 0, pltpu.roll(y, 96, 1), pltpu.roll(y, 32, 1))


def _head_rms(z, bd, g):
    ms = _dot_split(z * z, bd)
    return z * lax.rsqrt(ms + EPS) * g


def _rope(y, cos, sin):
    return y * cos + _swap_half_heads(y) * sin


def _sigmoid(z):
    return 1.0 / (1.0 + jnp.exp(-z))


def _proj_kernel(x_ref, nw_ref, w_ref, bd_ref, pool_ref, cos_ref, sin_ref, gn_ref, bf_ref,
                 qal_ref, cmp_ref, slc_ref, win_ref, misc_ref, gate_ref, bqal_ref, fkv_ref,
                 cqkv_ref, kvb_ref):
    x = x_ref[...]
    ms = jnp.mean(x * x, axis=-1, keepdims=True)
    xn = (x * lax.rsqrt(ms + EPS) * nw_ref[...]).astype(bf16)
    bd = bd_ref[...]
    cos = cos_ref[...]
    sin = sin_ref[...]
    lane = _lane((x.shape[0], LANES))
    lo = lane < HD

    def z_of(c0, width):
        return _dot_nt(xn, w_ref[c0:c0 + width, :])

    def unit(z, u):
        return z[:, u * LANES:(u + 1) * LANES]

    z = z_of(C_AQ, H_A * HD)
    g_aq = gn_ref[0:1, :]
    for u in range(H_A // 2):
        y = _rope(_head_rms(unit(z, u), bd, g_aq), cos, sin) * SCALE
        yr = pltpu.roll(y, HD, 1)
        kv_lo = (2 * u) // GQA_A == 0
        keep = lo if kv_lo else jnp.logical_not(lo)
        even, odd = (y, yr) if kv_lo else (yr, y)
        qal_ref[:, (2 * u) * LANES:(2 * u + 1) * LANES] = jnp.where(keep, even, 0.0).astype(bf16)
        qal_ref[:, (2 * u + 1) * LANES:(2 * u + 2) * LANES] = jnp.where(keep, odd, 0.0).astype(bf16)

    for i, (c0, o_ref) in enumerate(((C_CMP, cmp_ref), (C_SLC, slc_ref), (C_WIN, win_ref))):
        z = z_of(c0, 2 * LANES)
        k = _rope(_head_rms(unit(z, 0), bd, gn_ref[1 + i:2 + i, :]), cos, sin)
        v = unit(z, 1)
        o_ref[:, 0:LANES] = k
        o_ref[:, LANES:2 * LANES] = v
        if i == 0:
            pool = pool_ref[...]
            kvb_ref[:, 0:LANES] = _dot_split_l(pool, k)
            kvb_ref[:, LANES:2 * LANES] = _dot_split_l(pool, v)

    z = z_of(C_MISC, LANES)
    zf = z + bf_ref[...]
    logf = jnp.minimum(zf, 0.0) - jnp.log1p(jnp.exp(-jnp.abs(zf)))
    misc_ref[...] = jnp.where(lane < MISC_G, _sigmoid(z), jnp.where(lane < MISC_G + H_B, logf, 0.0))

    for c in range(0, 1024, 512):
        z = z_of(C_GATE + c, 512)
        gate_ref[:, c:c + 512] = z * _sigmoid(z)

    z = z_of(C_BQ, H_B * HD)
    for u in range(H_B // 2):
        y = _head_rms(unit(z, u), bd, gn_ref[4:5, :]) * SCALE
        bqal_ref[:, (2 * u) * LANES:(2 * u + 1) * LANES] = jnp.where(lo, y, 0.0).astype(bf16)
        bqal_ref[:, (2 * u + 1) * LANES:(2 * u + 2) * LANES] = jnp.where(lo, 0.0, y).astype(bf16)
    z = z_of(C_BKV, 2 * H_B * HD)
    for u in range(2):
        fkv_ref[:, u * LANES:(u + 1) * LANES] = _head_rms(unit(z, u), bd, gn_ref[5:6, :])
    fkv_ref[:, 2 * LANES:4 * LANES] = z[:, 2 * LANES:4 * LANES]

    z = z_of(C_CQKV, 3 * H_C * HD)
    for u in range(2):
        cqkv_ref[:, u * LANES:(u + 1) * LANES] = _rope(unit(z, u), cos, sin).astype(bf16)
        cqkv_ref[:, (2 + u) * LANES:(3 + u) * LANES] = (_rope(unit(z, 2 + u), cos, sin) * SCALE).astype(bf16)
    cqkv_ref[:, 4 * LANES:6 * LANES] = z[:, 4 * LANES:6 * LANES].astype(bf16)


def _pool_matrix(n_out, n_rows):
    r = np.arange(n_rows)[None, :] // L_CMP == np.arange(n_out)[:, None]
    return jnp.asarray(r.astype(np.float32) / L_CMP, bf16)


def _proj(x2, lw, cos_t, sin_t, tm):
    n = x2.shape[0]
    nt = cos_t.shape[0] // tm
    const = lambda i: (0, 0)
    rows = lambda i: (i, 0)
    outs = [(H_A * LANES, bf16), (2 * LANES, f32), (2 * LANES, f32), (2 * LANES, f32), (LANES, f32),
            (1024, f32), (H_B * LANES, bf16), (4 * LANES, f32), (6 * LANES, bf16)]
    out_shape = [jax.ShapeDtypeStruct((n, w), dt) for w, dt in outs]
    out_specs = [pl.BlockSpec((tm, w), rows) for w, _ in outs]
    out_shape.append(jax.ShapeDtypeStruct((n // L_CMP, 2 * LANES), f32))
    out_specs.append(pl.BlockSpec((tm // L_CMP, 2 * LANES), rows))
    return pl.pallas_call(
        _proj_kernel,
        grid=(n // tm,),
        in_specs=[pl.BlockSpec((tm, 1024), rows),
                  pl.BlockSpec((1, 1024), const),
                  pl.BlockSpec((D_PACK, 1024), const),
                  pl.BlockSpec((LANES, LANES), const),
                  pl.BlockSpec((tm // L_CMP, tm), const),
                  pl.BlockSpec((tm, LANES), lambda i: (i % nt, 0)),
                  pl.BlockSpec((tm, LANES), lambda i: (i % nt, 0)),
                  pl.BlockSpec((8, LANES), const),
                  pl.BlockSpec((1, LANES), const)],
        out_specs=out_specs,
        out_shape=out_shape,
        compiler_params=_cparams(("parallel",)),
        name="proj",
    )(x2, lw["norm_w"], lw["w_pack"], lw["bd"], _pool_matrix(tm // L_CMP, tm), cos_t, sin_t,
      lw["gains"], lw["bf"])


def _merge_kernel(x_ref, oa_ref, ob_ref, oc_ref, gate_ref, w_ref, y_ref):
    g = gate_ref[...]
    wa, wb = H_A * HD, H_A * HD + H_B * HD
    acc = _dot((oa_ref[...] * g[:, 0:wa]).astype(bf16), w_ref[0:wa, :])
    acc += _dot((ob_ref[...] * g[:, wa:wb]).astype(bf16), w_ref[wa:wb, :])
    acc += _dot((oc_ref[...] * g[:, wb:]).astype(bf16), w_ref[wb:, :])
    y_ref[...] = x_ref[...] + acc


def _merge(x2, oa, ob, oc, gate, w_out, tm):
    n = x2.shape[0]
    rows = lambda i: (i, 0)
    return pl.pallas_call(
        _merge_kernel,
        grid=(n // tm,),
        in_specs=[pl.BlockSpec((tm, 1024), rows), pl.BlockSpec((tm, 512), rows),
                  pl.BlockSpec((tm, 256), rows), pl.BlockSpec((tm, 256), rows),
                  pl.BlockSpec((tm, 1024), rows), pl.BlockSpec((1024, 1024), lambda i: (0, 0))],
        out_specs=pl.BlockSpec((tm, 1024), rows),
        out_shape=jax.ShapeDtypeStruct((n, 1024), f32),
        compiler_params=_cparams(("parallel",)),
        name="merge",
    )(x2, oa, ob, oc, gate, w_out)


def _ret_prompt_kernel(q_ref, k_ref, v_ref, dm_ref, xi_ref, zeta_ref, gc_ref, gnw_ref, bd_ref,
                       o_ref, r_ref):
    c = RET_CHUNK
    n_chunks = q_ref.shape[1] // c
    lo = _lane((c, LANES)) < HD
    same_head = (_row((LANES, LANES)) < HD) == (_lane((LANES, LANES)) < HD)
    bd = bd_ref[...]
    xi = xi_ref[0]
    zeta = zeta_ref[0]
    gnw = gnw_ref[...]

    def step(i, r):
        rows = pl.ds(pl.multiple_of(i * c, c), c)
        q = q_ref[0, rows, :]
        k = k_ref[0, rows, :]
        v = v_ref[0, rows, :]
        zero = jnp.zeros_like(q)
        a0 = _dot_nt(jnp.where(lo, q, zero), k) * dm_ref[0]
        a1 = _dot_nt(jnp.where(lo, zero, q), k) * dm_ref[1]
        o = jnp.where(lo, _dot(a0.astype(bf16), v), _dot(a1.astype(bf16), v))
        o = o + _dot(q, r.astype(bf16)) * xi
        kz_t = (k.astype(f32) * zeta).T.astype(bf16)
        r = r * gc_ref[0] + jnp.where(same_head, _dot(kz_t, v), 0.0)
        mu = _dot_split(o, bd)
        d = o - mu
        var = _dot_split(d * d, bd)
        o_ref[0, rows, :] = d * lax.rsqrt(var + EPS) * gnw
        return r

    r = lax.fori_loop(0, n_chunks, step, jnp.zeros((LANES, LANES), f32))
    r_ref[0, 0] = r[0:HD, 0:HD]
    r_ref[0, 1] = pltpu.roll(r, HD, 1)[HD:2 * HD, 0:HD]


def _ret_prompt(cqkv, lw, b, t):
    blk = lambda u0: pl.BlockSpec((1, t, LANES), lambda i, p: (i, 0, u0 + p))
    pair = lambda shape: pl.BlockSpec(shape, lambda i, p: (p,) + (0,) * (len(shape) - 1))
    return pl.pallas_call(
        _ret_prompt_kernel,
        grid=(b, 2),
        in_specs=[blk(0), blk(2), blk(4),
                  pair((2, RET_CHUNK, RET_CHUNK)), pair((1, RET_CHUNK, LANES)),
                  pair((1, RET_CHUNK, LANES)), pair((1, 1, LANES)),
                  pl.BlockSpec((1, LANES), lambda i, p: (0, p)),
                  pl.BlockSpec((LANES, LANES), lambda i, p: (0, 0))],
        out_specs=[pl.BlockSpec((1, t, LANES), lambda i, p: (i, 0, p)),
                   pl.BlockSpec((1, 2, HD, HD), lambda i, p: (i, p, 0, 0))],
        out_shape=[jax.ShapeDtypeStruct((b, t, H_C * HD), f32),
                   jax.ShapeDtypeStruct((b, H_C, HD, HD), f32)],
        compiler_params=_cparams(("parallel", "parallel")),
        name="ret_prompt",
    )(cqkv, cqkv, cqkv, lw["ret_dmask"], lw["ret_xi"], lw["ret_zeta"], lw["ret_gc"], lw["ret_gn_w"],
      lw["bd"])


def _fox_prep_kernel(misc_ref, ltri_ref, ccol_ref, crow_ref):
    c = LANES
    n_chunks = misc_ref.shape[1] // c
    ltri = ltri_ref[...]

    def step(i, carry):
        rows = pl.ds(pl.multiple_of(i * c, c), c)
        cs = _dot_split_l(ltri, misc_ref[0, rows, :], 3) + carry
        for p in range(2):
            sh = pltpu.roll(cs, LANES - (MISC_G + 2 * p), 1)
            ccol_ref[0, p, rows, :] = sh
            crow_ref[0, p, i] = sh.T[0:8, :]
        return cs[c - 1:c, :]

    lax.fori_loop(0, n_chunks, step, jnp.zeros((1, LANES), f32))


def _fox_prep(misc3, ltri):
    b, t, _ = misc3.shape
    return pl.pallas_call(
        _fox_prep_kernel,
        grid=(b,),
        in_specs=[pl.BlockSpec((1, t, LANES), lambda i: (i, 0, 0)),
                  pl.BlockSpec((LANES, LANES), lambda i: (0, 0))],
        out_specs=[pl.BlockSpec((1, 2, t, LANES), lambda i: (i, 0, 0, 0)),
                   pl.BlockSpec((1, 2, t // LANES, 8, LANES), lambda i: (i, 0, 0, 0, 0))],
        out_shape=[jax.ShapeDtypeStruct((b, 2, t, LANES), f32),
                   jax.ShapeDtypeStruct((b, 2, t // LANES, 8, LANES), f32)],
        compiler_params=_cparams(("parallel",)),
        name="fox_prep",
    )(misc3, ltri)


def _softmax_step2(s, m, l, acc, v):
    s_a, s_b = s[:, 0:LANES], s[:, LANES:2 * LANES]
    m_new = jnp.maximum(m, jnp.max(jnp.maximum(s_a, s_b), axis=1, keepdims=True))
    alpha = jnp.exp(m - m_new)
    p_a = jnp.exp(s_a - m_new)
    p_b = jnp.exp(s_b - m_new)
    l = alpha * l + (p_a + p_b)
    pv = _dot(p_a.astype(bf16), v[0:LANES]) + _dot(p_b.astype(bf16), v[LANES:2 * LANES])
    return m_new, l, alpha * acc + pv


def _softmax_finish(l, acc):
    return acc / jnp.sum(l, axis=1, keepdims=True)


def _fox_prompt_kernel(q_ref, k_ref, v_ref, ccol_ref, crow_ref, o_ref, m_scr, l_scr, acc_scr):
    tq = q_ref.shape[1]
    qi = pl.program_id(1)
    pairs = range(H_B // 2)
    q2, cq = [], []
    for p in pairs:
        q2.append(jnp.concatenate([q_ref[0, :, (2 * p) * LANES:(2 * p + 1) * LANES],
                                   q_ref[0, :, (2 * p + 1) * LANES:(2 * p + 2) * LANES]], axis=0))
        cc = ccol_ref[0, p]
        cq.append(jnp.broadcast_to(jnp.concatenate([cc[:, 0:1], cc[:, 1:2]], axis=0), (2 * tq, tq)))
    m_scr[...] = jnp.full(m_scr.shape, NEG, f32)
    l_scr[...] = jnp.zeros(l_scr.shape, f32)
    acc_scr[...] = jnp.zeros(acc_scr.shape, f32)

    def tile(kt, masked):
        rows = pl.ds(pl.multiple_of(kt * tq, tq), tq)
        for p in pairs:
            k = k_ref[0, rows, p * LANES:(p + 1) * LANES].astype(bf16)
            v = v_ref[0, rows, p * LANES:(p + 1) * LANES].astype(bf16)
            ck = jnp.concatenate([crow_ref[0, p, 2 * kt], crow_ref[0, p, 2 * kt + 1]], axis=1)
            ck2 = jnp.concatenate([jnp.broadcast_to(ck[0:1, :], (tq, tq)),
                                   jnp.broadcast_to(ck[1:2, :], (tq, tq))], axis=0)
            s = _dot_nt(q2[p], k) + (cq[p] - ck2)
            if masked:
                causal = _lane((tq, tq)) <= _row((tq, tq))
                s = jnp.where(jnp.concatenate([causal, causal], axis=0), s, NEG)
            m, l, acc = _softmax_step2(s, m_scr[p], l_scr[p], acc_scr[p], v)
            m_scr[p] = m
            l_scr[p] = l
            acc_scr[p] = acc

    def body(kt, carry):
        tile(kt, False)
        return carry

    lax.fori_loop(0, qi, body, 0)
    tile(qi, True)
    lo = _lane((tq, LANES)) < HD
    for p in pairs:
        o = _softmax_finish(l_scr[p], acc_scr[p])
        o_ref[0, :, p * LANES:(p + 1) * LANES] = jnp.where(lo, o[0:tq], o[tq:2 * tq])


def _fox_prompt(bqal, fkv, ccol, crow, b, t):
    tq = 2 * LANES
    scratch = pltpu.VMEM((H_B // 2, 2 * tq, LANES), f32)
    return pl.pallas_call(
        _fox_prompt_kernel,
        grid=(b, t // tq),
        in_specs=[pl.BlockSpec((1, tq, H_B * LANES), lambda i, j: (i, j, 0)),
                  pl.BlockSpec((1, t, 2 * LANES), lambda i, j: (i, 0, 0)),
                  pl.BlockSpec((1, t, 2 * LANES), lambda i, j: (i, 0, 1)),
                  pl.BlockSpec((1, 2, tq, LANES), lambda i, j: (i, 0, j, 0)),
                  pl.BlockSpec((1, 2, t // LANES, 8, LANES), lambda i, j: (i, 0, 0, 0, 0))],
        out_specs=pl.BlockSpec((1, tq, H_B * HD), lambda i, j: (i, j, 0)),
        out_shape=jax.ShapeDtypeStruct((b, t, H_B * HD), f32),
        scratch_shapes=[scratch, scratch, scratch],
        compiler_params=_cparams(("parallel", "parallel")),
        name="fox_prompt",
    )(bqal, fkv, fkv, ccol, crow)


def _top_n_mask(score, n):
    lane = _lane(score.shape).astype(f32)
    sel = jnp.zeros(score.shape, f32)
    for _ in range(n):
        m = jnp.broadcast_to(jnp.max(score, axis=1, keepdims=True), score.shape)
        first = jnp.min(jnp.where(score == m, lane, float(LANES)), axis=1, keepdims=True)
        pick = lane == jnp.broadcast_to(first, score.shape)
        sel = jnp.where(pick, 1.0, sel)
        score = jnp.where(pick, -2.0, score)
    return sel


def _masked_softmax_rows(s, valid):
    s = jnp.where(valid, s, NEG)
    m = jnp.broadcast_to(jnp.max(s, axis=1, keepdims=True), s.shape)
    m = jnp.where(m > 0.5 * NEG, m, 0.0)
    p = jnp.where(valid, jnp.exp(s - m), 0.0)
    den = jnp.broadcast_to(jnp.sum(p, axis=1, keepdims=True), s.shape)
    return p / jnp.where(den > 0, den, 1.0)


def _select_blocks(imp, cur2):
    return _select_pairs(imp + pltpu.roll(imp, LANES - 1, 1), cur2)


def _select_pairs(imp2, cur2):
    lane = _lane(imp2.shape)
    is_blk = (lane & 1) == 0
    valid = is_blk & (lane <= cur2)
    forced = (lane == 0) | (lane == cur2) | (lane == cur2 - 2)
    score = jnp.where(valid, imp2 + jnp.where(forced, FORCE_BONUS, 0.0), -1.0)
    return _top_n_mask(score, TOP_N)


def _nsa_prompt_kernel(q_ref, kvb_ref, slc_ref, win_ref, misc_ref, exp_ref, blkend_ref,
                       o_ref, m_scr, l_scr, acc_scr):
    tq = q_ref.shape[1]
    tk = 2 * LANES
    qi = pl.program_id(1)
    t_col = qi * tq + _row((tq, 1))
    t8 = jnp.concatenate([t_col] * H_A, axis=0)
    lane = _lane((tq, LANES))
    lo = lane < HD
    key = _lane((tq, tk))
    misc = misc_ref[0]
    nb = kvb_ref.shape[1]
    zpad = jnp.zeros((LANES - nb, LANES), bf16)
    q8 = jnp.concatenate([q_ref[0, :, g * LANES:(g + 1) * LANES] for g in range(H_A)], axis=0)

    kb = jnp.concatenate([kvb_ref[0, :, 0:LANES].astype(bf16), zpad], axis=0)
    vb = jnp.concatenate([kvb_ref[0, :, LANES:2 * LANES].astype(bf16), zpad], axis=0)
    p = _masked_softmax_rows(_dot_nt(q8, kb), blkend_ref[...] <= t8)
    o_cmp = _dot(p.astype(bf16), vb)
    sel = []
    for kv in range(KV_A):
        imp = p[GQA_A * kv * tq:(GQA_A * kv + 1) * tq]
        for j in range(1, GQA_A):
            imp = imp + p[(GQA_A * kv + j) * tq:(GQA_A * kv + j + 1) * tq]
        sel.append(_select_blocks(imp, (t_col // L_SEL) * 2).astype(bf16))

    def slc_bias(kt):
        causal = (kt * tk + key) <= t_col
        per_kv = [jnp.where((_dot(sel[kv], exp_ref[kt]) > 0.5) & causal, 0.0, NEG) for kv in range(KV_A)]
        return jnp.concatenate([per_kv[g // GQA_A] for g in range(H_A)], axis=0)

    def win_bias(kt):
        d = t_col - (kt * tk + key)
        return jnp.concatenate([jnp.where((d >= 0) & (d < WINDOW), 0.0, NEG)] * H_A, axis=0)

    def flash(kv_ref, kt_lo, kt_hi, bias_fn):
        m_scr[...] = jnp.full(m_scr.shape, NEG, f32)
        l_scr[...] = jnp.zeros(l_scr.shape, f32)
        acc_scr[...] = jnp.zeros(acc_scr.shape, f32)

        def body(kt, carry):
            rows = pl.ds(pl.multiple_of(kt * tk, tk), tk)
            k = kv_ref[0, rows, 0:LANES].astype(bf16)
            v = kv_ref[0, rows, LANES:2 * LANES].astype(bf16)
            s = _dot_nt(q8, k) + bias_fn(kt)
            m, l, acc = _softmax_step2(s, m_scr[...], l_scr[...], acc_scr[...], v)
            m_scr[...] = m
            l_scr[...] = l
            acc_scr[...] = acc
            return carry

        lax.fori_loop(kt_lo, kt_hi, body, 0)
        return _softmax_finish(l_scr[...], acc_scr[...])

    kt_end = ((qi + 1) * tq - 1) // tk + 1
    o_slc = flash(slc_ref, 0, kt_end, slc_bias)
    o_win = flash(win_ref, jnp.maximum(qi * tq - WINDOW, 0) // tk, kt_end, win_bias)

    heads = []
    for g in range(H_A):
        rows = slice(g * tq, (g + 1) * tq)
        o = (misc[:, 3 * g:3 * g + 1] * o_cmp[rows] + misc[:, 3 * g + 1:3 * g + 2] * o_slc[rows]
             + misc[:, 3 * g + 2:3 * g + 3] * o_win[rows])
        heads.append(o if (g % 2) == g // GQA_A else pltpu.roll(o, HD, 1))
    for u in range(H_A // 2):
        o_ref[0, :, u * LANES:(u + 1) * LANES] = jnp.where(lo, heads[2 * u], heads[2 * u + 1])


def _nsa_prompt(qal, kvb, slc, win, misc3, consts, b, t, tq):
    nb = t // L_CMP
    tk = 2 * LANES
    return pl.pallas_call(
        _nsa_prompt_kernel,
        grid=(b, t // tq),
        in_specs=[pl.BlockSpec((1, tq, H_A * LANES), lambda i, j: (i, j, 0)),
                  pl.BlockSpec((1, nb, 2 * LANES), lambda i, j: (i, 0, 0)),
                  pl.BlockSpec((1, t, 2 * LANES), lambda i, j: (i, 0, 0)),
                  pl.BlockSpec((1, t, 2 * LANES), lambda i, j: (i, 0, 0)),
                  pl.BlockSpec((1, tq, LANES), lambda i, j: (i, j, 0)),
                  pl.BlockSpec((t // tk, LANES, tk), lambda i, j: (0, 0, 0)),
                  pl.BlockSpec((1, LANES), lambda i, j: (0, 0))],
        out_specs=pl.BlockSpec((1, tq, H_A * HD), lambda i, j: (i, j, 0)),
        out_shape=jax.ShapeDtypeStruct((b, t, H_A * HD), f32),
        scratch_shapes=[pltpu.VMEM((H_A * tq, LANES), f32), pltpu.VMEM((H_A * tq, LANES), f32),
                        pltpu.VMEM((H_A * tq, LANES), f32)],
        compiler_params=_cparams(("parallel", "parallel")),
        name="nsa_prompt",
    )(qal, kvb, slc, win, misc3, consts["expand"], consts["blkend"])


def _rope_tables(pos):
    half = HD // 2
    inv = ROPE_THETA ** (-jnp.arange(half, dtype=f32) / half)
    ang = pos.astype(f32)[:, None] * inv[None, :]
    cos, sin = jnp.cos(ang), jnp.sin(ang)
    return (jnp.concatenate([cos, cos, cos, cos], axis=-1),
            jnp.concatenate([-sin, sin, -sin, sin], axis=-1))


def _constants(t, tq):
    lane = np.arange(LANES)
    bd = ((lane[:, None] // HD) == (lane[None, :] // HD)).astype(np.float32) / HD
    ltri = (lane[None, :] <= lane[:, None]).astype(np.float32)
    key = np.arange(t).reshape(t // tq, 1, tq)
    expand = ((lane[None, :, None] % 2 == 0) & (key // L_SEL == lane[None, :, None] // 2)).astype(np.float32)
    blkend = np.where(lane < t // L_CMP, (lane + 1) * L_CMP - 1, 2 ** 30).astype(np.int32)[None, :]
    log_g = jnp.log1p(-jnp.exp2(-5.0 - jnp.arange(H_C, dtype=f32)))
    i = jnp.arange(RET_CHUNK, dtype=f32)
    diff = i[:, None] - i[None, :]
    dmask = jnp.where(diff >= 0, jnp.exp(jnp.maximum(diff, 0.0) * log_g[:, None, None]), 0.0)
    xi = jnp.exp((i + 1.0)[None, :] * log_g[:, None])
    zeta = jnp.exp((RET_CHUNK - 1.0 - i)[None, :] * log_g[:, None])
    g_c = jnp.exp(RET_CHUNK * log_g)
    by_lane = lambda a: jnp.repeat(a.reshape(2, 2, -1).transpose(0, 2, 1), HD, axis=-1)
    return {"bd": jnp.asarray(bd, bf16), "pool8": _pool_matrix(8, 2 * PAGE),
            "ltri": jnp.asarray(ltri, bf16),
            "expand": jnp.asarray(expand, bf16), "blkend": jnp.asarray(blkend),
            "ret_dmask": dmask, "ret_xi": by_lane(xi), "ret_zeta": by_lane(zeta),
            "ret_gc": by_lane(g_c[:, None]), "log_g": log_g}


def _pack_w_in(wt):
    off = np.cumsum([0, 512, 128, 128, 128, 128, 128, 128, 24, 512, 256, 256, 256, 4, 256, 256, 256, 256, 256])
    seg = lambda i: wt[off[i]:off[i + 1], :]
    (a_q, a_kc, a_vc, a_ks, a_vs, a_kw, a_vw, a_g, a_gate,
     b_q, b_k, b_v, b_f, b_gate, c_q, c_k, c_v, c_gate) = [seg(i) for i in range(18)]
    pad = jnp.zeros((LANES - MISC_G - H_B, wt.shape[1]), wt.dtype)
    return jnp.concatenate([a_q, a_kc, a_vc, a_ks, a_vs, a_kw, a_vw, a_g, b_f, pad,
                            a_gate, b_gate, c_gate, b_q, b_k, b_v, c_q, c_k, c_v], axis=0).astype(bf16)


def _layer_params(l, consts, norm_w, w_in, fox_bf, nsa_q_norm, nsa_k_norm, fox_q_norm, fox_k_norm,
                  ret_gn_w, w_out):
    two = lambda g: jnp.concatenate([g, g])[None, :]
    gains = jnp.concatenate([two(nsa_q_norm[l]), two(nsa_k_norm[l, 0]), two(nsa_k_norm[l, 1]),
                             two(nsa_k_norm[l, 2]), two(fox_q_norm[l]), two(fox_k_norm[l]),
                             jnp.zeros((2, LANES), f32)], axis=0)
    bf = jnp.zeros((1, LANES), f32).at[0, MISC_G:MISC_G + H_B].set(fox_bf[l])
    lw = dict(consts)
    lw.update(norm_w=norm_w[l][None, :], w_pack=_pack_w_in(jnp.transpose(w_in, (2, 0, 1))[:, l, :]), gains=gains, bf=bf,
              ret_gn_w=ret_gn_w[l][None, :], w_out=w_out[l].astype(bf16))
    return lw


def _prompt_layer(x, lw, consts, cos_t, sin_t):
    b, t, d = x.shape
    tq = 2 * LANES
    x2 = x.reshape(b * t, d)
    qal, cmp_, slc, win, misc, gate, bqal, fkv, cqkv, kvb = _proj(x2, lw, cos_t, sin_t, 256)
    r3 = lambda a: a.reshape(b, t, a.shape[-1])
    misc3 = r3(misc)
    o_a = _nsa_prompt(r3(qal), kvb.reshape(b, t // L_CMP, 2 * LANES), r3(slc), r3(win), misc3,
                      consts, b, t, tq)
    ccol, crow = _fox_prep(misc3, consts["ltri"])
    o_b = _fox_prompt(r3(bqal), r3(fkv), ccol, crow, b, t)
    o_c, r_new = _ret_prompt(r3(cqkv), lw, b, t)
    y = _merge(x2, o_a.reshape(b * t, -1), o_b.reshape(b * t, -1), o_c.reshape(b * t, -1), gate,
               lw["w_out"], 256)
    wk = min(WINDOW, t)
    states = (cmp_.reshape(b, t, 2, KV_A, HD), slc.reshape(b, t, 2, KV_A, HD),
              fkv.reshape(b, t, 2, H_B, HD), misc3[:, :, MISC_G:MISC_G + H_B],
              r3(win)[:, t - wk:].reshape(b, wk, 2, KV_A, HD), r_new)
    return y.reshape(b, t, d), states


def _head_col(vec_row, shape, base, stride):
    pick = _lane(shape) == base + stride * _row(shape)
    return jnp.sum(jnp.where(pick, jnp.broadcast_to(vec_row, shape), 0.0), axis=1, keepdims=True)


def _seg_mean(x, seg):
    lane = _lane(x.shape)
    sh = seg // 2
    while sh >= 1:
        x = x + jnp.where((lane & sh) != 0, pltpu.roll(x, sh, 1), pltpu.roll(x, LANES - sh, 1))
        sh //= 2
    return x * (1.0 / seg)


def _tree(op, xs):
    xs = list(xs)
    while len(xs) > 1:
        xs = [op(xs[i], xs[i + 1]) if i + 1 < len(xs) else xs[i] for i in range(0, len(xs), 2)]
    return xs[0]


SEQ_PER_STEP = 2


def _seq_views(refs, s, lead_axis=0):
    one = pl.ds(s, 1)
    return [r.at[one] if lead_axis == 0 else r.at[:, one] for r in refs]


def _dec_nsa_kernel(pt_ref, q_ref, slc_new_ref, win_new_ref, misc_ref, win_ref, *rest, n_pages):
    o_ref = rest[2 * n_pages * SEQ_PER_STEP]
    for s in range(SEQ_PER_STEP):
        pages = rest[2 * n_pages * s:2 * n_pages * (s + 1)]
        q, slc_new, win_new, misc, o = _seq_views([q_ref, slc_new_ref, win_new_ref, misc_ref, o_ref], s)
        win, = _seq_views([win_ref], s, lead_axis=1)
        _dec_nsa_seq(q, slc_new, win_new, misc, win, pages[0:n_pages], pages[n_pages:], o, n_pages)


def _dec_nsa_seq(q_ref, slc_new_ref, win_new_ref, misc_ref, win_ref, cmp_pages, slc_pages, o_ref, n_pages):
    shape = (H_A, LANES)
    lane = _lane(shape)
    row = _row(shape)
    lo = lane < HD
    q8 = q_ref[0]
    qf = q8.astype(f32)
    past = n_pages * PAGE
    cur2 = 2 * (past // L_SEL)
    bf_round = lambda a: a.astype(bf16).astype(f32)
    k_of = lambda pg: pg[0, 0, 0:LANES, :].astype(bf16)
    v_of = lambda pg: pg[0, 0, LANES:2 * LANES, :].astype(bf16)
    row_max = lambda xs: jnp.max(_tree(jnp.maximum, xs), axis=1, keepdims=True)
    row_sum = lambda xs: jnp.sum(_tree(jnp.add, xs), axis=1, keepdims=True)

    sb = [_seg_mean(_dot(q8, k_of(pg)), L_CMP) for pg in cmp_pages]
    m = row_max(sb)
    p = [jnp.exp(x - m) * (1.0 / L_CMP) for x in sb]
    cl = row_sum(p)
    o_cmp = _tree(jnp.add, [_dot_nt(x.astype(bf16), v_of(pg)) for x, pg in zip(p, cmp_pages)]) / cl
    to_prob = L_CMP / cl
    imp2 = jnp.zeros(shape, f32)
    for page, x in enumerate(p):
        e = x * to_prob
        e = jnp.where(row < GQA_A,
                      jnp.sum(e[0:GQA_A], axis=0, keepdims=True),
                      jnp.sum(e[GQA_A:2 * GQA_A], axis=0, keepdims=True))
        pair = e + pltpu.roll(e, LANES - L_CMP, 1)
        first = pltpu.roll(pair, 4 * page, 1) if page else pair
        imp2 = jnp.where(lane == 4 * page, first, imp2)
        imp2 = jnp.where(lane == 4 * page + 2, pltpu.roll(pair, (4 * page + 2 + HD) % LANES, 1), imp2)
    sel = _select_pairs(imp2, cur2)

    k_new = bf_round(slc_new_ref[0, :, 0:LANES])
    v_new = bf_round(slc_new_ref[0, :, LANES:2 * LANES])
    s_new = jnp.sum(qf * k_new, axis=1, keepdims=True)
    s = []
    for page, pg in enumerate(slc_pages):
        hit = jnp.where(lo, sel[:, 4 * page:4 * page + 1], sel[:, 4 * page + 2:4 * page + 3]) > 0.5
        s.append(jnp.where(hit, _dot(q8, k_of(pg)), NEG))
    m = jnp.maximum(row_max(s), s_new)
    p = [jnp.exp(x - m) for x in s]
    p_new = jnp.exp(s_new - m)
    acc = _tree(jnp.add, [_dot_nt(x.astype(bf16), v_of(pg)) for x, pg in zip(p, slc_pages)])
    o_slc = (acc + p_new * v_new) / (row_sum(p) + p_new)

    wk = win_ref.shape[3]
    kw = win_ref[0, 0, 0:LANES, :].astype(bf16)
    vw = win_ref[0, 0, LANES:2 * LANES, :].astype(bf16)
    kw_new = bf_round(win_new_ref[0, :, 0:LANES])
    vw_new = bf_round(win_new_ref[0, :, LANES:2 * LANES])
    s = jnp.where(_lane((H_A, wk)) > wk - WINDOW, _dot(q8, kw), NEG)
    s_new = jnp.sum(qf * kw_new, axis=1, keepdims=True)
    m = jnp.maximum(jnp.max(s, axis=1, keepdims=True), s_new)
    pw = jnp.exp(s - m)
    pw_new = jnp.exp(s_new - m)
    o_win = ((_dot_nt(pw.astype(bf16), vw) + pw_new * vw_new)
             / (jnp.sum(pw, axis=1, keepdims=True) + pw_new))

    misc = misc_ref[0]
    o8 = (_head_col(misc, shape, 0, 3) * o_cmp + _head_col(misc, shape, 1, 3) * o_slc
          + _head_col(misc, shape, 2, 3) * o_win)
    o8r = pltpu.roll(o8, HD, 1)
    lo1 = _lane((1, LANES)) < HD
    for u in range(H_A // 2):
        kv_lo = (2 * u) // GQA_A == 0
        even = (o8 if kv_lo else o8r)[2 * u:2 * u + 1, :]
        odd = (o8r if kv_lo else o8)[2 * u + 1:2 * u + 2, :]
        o_ref[0, :, u * LANES:(u + 1) * LANES] = jnp.where(lo1, even, odd)


def _page_specs(block, layer, count, s):
    def spec(g):
        return pl.BlockSpec(block, lambda b, pt: (layer, pt[(b * SEQ_PER_STEP + s) * count + g])
                            + (0,) * (len(block) - 2))
    return [spec(g) for g in range(count)]


def _dec_nsa(pt_flat, q8, slc_new, win_new, misc, win_state, cache_cmp, cache_slc, layer):
    n_seq = q8.shape[0]
    n_pages = pt_flat.shape[0] // n_seq
    wk = win_state.shape[3]
    sub = SEQ_PER_STEP
    seq3 = lambda w: pl.BlockSpec((sub, 1, w), lambda b, pt: (b, 0, 0))
    page_block = (1, 1, 2 * LANES, PAGE)
    page_specs, page_args = [], []
    for s in range(sub):
        page_specs += _page_specs(page_block, layer, n_pages, s) + _page_specs(page_block, layer, n_pages, s)
        page_args += [cache_cmp] * n_pages + [cache_slc] * n_pages
    grid_spec = pltpu.PrefetchScalarGridSpec(
        num_scalar_prefetch=1,
        grid=(n_seq // sub,),
        in_specs=[pl.BlockSpec((sub, H_A, LANES), lambda b, pt: (b, 0, 0)),
                  seq3(2 * LANES), seq3(2 * LANES), seq3(LANES),
                  pl.BlockSpec((1, sub, 2 * LANES, wk), lambda b, pt: (layer, b, 0, 0))] + page_specs,
        out_specs=pl.BlockSpec((sub, 1, H_A * HD), lambda b, pt: (b, 0, 0)))
    return pl.pallas_call(
        functools.partial(_dec_nsa_kernel, n_pages=n_pages),
        grid_spec=grid_spec,
        out_shape=jax.ShapeDtypeStruct((n_seq, 1, H_A * HD), f32),
        compiler_params=_cparams(("parallel",)),
        name="dec_nsa",
    )(pt_flat, q8, slc_new, win_new, misc, win_state, *page_args)


def _lane_cumsum(x):
    lane = _lane(x.shape)
    sh = 1
    while sh < LANES:
        x = x + jnp.where(lane >= sh, pltpu.roll(x, sh, 1), 0.0)
        sh *= 2
    return x


def _dec_fox_kernel(pt_ref, q_ref, kv_new_ref, misc_ref, *rest, n_pages):
    o_ref = rest[2 * n_pages * SEQ_PER_STEP]
    for s in range(SEQ_PER_STEP):
        pages = rest[2 * n_pages * s:2 * n_pages * (s + 1)]
        q, kv_new, misc, o = _seq_views([q_ref, kv_new_ref, misc_ref, o_ref], s)
        _dec_fox_seq(q, kv_new, misc, pages[0:n_pages], pages[n_pages:], o)


def _dec_fox_seq(q_ref, kv_new_ref, misc_ref, kv_pages, lf_pages, o_ref):
    shape = (8, LANES)
    first_pair = _row(shape) < 2
    q8 = q_ref[0]
    unit = lambda kv, u: kv[0, 0, u * LANES:(u + 1) * LANES, :].astype(bf16)

    cs = [_lane_cumsum(lf[0, 0]) for lf in lf_pages]
    carry = jnp.zeros((8, 1), f32)
    s = []
    for kv, c in zip(kv_pages, cs):
        s.append(jnp.where(first_pair, _dot(q8, unit(kv, 0)), _dot(q8, unit(kv, 1))) - (carry + c))
        carry = carry + c[:, LANES - 1:LANES]
    bf_round = lambda a: jnp.broadcast_to(a.astype(bf16).astype(f32), shape)
    new = kv_new_ref[0]
    k_new = jnp.where(first_pair, bf_round(new[:, 0:LANES]), bf_round(new[:, LANES:2 * LANES]))
    v_new = jnp.where(first_pair, bf_round(new[:, 2 * LANES:3 * LANES]), bf_round(new[:, 3 * LANES:4 * LANES]))
    s_new = (jnp.sum(q8.astype(f32) * k_new, axis=1, keepdims=True)
             - (carry + _head_col(misc_ref[0], shape, MISC_G, 1)))
    m = jnp.maximum(jnp.max(_tree(jnp.maximum, s), axis=1, keepdims=True), s_new)
    p = [jnp.exp(x - m) for x in s]
    p_new = jnp.exp(s_new - m)
    den = jnp.sum(_tree(jnp.add, p), axis=1, keepdims=True) + p_new
    pb = [x.astype(bf16) for x in p]
    acc0 = _tree(jnp.add, [_dot_nt(x, unit(kv, 2)) for x, kv in zip(pb, kv_pages)])
    acc1 = _tree(jnp.add, [_dot_nt(x, unit(kv, 3)) for x, kv in zip(pb, kv_pages)])
    o8 = (jnp.where(first_pair, acc0, acc1) + p_new * v_new) / den
    lo1 = _lane((1, LANES)) < HD
    for u in range(H_B // 2):
        o_ref[0, :, u * LANES:(u + 1) * LANES] = jnp.where(lo1, o8[2 * u:2 * u + 1, :],
                                                           o8[2 * u + 1:2 * u + 2, :])


def _dec_fox(pt_flat, qb4, fkv_new, misc, cache_fkv, cache_lft, layer):
    n_seq = qb4.shape[0]
    n_pages = pt_flat.shape[0] // n_seq
    sub = SEQ_PER_STEP
    seq3 = lambda w: pl.BlockSpec((sub, 1, w), lambda b, pt: (b, 0, 0))
    page_specs, page_args = [], []
    for s in range(sub):
        page_specs += (_page_specs((1, 1, 4 * LANES, PAGE), layer, n_pages, s)
                       + _page_specs((1, 1, 8, PAGE), layer, n_pages, s))
        page_args += [cache_fkv] * n_pages + [cache_lft] * n_pages
    grid_spec = pltpu.PrefetchScalarGridSpec(
        num_scalar_prefetch=1,
        grid=(n_seq // sub,),
        in_specs=[pl.BlockSpec((sub, 8, LANES), lambda b, pt: (b, 0, 0)), seq3(4 * LANES), seq3(LANES)]
        + page_specs,
        out_specs=pl.BlockSpec((sub, 1, H_B * HD), lambda b, pt: (b, 0, 0)))
    return pl.pallas_call(
        functools.partial(_dec_fox_kernel, n_pages=n_pages),
        grid_spec=grid_spec,
        out_shape=jax.ShapeDtypeStruct((n_seq, 1, H_B * HD), f32),
        compiler_params=_cparams(("parallel",)),
        name="dec_fox",
    )(pt_flat, qb4, fkv_new, misc, *page_args)


def _dec_ret_kernel(q_ref, k_ref, v_ref, r_ref, gam_ref, gnw_ref, o_ref, rn_ref):
    q = q_ref[0].astype(f32)
    k = k_ref[0].astype(f32)
    v = v_ref[0].astype(f32)
    qk = jnp.sum(q * k, axis=1, keepdims=True)
    eye = _row((HD, HD)) == _lane((HD, HD))
    col = lambda a: jnp.sum(jnp.where(eye, jnp.broadcast_to(a, (HD, HD)), 0.0), axis=1, keepdims=True)
    for h in range(H_C):
        hs = slice(h, h + 1)
        r = r_ref[0, 0, h]
        gam = gam_ref[hs, :]
        o = qk[hs, :] * v[hs, :] + jnp.sum(col(q[hs, :]) * r, axis=0, keepdims=True) * gam
        rn_ref[0, h] = r * gam + col(k[hs, :]) * v[hs, :]
        d = o - jnp.mean(o, axis=1, keepdims=True)
        var = jnp.mean(d * d, axis=1, keepdims=True)
        o_ref[0, hs, :] = d * lax.rsqrt(var + EPS) * gnw_ref[hs, :]


def _dec_ret(cq, ck, cv, state_ret, gam, gnw, layer):
    n_seq = cq.shape[0]
    head = pl.BlockSpec((1, H_C, HD), lambda b: (b, 0, 0))
    table = pl.BlockSpec((H_C, HD), lambda b: (0, 0))
    return pl.pallas_call(
        _dec_ret_kernel,
        grid=(n_seq,),
        in_specs=[head, head, head,
                  pl.BlockSpec((1, 1, H_C, HD, HD), lambda b: (layer, b, 0, 0, 0)), table, table],
        out_specs=[head, pl.BlockSpec((1, H_C, HD, HD), lambda b: (b, 0, 0, 0))],
        out_shape=[jax.ShapeDtypeStruct((n_seq, H_C, HD), f32),
                   jax.ShapeDtypeStruct((n_seq, H_C, HD, HD), f32)],
        compiler_params=_cparams(("parallel",)),
        name="dec_ret",
    )(cq, ck, cv, state_ret, gam, gnw)


def _sample_layer(x, layer, lw, consts, caches, pt_flat, cos_t, sin_t):
    n, _, d = x.shape
    x2 = x.reshape(n, d)
    qal, cmp_, slc, win, misc, gate, bqal, fkv, cqkv, _ = _proj(x2, lw, cos_t, sin_t, n)
    row3 = lambda a: a[:, None, :]
    o_a8 = _dec_nsa(pt_flat, qal.reshape(n, H_A, LANES), row3(slc), row3(win), row3(misc),
                    caches["win"], caches["cmp"], caches["slc"], layer)
    qb8 = jnp.pad(bqal.reshape(n, H_B, LANES), ((0, 0), (0, 8 - H_B), (0, 0)))
    o_b8 = _dec_fox(pt_flat, qb8, row3(fkv), row3(misc),
                    caches["fkv"], caches["lft"], layer)
    heads = lambda u: cqkv[:, 2 * u * LANES:(2 * u + 2) * LANES].reshape(n, H_C, HD).astype(f32)
    gam = jnp.broadcast_to(jnp.exp(consts["log_g"])[:, None], (H_C, HD))
    o_c, r_new = _dec_ret(heads(0), heads(1), heads(2), caches["ret"], gam,
                          lw["ret_gn_w"].reshape(H_C, HD), layer)
    y = _merge(x2, o_a8.reshape(n, H_A * HD), o_b8.reshape(n, H_B * HD), o_c.reshape(n, H_C * HD), gate,
               lw["w_out"], n)
    new_win = jnp.concatenate([caches["win"][layer][:, :, 1:], win[:, :, None]], axis=2)
    wk = new_win.shape[2]
    new_win = jnp.transpose(new_win.reshape(n, 2, KV_A, HD, wk), (0, 4, 1, 2, 3))
    states = (cmp_.reshape(n, 1, 2, KV_A, HD), slc.reshape(n, 1, 2, KV_A, HD),
              fkv.reshape(n, 1, 2, H_B, HD), misc[:, None, MISC_G:MISC_G + H_B], new_win, r_new)
    return y.reshape(n, 1, d), states


def kernel(x_prompt, x_sample, cache_nsa_cmp, cache_nsa_slc, cache_fox_kv, cache_fox_logf,
           state_nsa_win, state_ret, page_table, norm_w, w_in, fox_bf, nsa_q_norm, nsa_k_norm,
           fox_q_norm, fox_k_norm, ret_gn_w, w_out):
    depth = norm_w.shape[0]
    b, t, _ = x_prompt.shape
    n_seq, n_pages = page_table.shape
    assert x_sample.shape[1] == 1 and t % 256 == 0 and n_pages % 2 == 0
    past = n_pages * PAGE
    consts = _constants(t, 2 * LANES)
    cos_p, sin_p = _rope_tables(jnp.arange(t, dtype=jnp.int32))
    cos_s, sin_s = _rope_tables(jnp.full((n_seq,), past, dtype=jnp.int32))
    n_phys = cache_nsa_cmp.shape[1]
    feat_major = lambda a, lead: jnp.transpose(a, (0, 1, 3, 4, 5, 2)).reshape(depth, lead, -1, a.shape[2])
    caches = {"cmp": feat_major(cache_nsa_cmp, n_phys), "slc": feat_major(cache_nsa_slc, n_phys),
              "fkv": feat_major(cache_fox_kv, n_phys),
              "lft": jnp.pad(jnp.swapaxes(cache_fox_logf, 2, 3), ((0, 0), (0, 0), (0, 8 - H_B), (0, 0))),
              "win": feat_major(state_nsa_win, n_seq),
              "ret": state_ret}
    pt_flat = page_table.reshape(-1)
    yp, ys = x_prompt, x_sample
    p_states, s_states = [], []
    for l in range(depth):
        lw = _layer_params(l, consts, norm_w, w_in, fox_bf, nsa_q_norm, nsa_k_norm, fox_q_norm,
                           fox_k_norm, ret_gn_w, w_out)
        yp, ps = _prompt_layer(yp, lw, consts, cos_p, sin_p)
        ys, ss = _sample_layer(ys, l, lw, consts, caches, pt_flat, cos_s, sin_s)
        p_states.append(ps)
        s_states.append(ss)
    stk = lambda states, i: jnp.stack([st[i] for st in states], axis=0)
    out = [yp, ys]
    for i in range(6):
        out += [stk(p_states, i), stk(s_states, i)]
    return tuple(out)
```

```python
import functools

import numpy as np
import jax
import jax.numpy as jnp
from jax import lax
from jax.experimental import pallas as pl
from jax.experimental.pallas import tpu as pltpu

f32 = jnp.float32
bf16 = jnp.bfloat16

HD = 64
H_A, KV_A, GQA_A = 8, 2, 4
H_B, H_C = 4, 4
L_CMP, L_SEL, TOP_N, WINDOW = 32, 64, 8, 512
PAGE = 128
RET_CHUNK = 128
ROPE_THETA = 10000.0
EPS = 1e-6
SCALE = HD ** -0.5
FORCE_BONUS = 1e3
NEG = -1e30

LANES = 128
VMEM_LIMIT = 48 * 1024 * 1024

C_AQ, C_CMP, C_SLC, C_WIN, C_MISC, C_GATE, C_BQ, C_BKV, C_CQKV, D_PACK = (
    0, 512, 768, 1024, 1280, 1408, 2432, 2688, 3200, 3968)
MISC_G = 3 * H_A


def _cparams(sem):
    return pltpu.CompilerParams(dimension_semantics=sem, vmem_limit_bytes=VMEM_LIMIT)


def _dot(a, b):
    return jnp.dot(a, b, preferred_element_type=f32)


def _dot_nt(a, b):
    return lax.dot_general(a, b, (((1,), (1,)), ((), ())), preferred_element_type=f32)


def _split(x, n):
    parts = []
    r = x
    for i in range(n):
        h = r.astype(bf16)
        parts.append(h)
        if i + 1 < n:
            r = r - h.astype(f32)
    return parts


def _dot_split(x, w, n=2):
    acc = None
    for h in _split(x, n):
        d = _dot(h, w)
        acc = d if acc is None else acc + d
    return acc


def _dot_split_l(w, x, n=2):
    acc = None
    for h in _split(x, n):
        d = _dot(w, h)
        acc = d if acc is None else acc + d
    return acc


def _lane(shape):
    return lax.broadcasted_iota(jnp.int32, shape, len(shape) - 1)


def _row(shape):
    return lax.broadcasted_iota(jnp.int32, shape, len(shape) - 2)


def _swap_half_heads(y):
    lane = _lane(y.shape)
    return jnp.where((lane & 32) == 0, pltpu.roll(y, 96, 1), pltpu.roll(y, 32, 1))


def _head_rms(z, bd, g):
    ms = _dot_split(z * z, bd)
    return z * lax.rsqrt(ms + EPS) * g


def _rope(y, cos, sin):
    return y * cos + _swap_half_heads(y) * sin


def _sigmoid(z):
    return 1.0 / (1.0 + jnp.exp(-z))


def _proj_kernel(x_ref, nw_ref, w_ref, bd_ref, pool_ref, cos_ref, sin_ref, gn_ref, bf_ref,
                 qal_ref, cmp_ref, slc_ref, win_ref, misc_ref, gate_ref, bqal_ref, fkv_ref,
                 cqkv_ref, kvb_ref):
    x = x_ref[...]
    ms = jnp.mean(x * x, axis=-1, keepdims=True)
    xn = (x * lax.rsqrt(ms + EPS) * nw_ref[...]).astype(bf16)
    bd = bd_ref[...]
    cos = cos_ref[...]
    sin = sin_ref[...]
    lane = _lane((x.shape[0], LANES))
    lo = lane < HD

    def z_of(c0, width):
        return _dot_nt(xn, w_ref[c0:c0 + width, :])

    def unit(z, u):
        return z[:, u * LANES:(u + 1) * LANES]

    z = z_of(C_AQ, H_A * HD)
    g_aq = gn_ref[0:1, :]
    for u in range(H_A // 2):
        y = _rope(_head_rms(unit(z, u), bd, g_aq), cos, sin) * SCALE
        yr = pltpu.roll(y, HD, 1)
        kv_lo = (2 * u) // GQA_A == 0
        keep = lo if kv_lo else jnp.logical_not(lo)
        even, odd = (y, yr) if kv_lo else (yr, y)
        qal_ref[:, (2 * u) * LANES:(2 * u + 1) * LANES] = jnp.where(keep, even, 0.0).astype(bf16)
        qal_ref[:, (2 * u + 1) * LANES:(2 * u + 2) * LANES] = jnp.where(keep, odd, 0.0).astype(bf16)

    for i, (c0, o_ref) in enumerate(((C_CMP, cmp_ref), (C_SLC, slc_ref), (C_WIN, win_ref))):
        z = z_of(c0, 2 * LANES)
        k = _rope(_head_rms(unit(z, 0), bd, gn_ref[1 + i:2 + i, :]), cos, sin)
        v = unit(z, 1)
        o_ref[:, 0:LANES] = k
        o_ref[:, LANES:2 * LANES] = v
        if i == 0:
            pool = pool_ref[...]
            kvb_ref[:, 0:LANES] = _dot_split_l(pool, k)
            kvb_ref[:, LANES:2 * LANES] = _dot_split_l(pool, v)

    z = z_of(C_MISC, LANES)
    zf = z + bf_ref[...]
    logf = jnp.minimum(zf, 0.0) - jnp.log1p(jnp.exp(-jnp.abs(zf)))
    misc_ref[...] = jnp.where(lane < MISC_G, _sigmoid(z), jnp.where(lane < MISC_G + H_B, logf, 0.0))

    for c in range(0, 1024, 512):
        z = z_of(C_GATE + c, 512)
        gate_ref[:, c:c + 512] = z * _sigmoid(z)

    z = z_of(C_BQ, H_B * HD)
    for u in range(H_B // 2):
        y = _head_rms(unit(z, u), bd, gn_ref[4:5, :]) * SCALE
        bqal_ref[:, (2 * u) * LANES:(2 * u + 1) * LANES] = jnp.where(lo, y, 0.0).astype(bf16)
        bqal_ref[:, (2 * u + 1) * LANES:(2 * u + 2) * LANES] = jnp.where(lo, 0.0, y).astype(bf16)
    z = z_of(C_BKV, 2 * H_B * HD)
    for u in range(2):
        fkv_ref[:, u * LANES:(u + 1) * LANES] = _head_rms(unit(z, u), bd, gn_ref[5:6, :])
    fkv_ref[:, 2 * LANES:4 * LANES] = z[:, 2 * LANES:4 * LANES]

    z = z_of(C_CQKV, 3 * H_C * HD)
    for u in range(2):
        cqkv_ref[:, u * LANES:(u + 1) * LANES] = _rope(unit(z, u), cos, sin).astype(bf16)
        cqkv_ref[:, (2 + u) * LANES:(3 + u) * LANES] = (_rope(unit(z, 2 + u), cos, sin) * SCALE).astype(bf16)
    cqkv_ref[:, 4 * LANES:6 * LANES] = z[:, 4 * LANES:6 * LANES].astype(bf16)


def _pool_matrix(n_out, n_rows):
    r = np.arange(n_rows)[None, :] // L_CMP == np.arange(n_out)[:, None]
    return jnp.asarray(r.astype(np.float32) / L_CMP, bf16)


def _proj(x2, lw, cos_t, sin_t, tm):
    n = x2.shape[0]
    nt = cos_t.shape[0] // tm
    const = lambda i: (0, 0)
    rows = lambda i: (i, 0)
    outs = [(H_A * LANES, bf16), (2 * LANES, f32), (2 * LANES, f32), (2 * LANES, f32), (LANES, f32),
            (1024, f32), (H_B * LANES, bf16), (4 * LANES, f32), (6 * LANES, bf16)]
    out_shape = [jax.ShapeDtypeStruct((n, w), dt) for w, dt in outs]
    out_specs = [pl.BlockSpec((tm, w), rows) for w, _ in outs]
    out_shape.append(jax.ShapeDtypeStruct((n // L_CMP, 2 * LANES), f32))
    out_specs.append(pl.BlockSpec((tm // L_CMP, 2 * LANES), rows))
    return pl.pallas_call(
        _proj_kernel,
        grid=(n // tm,),
        in_specs=[pl.BlockSpec((tm, 1024), rows),
                  pl.BlockSpec((1, 1024), const),
                  pl.BlockSpec((D_PACK, 1024), const),
                  pl.BlockSpec((LANES, LANES), const),
                  pl.BlockSpec((tm // L_CMP, tm), const),
                  pl.BlockSpec((tm, LANES), lambda i: (i % nt, 0)),
                  pl.BlockSpec((tm, LANES), lambda i: (i % nt, 0)),
                  pl.BlockSpec((8, LANES), const),
                  pl.BlockSpec((1, LANES), const)],
        out_specs=out_specs,
        out_shape=out_shape,
        compiler_params=_cparams(("parallel",)),
        name="proj",
    )(x2, lw["norm_w"], lw["w_pack"], lw["bd"], _pool_matrix(tm // L_CMP, tm), cos_t, sin_t,
      lw["gains"], lw["bf"])


def _merge_kernel(x_ref, oa_ref, ob_ref, oc_ref, gate_ref, w_ref, y_ref):
    g = gate_ref[...]
    wa, wb = H_A * HD, H_A * HD + H_B * HD
    acc = _dot((oa_ref[...] * g[:, 0:wa]).astype(bf16), w_ref[0:wa, :])
    acc += _dot((ob_ref[...] * g[:, wa:wb]).astype(bf16), w_ref[wa:wb, :])
    acc += _dot((oc_ref[...] * g[:, wb:]).astype(bf16), w_ref[wb:, :])
    y_ref[...] = x_ref[...] + acc


def _merge(x2, oa, ob, oc, gate, w_out, tm):
    n = x2.shape[0]
    rows = lambda i: (i, 0)
    return pl.pallas_call(
        _merge_kernel,
        grid=(n // tm,),
        in_specs=[pl.BlockSpec((tm, 1024), rows), pl.BlockSpec((tm, 512), rows),
                  pl.BlockSpec((tm, 256), rows), pl.BlockSpec((tm, 256), rows),
                  pl.BlockSpec((tm, 1024), rows), pl.BlockSpec((1024, 1024), lambda i: (0, 0))],
        out_specs=pl.BlockSpec((tm, 1024), rows),
        out_shape=jax.ShapeDtypeStruct((n, 1024), f32),
        compiler_params=_cparams(("parallel",)),
        name="merge",
    )(x2, oa, ob, oc, gate, w_out)


def _ret_prompt_kernel(q_ref, k_ref, v_ref, dm_ref, xi_ref, zeta_ref, gc_ref, gnw_ref, bd_ref,
                       o_ref, r_ref):
    c = RET_CHUNK
    n_chunks = q_ref.shape[1] // c
    pairs = range(H_C // 2)
    lo = _lane((c, LANES)) < HD
    same_head = (_row((LANES, LANES)) < HD) == (_lane((LANES, LANES)) < HD)
    bd = bd_ref[...]

    def step(i, rs):
        rows = pl.ds(pl.multiple_of(i * c, c), c)
        out = []
        for p, r in zip(pairs, rs):
            lanes = slice(p * LANES, (p + 1) * LANES)
            q = q_ref[0, rows, lanes]
            k = k_ref[0, rows, lanes]
            v = v_ref[0, rows, lanes]
            zero = jnp.zeros_like(q)
            a0 = _dot_nt(jnp.where(lo, q, zero), k) * dm_ref[2 * p]
            a1 = _dot_nt(jnp.where(lo, zero, q), k) * dm_ref[2 * p + 1]
            o = jnp.where(lo, _dot(a0.astype(bf16), v), _dot(a1.astype(bf16), v))
            o = o + _dot(q, r.astype(bf16)) * xi_ref[p]
            kz_t = (k.astype(f32) * zeta_ref[p]).T.astype(bf16)
            out.append(r * gc_ref[p] + jnp.where(same_head, _dot(kz_t, v), 0.0))
            mu = _dot_split(o, bd)
            d = o - mu
            var = _dot_split(d * d, bd)
            o_ref[0, rows, lanes] = d * lax.rsqrt(var + EPS) * gnw_ref[:, lanes]
        return tuple(out)

    rs = lax.fori_loop(0, n_chunks, step, tuple(jnp.zeros((LANES, LANES), f32) for _ in pairs))
    for p, r in zip(pairs, rs):
        r_ref[0, 2 * p] = r[0:HD, 0:HD]
        r_ref[0, 2 * p + 1] = pltpu.roll(r, HD, 1)[HD:2 * HD, 0:HD]


def _ret_prompt(cqkv, lw, b, t):
    width = H_C * HD
    blk = lambda u: pl.BlockSpec((1, t, width), lambda i: (i, 0, u))
    whole = lambda a: pl.BlockSpec(a.shape, lambda i: (0,) * a.ndim)
    tables = [lw["ret_dmask"], lw["ret_xi"], lw["ret_zeta"], lw["ret_gc"], lw["ret_gn_w"], lw["bd"]]
    return pl.pallas_call(
        _ret_prompt_kernel,
        grid=(b,),
        in_specs=[blk(0), blk(1), blk(2)] + [whole(a) for a in tables],
        out_specs=[pl.BlockSpec((1, t, width), lambda i: (i, 0, 0)),
                   pl.BlockSpec((1, H_C, HD, HD), lambda i: (i, 0, 0, 0))],
        out_shape=[jax.ShapeDtypeStruct((b, t, width), f32),
                   jax.ShapeDtypeStruct((b, H_C, HD, HD), f32)],
        compiler_params=_cparams(("parallel",)),
        name="ret_prompt",
    )(cqkv, cqkv, cqkv, *tables)


def _fox_prep_kernel(misc_ref, ltri_ref, ccol_ref, crow_ref):
    c = LANES
    n_chunks = misc_ref.shape[1] // c
    ltri = ltri_ref[...]

    def step(i, carry):
        rows = pl.ds(pl.multiple_of(i * c, c), c)
        cs = _dot_split_l(ltri, misc_ref[0, rows, :], 3) + carry
        for p in range(2):
            sh = pltpu.roll(cs, LANES - (MISC_G + 2 * p), 1)
            ccol_ref[0, p, rows, :] = sh
            crow_ref[0, p, i] = sh.T[0:8, :]
        return cs[c - 1:c, :]

    lax.fori_loop(0, n_chunks, step, jnp.zeros((1, LANES), f32))


def _fox_prep(misc3, ltri):
    b, t, _ = misc3.shape
    return pl.pallas_call(
        _fox_prep_kernel,
        grid=(b,),
        in_specs=[pl.BlockSpec((1, t, LANES), lambda i: (i, 0, 0)),
                  pl.BlockSpec((LANES, LANES), lambda i: (0, 0))],
        out_specs=[pl.BlockSpec((1, 2, t, LANES), lambda i: (i, 0, 0, 0)),
                   pl.BlockSpec((1, 2, t // LANES, 8, LANES), lambda i: (i, 0, 0, 0, 0))],
        out_shape=[jax.ShapeDtypeStruct((b, 2, t, LANES), f32),
                   jax.ShapeDtypeStruct((b, 2, t // LANES, 8, LANES), f32)],
        compiler_params=_cparams(("parallel",)),
        name="fox_prep",
    )(misc3, ltri)


def _softmax_step2(s, m, l, acc, v):
    s_a, s_b = s[:, 0:LANES], s[:, LANES:2 * LANES]
    m_new = jnp.maximum(m, jnp.max(jnp.maximum(s_a, s_b), axis=1, keepdims=True))
    alpha = jnp.exp(m - m_new)
    p_a = jnp.exp(s_a - m_new)
    p_b = jnp.exp(s_b - m_new)
    l = alpha * l + (p_a + p_b)
    pv = _dot(p_a.astype(bf16), v[0:LANES]) + _dot(p_b.astype(bf16), v[LANES:2 * LANES])
    return m_new, l, alpha * acc + pv


def _softmax_finish(l, acc):
    return acc / jnp.sum(l, axis=1, keepdims=True)


def _fox_prompt_kernel(q_ref, k_ref, v_ref, ccol_ref, crow_ref, o_ref, m_scr, l_scr, acc_scr):
    tq = q_ref.shape[1]
    qi = pl.program_id(1)
    pairs = range(H_B // 2)
    q2, cq = [], []
    for p in pairs:
        q2.append(jnp.concatenate([q_ref[0, :, (2 * p) * LANES:(2 * p + 1) * LANES],
                                   q_ref[0, :, (2 * p + 1) * LANES:(2 * p + 2) * LANES]], axis=0))
        cc = ccol_ref[0, p]
        cq.append(jnp.broadcast_to(jnp.concatenate([cc[:, 0:1], cc[:, 1:2]], axis=0), (2 * tq, tq)))
    m_scr[...] = jnp.full(m_scr.shape, NEG, f32)
    l_scr[...] = jnp.zeros(l_scr.shape, f32)
    acc_scr[...] = jnp.zeros(acc_scr.shape, f32)

    def tile(kt, masked):
        rows = pl.ds(pl.multiple_of(kt * tq, tq), tq)
        for p in pairs:
            k = k_ref[0, rows, p * LANES:(p + 1) * LANES].astype(bf16)
            v = v_ref[0, rows, p * LANES:(p + 1) * LANES].astype(bf16)
            ck = jnp.concatenate([crow_ref[0, p, 2 * kt], crow_ref[0, p, 2 * kt + 1]], axis=1)
            ck2 = jnp.concatenate([jnp.broadcast_to(ck[0:1, :], (tq, tq)),
                                   jnp.broadcast_to(ck[1:2, :], (tq, tq))], axis=0)
            s = _dot_nt(q2[p], k) + (cq[p] - ck2)
            if masked:
                causal = _lane((tq, tq)) <= _row((tq, tq))
                s = jnp.where(jnp.concatenate([causal, causal], axis=0), s, NEG)
            m, l, acc = _softmax_step2(s, m_scr[p], l_scr[p], acc_scr[p], v)
            m_scr[p] = m
            l_scr[p] = l
            acc_scr[p] = acc

    def body(kt, carry):
        tile(kt, False)
        return carry

    lax.fori_loop(0, qi, body, 0)
    tile(qi, True)
    lo = _lane((tq, LANES)) < HD
    for p in pairs:
        o = _softmax_finish(l_scr[p], acc_scr[p])
        o_ref[0, :, p * LANES:(p + 1) * LANES] = jnp.where(lo, o[0:tq], o[tq:2 * tq])


def _fox_prompt(bqal, fkv, ccol, crow, b, t):
    tq = 2 * LANES
    scratch = pltpu.VMEM((H_B // 2, 2 * tq, LANES), f32)
    return pl.pallas_call(
        _fox_prompt_kernel,
        grid=(b, t // tq),
        in_specs=[pl.BlockSpec((1, tq, H_B * LANES), lambda i, j: (i, j, 0)),
                  pl.BlockSpec((1, t, 2 * LANES), lambda i, j: (i, 0, 0)),
                  pl.BlockSpec((1, t, 2 * LANES), lambda i, j: (i, 0, 1)),
                  pl.BlockSpec((1, 2, tq, LANES), lambda i, j: (i, 0, j, 0)),
                  pl.BlockSpec((1, 2, t // LANES, 8, LANES), lambda i, j: (i, 0, 0, 0, 0))],
        out_specs=pl.BlockSpec((1, tq, H_B * HD), lambda i, j: (i, j, 0)),
        out_shape=jax.ShapeDtypeStruct((b, t, H_B * HD), f32),
        scratch_shapes=[scratch, scratch, scratch],
        compiler_params=_cparams(("parallel", "parallel")),
        name="fox_prompt",
    )(bqal, fkv, fkv, ccol, crow)


def _top_n_mask(score, n):
    lane = _lane(score.shape).astype(f32)
    sel = jnp.zeros(score.shape, f32)
    for _ in range(n):
        m = jnp.broadcast_to(jnp.max(score, axis=1, keepdims=True), score.shape)
        first = jnp.min(jnp.where(score == m, lane, float(LANES)), axis=1, keepdims=True)
        pick = lane == jnp.broadcast_to(first, score.shape)
        sel = jnp.where(pick, 1.0, sel)
        score = jnp.where(pick, -2.0, score)
    return sel


def _masked_softmax_rows(s, valid):
    s = jnp.where(valid, s, NEG)
    m = jnp.broadcast_to(jnp.max(s, axis=1, keepdims=True), s.shape)
    m = jnp.where(m > 0.5 * NEG, m, 0.0)
    p = jnp.where(valid, jnp.exp(s - m), 0.0)
    den = jnp.broadcast_to(jnp.sum(p, axis=1, keepdims=True), s.shape)
    return p / jnp.where(den > 0, den, 1.0)


def _select_blocks(imp, cur2):
    return _select_pairs(imp + pltpu.roll(imp, LANES - 1, 1), cur2)


def _select_pairs(imp2, cur2, top_fn=_top_n_mask):
    lane = _lane(imp2.shape)
    is_blk = (lane & 1) == 0
    valid = is_blk & (lane <= cur2)
    forced = (lane == 0) | (lane == cur2) | (lane == cur2 - 2)
    score = jnp.where(valid, imp2 + jnp.where(forced, FORCE_BONUS, 0.0), -1.0)
    return top_fn(score, TOP_N)


def _top_n_rank(score, n):
    sq = (LANES, LANES)
    i, j = _row(sq), _lane(sq)

    def one(row):
        r = jnp.broadcast_to(row, sq)
        c = jnp.sum(jnp.where(i == j, r, 0.0), axis=1, keepdims=True)
        beats = (c > r) | ((c == r) & (i < j))
        rank = jnp.sum(jnp.where(beats, 1.0, 0.0), axis=0, keepdims=True)
        return jnp.where(rank < n, 1.0, 0.0)

    first = _row(score.shape) < GQA_A
    return jnp.where(first, one(score[0:1]), one(score[GQA_A:GQA_A + 1]))


def _nsa_prompt_kernel(q_ref, kvb_ref, slc_ref, win_ref, misc_ref, exp_ref, blkend_ref,
                       o_ref, m_scr, l_scr, acc_scr):
    tq = q_ref.shape[1]
    tk = 2 * LANES
    qi = pl.program_id(1)
    t_col = qi * tq + _row((tq, 1))
    t8 = jnp.concatenate([t_col] * H_A, axis=0)
    lane = _lane((tq, LANES))
    lo = lane < HD
    key = _lane((tq, tk))
    misc = misc_ref[0]
    nb = kvb_ref.shape[1]
    zpad = jnp.zeros((LANES - nb, LANES), bf16)
    q8 = jnp.concatenate([q_ref[0, :, g * LANES:(g + 1) * LANES] for g in range(H_A)], axis=0)

    kb = jnp.concatenate([kvb_ref[0, :, 0:LANES].astype(bf16), zpad], axis=0)
    vb = jnp.concatenate([kvb_ref[0, :, LANES:2 * LANES].astype(bf16), zpad], axis=0)
    p = _masked_softmax_rows(_dot_nt(q8, kb), blkend_ref[...] <= t8)
    o_cmp = _dot(p.astype(bf16), vb)
    sel = []
    for kv in range(KV_A):
        imp = p[GQA_A * kv * tq:(GQA_A * kv + 1) * tq]
        for j in range(1, GQA_A):
            imp = imp + p[(GQA_A * kv + j) * tq:(GQA_A * kv + j + 1) * tq]
        sel.append(_select_blocks(imp, (t_col // L_SEL) * 2).astype(bf16))

    def slc_bias(kt):
        causal = (kt * tk + key) <= t_col
        per_kv = [jnp.where((_dot(sel[kv], exp_ref[kt]) > 0.5) & causal, 0.0, NEG) for kv in range(KV_A)]
        return jnp.concatenate([per_kv[g // GQA_A] for g in range(H_A)], axis=0)

    def win_bias(kt):
        d = t_col - (kt * tk + key)
        return jnp.concatenate([jnp.where((d >= 0) & (d < WINDOW), 0.0, NEG)] * H_A, axis=0)

    def flash(kv_ref, kt_lo, kt_hi, bias_fn):
        m_scr[...] = jnp.full(m_scr.shape, NEG, f32)
        l_scr[...] = jnp.zeros(l_scr.shape, f32)
        acc_scr[...] = jnp.zeros(acc_scr.shape, f32)

        def body(kt, carry):
            rows = pl.ds(pl.multiple_of(kt * tk, tk), tk)
            k = kv_ref[0, rows, 0:LANES].astype(bf16)
            v = kv_ref[0, rows, LANES:2 * LANES].astype(bf16)
            s = _dot_nt(q8, k) + bias_fn(kt)
            m, l, acc = _softmax_step2(s, m_scr[...], l_scr[...], acc_scr[...], v)
            m_scr[...] = m
            l_scr[...] = l
            acc_scr[...] = acc
            return carry

        lax.fori_loop(kt_lo, kt_hi, body, 0)
        return _softmax_finish(l_scr[...], acc_scr[...])

    kt_end = ((qi + 1) * tq - 1) // tk + 1
    o_slc = flash(slc_ref, 0, kt_end, slc_bias)
    o_win = flash(win_ref, jnp.maximum(qi * tq - WINDOW, 0) // tk, kt_end, win_bias)

    heads = []
    for g in range(H_A):
        rows = slice(g * tq, (g + 1) * tq)
        o = (misc[:, 3 * g:3 * g + 1] * o_cmp[rows] + misc[:, 3 * g + 1:3 * g + 2] * o_slc[rows]
             + misc[:, 3 * g + 2:3 * g + 3] * o_win[rows])
        heads.append(o if (g % 2) == g // GQA_A else pltpu.roll(o, HD, 1))
    for u in range(H_A // 2):
        o_ref[0, :, u * LANES:(u + 1) * LANES] = jnp.where(lo, heads[2 * u], heads[2 * u + 1])


def _nsa_prompt(qal, kvb, slc, win, misc3, consts, b, t, tq):
    nb = t // L_CMP
    tk = 2 * LANES
    return pl.pallas_call(
        _nsa_prompt_kernel,
        grid=(b, t // tq),
        in_specs=[pl.BlockSpec((1, tq, H_A * LANES), lambda i, j: (i, j, 0)),
                  pl.BlockSpec((1, nb, 2 * LANES), lambda i, j: (i, 0, 0)),
                  pl.BlockSpec((1, t, 2 * LANES), lambda i, j: (i, 0, 0)),
                  pl.BlockSpec((1, t, 2 * LANES), lambda i, j: (i, 0, 0)),
                  pl.BlockSpec((1, tq, LANES), lambda i, j: (i, j, 0)),
                  pl.BlockSpec((t // tk, LANES, tk), lambda i, j: (0, 0, 0)),
                  pl.BlockSpec((1, LANES), lambda i, j: (0, 0))],
        out_specs=pl.BlockSpec((1, tq, H_A * HD), lambda i, j: (i, j, 0)),
        out_shape=jax.ShapeDtypeStruct((b, t, H_A * HD), f32),
        scratch_shapes=[pltpu.VMEM((H_A * tq, LANES), f32), pltpu.VMEM((H_A * tq, LANES), f32),
                        pltpu.VMEM((H_A * tq, LANES), f32)],
        compiler_params=_cparams(("parallel", "parallel")),
        name="nsa_prompt",
    )(qal, kvb, slc, win, misc3, consts["expand"], consts["blkend"])


def _rope_tables(pos):
    half = HD // 2
    inv = ROPE_THETA ** (-jnp.arange(half, dtype=f32) / half)
    ang = pos.astype(f32)[:, None] * inv[None, :]
    cos, sin = jnp.cos(ang), jnp.sin(ang)
    return (jnp.concatenate([cos, cos, cos, cos], axis=-1),
            jnp.concatenate([-sin, sin, -sin, sin], axis=-1))


def _constants(t, tq):
    lane = np.arange(LANES)
    bd = ((lane[:, None] // HD) == (lane[None, :] // HD)).astype(np.float32) / HD
    ltri = (lane[None, :] <= lane[:, None]).astype(np.float32)
    key = np.arange(t).reshape(t // tq, 1, tq)
    expand = ((lane[None, :, None] % 2 == 0) & (key // L_SEL == lane[None, :, None] // 2)).astype(np.float32)
    blkend = np.where(lane < t // L_CMP, (lane + 1) * L_CMP - 1, 2 ** 30).astype(np.int32)[None, :]
    log_g = jnp.log1p(-jnp.exp2(-5.0 - jnp.arange(H_C, dtype=f32)))
    i = jnp.arange(RET_CHUNK, dtype=f32)
    diff = i[:, None] - i[None, :]
    dmask = jnp.where(diff >= 0, jnp.exp(jnp.maximum(diff, 0.0) * log_g[:, None, None]), 0.0)
    xi = jnp.exp((i + 1.0)[None, :] * log_g[:, None])
    zeta = jnp.exp((RET_CHUNK - 1.0 - i)[None, :] * log_g[:, None])
    g_c = jnp.exp(RET_CHUNK * log_g)
    by_lane = lambda a: jnp.repeat(a.reshape(2, 2, -1).transpose(0, 2, 1), HD, axis=-1)
    return {"bd": jnp.asarray(bd, bf16), "pool8": _pool_matrix(8, 2 * PAGE),
            "ltri": jnp.asarray(ltri, bf16),
            "expand": jnp.asarray(expand, bf16), "blkend": jnp.asarray(blkend),
            "ret_dmask": dmask, "ret_xi": by_lane(xi), "ret_zeta": by_lane(zeta),
            "ret_gc": by_lane(g_c[:, None]), "log_g": log_g}


def _pack_w_in(wt):
    off = np.cumsum([0, 512, 128, 128, 128, 128, 128, 128, 24, 512, 256, 256, 256, 4, 256, 256, 256, 256, 256])
    seg = lambda i: wt[off[i]:off[i + 1], :]
    (a_q, a_kc, a_vc, a_ks, a_vs, a_kw, a_vw, a_g, a_gate,
     b_q, b_k, b_v, b_f, b_gate, c_q, c_k, c_v, c_gate) = [seg(i) for i in range(18)]
    pad = jnp.zeros((LANES - MISC_G - H_B, wt.shape[1]), wt.dtype)
    return jnp.concatenate([a_q, a_kc, a_vc, a_ks, a_vs, a_kw, a_vw, a_g, b_f, pad,
                            a_gate, b_gate, c_gate, b_q, b_k, b_v, c_q, c_k, c_v], axis=0).astype(bf16)


def _layer_params(l, consts, norm_w, w_in, fox_bf, nsa_q_norm, nsa_k_norm, fox_q_norm, fox_k_norm,
                  ret_gn_w, w_out):
    two = lambda g: jnp.concatenate([g, g])[None, :]
    gains = jnp.concatenate([two(nsa_q_norm[l]), two(nsa_k_norm[l, 0]), two(nsa_k_norm[l, 1]),
                             two(nsa_k_norm[l, 2]), two(fox_q_norm[l]), two(fox_k_norm[l]),
                             jnp.zeros((2, LANES), f32)], axis=0)
    bf = jnp.zeros((1, LANES), f32).at[0, MISC_G:MISC_G + H_B].set(fox_bf[l])
    lw = dict(consts)
    lw.update(norm_w=norm_w[l][None, :], w_pack=_pack_w_in(jnp.transpose(w_in, (2, 0, 1))[:, l, :]), gains=gains, bf=bf,
              ret_gn_w=ret_gn_w[l][None, :], w_out=w_out[l].astype(bf16))
    return lw


def _prompt_layer(x, lw, consts, cos_t, sin_t):
    b, t, d = x.shape
    tq = 2 * LANES
    x2 = x.reshape(b * t, d)
    qal, cmp_, slc, win, misc, gate, bqal, fkv, cqkv, kvb = _proj(x2, lw, cos_t, sin_t, 256)
    r3 = lambda a: a.reshape(b, t, a.shape[-1])
    misc3 = r3(misc)
    o_a = _nsa_prompt(r3(qal), kvb.reshape(b, t // L_CMP, 2 * LANES), r3(slc), r3(win), misc3,
                      consts, b, t, tq)
    ccol, crow = _fox_prep(misc3, consts["ltri"])
    o_b = _fox_prompt(r3(bqal), r3(fkv), ccol, crow, b, t)
    o_c, r_new = _ret_prompt(r3(cqkv), lw, b, t)
    y = _merge(x2, o_a.reshape(b * t, -1), o_b.reshape(b * t, -1), o_c.reshape(b * t, -1), gate,
               lw["w_out"], 256)
    wk = min(WINDOW, t)
    states = (cmp_.reshape(b, t, 2, KV_A, HD), slc.reshape(b, t, 2, KV_A, HD),
              fkv.reshape(b, t, 2, H_B, HD), misc3[:, :, MISC_G:MISC_G + H_B],
              r3(win)[:, t - wk:].reshape(b, wk, 2, KV_A, HD), r_new)
    return y.reshape(b, t, d), states


def _head_col(vec_row, shape, base, stride):
    pick = _lane(shape) == base + stride * _row(shape)
    return jnp.sum(jnp.where(pick, jnp.broadcast_to(vec_row, shape), 0.0), axis=1, keepdims=True)


def _seg_mean(x, seg):
    lane = _lane(x.shape)
    sh = seg // 2
    while sh >= 1:
        x = x + jnp.where((lane & sh) != 0, pltpu.roll(x, sh, 1), pltpu.roll(x, LANES - sh, 1))
        sh //= 2
    return x * (1.0 / seg)


def _tree(op, xs):
    xs = list(xs)
    while len(xs) > 1:
        xs = [op(xs[i], xs[i + 1]) if i + 1 < len(xs) else xs[i] for i in range(0, len(xs), 2)]
    return xs[0]


SEQ_PER_STEP = 2


def _seq_views(refs, s, lead_axis=0):
    one = pl.ds(s, 1)
    return [r.at[one] if lead_axis == 0 else r.at[:, one] for r in refs]


def _dec_nsa_kernel(pt_ref, q_ref, slc_new_ref, win_new_ref, misc_ref, win_ref, *rest, n_pages):
    o_ref = rest[2 * n_pages * SEQ_PER_STEP]
    for s in range(SEQ_PER_STEP):
        pages = rest[2 * n_pages * s:2 * n_pages * (s + 1)]
        q, slc_new, win_new, misc, o = _seq_views([q_ref, slc_new_ref, win_new_ref, misc_ref, o_ref], s)
        win, = _seq_views([win_ref], s, lead_axis=1)
        _dec_nsa_seq(q, slc_new, win_new, misc, win, pages[0:n_pages], pages[n_pages:], o, n_pages)


def _dec_nsa_seq(q_ref, slc_new_ref, win_new_ref, misc_ref, win_ref, cmp_pages, slc_pages, o_ref, n_pages):
    shape = (H_A, LANES)
    lane = _lane(shape)
    row = _row(shape)
    lo = lane < HD
    q8 = q_ref[0]
    qf = q8.astype(f32)
    past = n_pages * PAGE
    cur2 = 2 * (past // L_SEL)
    bf_round = lambda a: a.astype(bf16).astype(f32)
    k_of = lambda pg: pg[0, 0, 0:LANES, :].astype(bf16)
    v_of = lambda pg: pg[0, 0, LANES:2 * LANES, :].astype(bf16)
    row_max = lambda xs: jnp.max(_tree(jnp.maximum, xs), axis=1, keepdims=True)
    row_sum = lambda xs: jnp.sum(_tree(jnp.add, xs), axis=1, keepdims=True)

    sb = [_seg_mean(_dot(q8, k_of(pg)), L_CMP) for pg in cmp_pages]
    m = row_max(sb)
    p = [jnp.exp(x - m) * (1.0 / L_CMP) for x in sb]
    cl = row_sum(p)
    o_cmp = _tree(jnp.add, [_dot_nt(x.astype(bf16), v_of(pg)) for x, pg in zip(p, cmp_pages)]) / cl
    to_prob = L_CMP / cl
    imp2 = jnp.zeros(shape, f32)
    for page, x in enumerate(p):
        e = x * to_prob
        e = jnp.where(row < GQA_A,
                      jnp.sum(e[0:GQA_A], axis=0, keepdims=True),
                      jnp.sum(e[GQA_A:2 * GQA_A], axis=0, keepdims=True))
        pair = e + pltpu.roll(e, LANES - L_CMP, 1)
        first = pltpu.roll(pair, 4 * page, 1) if page else pair
        imp2 = jnp.where(lane == 4 * page, first, imp2)
        imp2 = jnp.where(lane == 4 * page + 2, pltpu.roll(pair, (4 * page + 2 + HD) % LANES, 1), imp2)
    sel = _select_pairs(imp2, cur2, _top_n_rank)

    k_new = bf_round(slc_new_ref[0, :, 0:LANES])
    v_new = bf_round(slc_new_ref[0, :, LANES:2 * LANES])
    s_new = jnp.sum(qf * k_new, axis=1, keepdims=True)
    s = []
    for page, pg in enumerate(slc_pages):
        hit = jnp.where(lo, sel[:, 4 * page:4 * page + 1], sel[:, 4 * page + 2:4 * page + 3]) > 0.5
        s.append(jnp.where(hit, _dot(q8, k_of(pg)), NEG))
    m = jnp.maximum(row_max(s), s_new)
    p = [jnp.exp(x - m) for x in s]
    p_new = jnp.exp(s_new - m)
    acc = _tree(jnp.add, [_dot_nt(x.astype(bf16), v_of(pg)) for x, pg in zip(p, slc_pages)])
    o_slc = (acc + p_new * v_new) / (row_sum(p) + p_new)

    wk = win_ref.shape[3]
    kw = win_ref[0, 0, 0:LANES, :].astype(bf16)
    vw = win_ref[0, 0, LANES:2 * LANES, :].astype(bf16)
    kw_new = bf_round(win_new_ref[0, :, 0:LANES])
    vw_new = bf_round(win_new_ref[0, :, LANES:2 * LANES])
    s = jnp.where(_lane((H_A, wk)) > wk - WINDOW, _dot(q8, kw), NEG)
    s_new = jnp.sum(qf * kw_new, axis=1, keepdims=True)
    m = jnp.maximum(jnp.max(s, axis=1, keepdims=True), s_new)
    pw = jnp.exp(s - m)
    pw_new = jnp.exp(s_new - m)
    o_win = ((_dot_nt(pw.astype(bf16), vw) + pw_new * vw_new)
             / (jnp.sum(pw, axis=1, keepdims=True) + pw_new))

    misc = misc_ref[0]
    o8 = (_head_col(misc, shape, 0, 3) * o_cmp + _head_col(misc, shape, 1, 3) * o_slc
          + _head_col(misc, shape, 2, 3) * o_win)
    o8r = pltpu.roll(o8, HD, 1)
    lo1 = _lane((1, LANES)) < HD
    for u in range(H_A // 2):
        kv_lo = (2 * u) // GQA_A == 0
        even = (o8 if kv_lo else o8r)[2 * u:2 * u + 1, :]
        odd = (o8r if kv_lo else o8)[2 * u + 1:2 * u + 2, :]
        o_ref[0, :, u * LANES:(u + 1) * LANES] = jnp.where(lo1, even, odd)


def _page_specs(block, layer, count, s):
    def spec(g):
        return pl.BlockSpec(block, lambda b, pt: (layer, pt[(b * SEQ_PER_STEP + s) * count + g])
                            + (0,) * (len(block) - 2))
    return [spec(g) for g in range(count)]


def _dec_nsa(pt_flat, q8, slc_new, win_new, misc, win_state, cache_cmp, cache_slc, layer):
    n_seq = q8.shape[0]
    n_pages = pt_flat.shape[0] // n_seq
    wk = win_state.shape[3]
    sub = SEQ_PER_STEP
    seq3 = lambda w: pl.BlockSpec((sub, 1, w), lambda b, pt: (b, 0, 0))
    page_block = (1, 1, 2 * LANES, PAGE)
    page_specs, page_args = [], []
    for s in range(sub):
        page_specs += _page_specs(page_block, layer, n_pages, s) + _page_specs(page_block, layer, n_pages, s)
        page_args += [cache_cmp] * n_pages + [cache_slc] * n_pages
    grid_spec = pltpu.PrefetchScalarGridSpec(
        num_scalar_prefetch=1,
        grid=(n_seq // sub,),
        in_specs=[pl.BlockSpec((sub, H_A, LANES), lambda b, pt: (b, 0, 0)),
                  seq3(2 * LANES), seq3(2 * LANES), seq3(LANES),
                  pl.BlockSpec((1, sub, 2 * LANES, wk), lambda b, pt: (layer, b, 0, 0))] + page_specs,
        out_specs=pl.BlockSpec((sub, 1, H_A * HD), lambda b, pt: (b, 0, 0)))
    return pl.pallas_call(
        functools.partial(_dec_nsa_kernel, n_pages=n_pages),
        grid_spec=grid_spec,
        out_shape=jax.ShapeDtypeStruct((n_seq, 1, H_A * HD), f32),
        compiler_params=_cparams(("parallel",)),
        name="dec_nsa",
    )(pt_flat, q8, slc_new, win_new, misc, win_state, *page_args)


def _lane_cumsum(x):
    lane = _lane(x.shape)
    sh = 1
    while sh < LANES:
        x = x + jnp.where(lane >= sh, pltpu.roll(x, sh, 1), 0.0)
        sh *= 2
    return x


def _dec_fox_kernel(pt_ref, q_ref, kv_new_ref, misc_ref, *rest, n_pages):
    o_ref = rest[2 * n_pages * SEQ_PER_STEP]
    for s in range(SEQ_PER_STEP):
        pages = rest[2 * n_pages * s:2 * n_pages * (s + 1)]
        q, kv_new, misc, o = _seq_views([q_ref, kv_new_ref, misc_ref, o_ref], s)
        _dec_fox_seq(q, kv_new, misc, pages[0:n_pages], pages[n_pages:], o)


def _dec_fox_seq(q_ref, kv_new_ref, misc_ref, kv_pages, lf_pages, o_ref):
    shape = (8, LANES)
    first_pair = _row(shape) < 2
    q8 = q_ref[0]
    unit = lambda kv, u: kv[0, 0, u * LANES:(u + 1) * LANES, :].astype(bf16)

    cs = [_lane_cumsum(lf[0, 0]) for lf in lf_pages]
    carry = jnp.zeros((8, 1), f32)
    s = []
    for kv, c in zip(kv_pages, cs):
        s.append(jnp.where(first_pair, _dot(q8, unit(kv, 0)), _dot(q8, unit(kv, 1))) - (carry + c))
        carry = carry + c[:, LANES - 1:LANES]
    bf_round = lambda a: jnp.broadcast_to(a.astype(bf16).astype(f32), shape)
    new = kv_new_ref[0]
    k_new = jnp.where(first_pair, bf_round(new[:, 0:LANES]), bf_round(new[:, LANES:2 * LANES]))
    v_new = jnp.where(first_pair, bf_round(new[:, 2 * LANES:3 * LANES]), bf_round(new[:, 3 * LANES:4 * LANES]))
    s_new = (jnp.sum(q8.astype(f32) * k_new, axis=1, keepdims=True)
             - (carry + _head_col(misc_ref[0], shape, MISC_G, 1)))
    m = jnp.maximum(jnp.max(_tree(jnp.maximum, s), axis=1, keepdims=True), s_new)
    p = [jnp.exp(x - m) for x in s]
    p_new = jnp.exp(s_new - m)
    den = jnp.sum(_tree(jnp.add, p), axis=1, keepdims=True) + p_new
    pb = [x.astype(bf16) for x in p]
    acc0 = _tree(jnp.add, [_dot_nt(x, unit(kv, 2)) for x, kv in zip(pb, kv_pages)])
    acc1 = _tree(jnp.add, [_dot_nt(x, unit(kv, 3)) for x, kv in zip(pb, kv_pages)])
    o8 = (jnp.where(first_pair, acc0, acc1) + p_new * v_new) / den
    lo1 = _lane((1, LANES)) < HD
    for u in range(H_B // 2):
        o_ref[0, :, u * LANES:(u + 1) * LANES] = jnp.where(lo1, o8[2 * u:2 * u + 1, :],
                                                           o8[2 * u + 1:2 * u + 2, :])


def _dec_fox(pt_flat, qb4, fkv_new, misc, cache_fkv, cache_lft, layer):
    n_seq = qb4.shape[0]
    n_pages = pt_flat.shape[0] // n_seq
    sub = SEQ_PER_STEP
    seq3 = lambda w: pl.BlockSpec((sub, 1, w), lambda b, pt: (b, 0, 0))
    page_specs, page_args = [], []
    for s in range(sub):
        page_specs += (_page_specs((1, 1, 4 * LANES, PAGE), layer, n_pages, s)
                       + _page_specs((1, 1, 8, PAGE), layer, n_pages, s))
        page_args += [cache_fkv] * n_pages + [cache_lft] * n_pages
    grid_spec = pltpu.PrefetchScalarGridSpec(
        num_scalar_prefetch=1,
        grid=(n_seq // sub,),
        in_specs=[pl.BlockSpec((sub, 8, LANES), lambda b, pt: (b, 0, 0)), seq3(4 * LANES), seq3(LANES)]
        + page_specs,
        out_specs=pl.BlockSpec((sub, 1, H_B * HD), lambda b, pt: (b, 0, 0)))
    return pl.pallas_call(
        functools.partial(_dec_fox_kernel, n_pages=n_pages),
        grid_spec=grid_spec,
        out_shape=jax.ShapeDtypeStruct((n_seq, 1, H_B * HD), f32),
        compiler_params=_cparams(("parallel",)),
        name="dec_fox",
    )(pt_flat, qb4, fkv_new, misc, *page_args)


def _dec_ret_kernel(q_ref, k_ref, v_ref, r_ref, gam_ref, gnw_ref, o_ref, rn_ref):
    q = q_ref[0].astype(f32)
    k = k_ref[0].astype(f32)
    v = v_ref[0].astype(f32)
    qk = jnp.sum(q * k, axis=1, keepdims=True)
    eye = _row((HD, HD)) == _lane((HD, HD))
    col = lambda a: jnp.sum(jnp.where(eye, jnp.broadcast_to(a, (HD, HD)), 0.0), axis=1, keepdims=True)
    for h in range(H_C):
        hs = slice(h, h + 1)
        r = r_ref[0, 0, h]
        gam = gam_ref[hs, :]
        o = qk[hs, :] * v[hs, :] + jnp.sum(col(q[hs, :]) * r, axis=0, keepdims=True) * gam
        rn_ref[0, h] = r * gam + col(k[hs, :]) * v[hs, :]
        d = o - jnp.mean(o, axis=1, keepdims=True)
        var = jnp.mean(d * d, axis=1, keepdims=True)
        o_ref[0, hs, :] = d * lax.rsqrt(var + EPS) * gnw_ref[hs, :]


def _dec_ret(cq, ck, cv, state_ret, gam, gnw, layer):
    n_seq = cq.shape[0]
    head = pl.BlockSpec((1, H_C, HD), lambda b: (b, 0, 0))
    table = pl.BlockSpec((H_C, HD), lambda b: (0, 0))
    return pl.pallas_call(
        _dec_ret_kernel,
        grid=(n_seq,),
        in_specs=[head, head, head,
                  pl.BlockSpec((1, 1, H_C, HD, HD), lambda b: (layer, b, 0, 0, 0)), table, table],
        out_specs=[head, pl.BlockSpec((1, H_C, HD, HD), lambda b: (b, 0, 0, 0))],
        out_shape=[jax.ShapeDtypeStruct((n_seq, H_C, HD), f32),
                   jax.ShapeDtypeStruct((n_seq, H_C, HD, HD), f32)],
        compiler_params=_cparams(("parallel",)),
        name="dec_ret",
    )(cq, ck, cv, state_ret, gam, gnw)


def _sample_layer(x, layer, lw, consts, caches, pt_flat, cos_t, sin_t):
    n, _, d = x.shape
    x2 = x.reshape(n, d)
    qal, cmp_, slc, win, misc, gate, bqal, fkv, cqkv, _ = _proj(x2, lw, cos_t, sin_t, n)
    row3 = lambda a: a[:, None, :]
    o_a8 = _dec_nsa(pt_flat, qal.reshape(n, H_A, LANES), row3(slc), row3(win), row3(misc),
                    caches["win"], caches["cmp"], caches["slc"], layer)
    qb8 = jnp.pad(bqal.reshape(n, H_B, LANES), ((0, 0), (0, 8 - H_B), (0, 0)))
    o_b8 = _dec_fox(pt_flat, qb8, row3(fkv), row3(misc),
                    caches["fkv"], caches["lft"], layer)
    heads = lambda u: cqkv[:, 2 * u * LANES:(2 * u + 2) * LANES].reshape(n, H_C, HD).astype(f32)
    gam = jnp.broadcast_to(jnp.exp(consts["log_g"])[:, None], (H_C, HD))
    o_c, r_new = _dec_ret(heads(0), heads(1), heads(2), caches["ret"], gam,
                          lw["ret_gn_w"].reshape(H_C, HD), layer)
    y = _merge(x2, o_a8.reshape(n, H_A * HD), o_b8.reshape(n, H_B * HD), o_c.reshape(n, H_C * HD), gate,
               lw["w_out"], n)
    states = (cmp_.reshape(n, 1, 2, KV_A, HD), slc.reshape(n, 1, 2, KV_A, HD),
              fkv.reshape(n, 1, 2, H_B, HD), misc[:, None, MISC_G:MISC_G + H_B], win, r_new)
    return y.reshape(n, 1, d), states


def kernel(x_prompt, x_sample, cache_nsa_cmp, cache_nsa_slc, cache_fox_kv, cache_fox_logf,
           state_nsa_win, state_ret, page_table, norm_w, w_in, fox_bf, nsa_q_norm, nsa_k_norm,
           fox_q_norm, fox_k_norm, ret_gn_w, w_out):
    depth = norm_w.shape[0]
    b, t, _ = x_prompt.shape
    n_seq, n_pages = page_table.shape
    assert x_sample.shape[1] == 1 and t % 256 == 0 and n_pages % 2 == 0
    past = n_pages * PAGE
    consts = _constants(t, 2 * LANES)
    cos_p, sin_p = _rope_tables(jnp.arange(t, dtype=jnp.int32))
    cos_s, sin_s = _rope_tables(jnp.full((n_seq,), past, dtype=jnp.int32))
    n_phys = cache_nsa_cmp.shape[1]
    feat_major = lambda a, lead: jnp.transpose(a, (0, 1, 3, 4, 5, 2)).reshape(depth, lead, -1, a.shape[2])
    caches = {"cmp": feat_major(cache_nsa_cmp, n_phys), "slc": feat_major(cache_nsa_slc, n_phys),
              "fkv": feat_major(cache_fox_kv, n_phys),
              "lft": jnp.pad(jnp.swapaxes(cache_fox_logf, 2, 3), ((0, 0), (0, 0), (0, 8 - H_B), (0, 0))),
              "win": feat_major(state_nsa_win, n_seq),
              "ret": state_ret}
    pt_flat = page_table.reshape(-1)
    yp, ys = x_prompt, x_sample
    p_states, s_states = [], []
    for l in range(depth):
        lw = _layer_params(l, consts, norm_w, w_in, fox_bf, nsa_q_norm, nsa_k_norm, fox_q_norm,
                           fox_k_norm, ret_gn_w, w_out)
        yp, ps = _prompt_layer(yp, lw, consts, cos_p, sin_p)
        ys, ss = _sample_layer(ys, l, lw, consts, caches, pt_flat, cos_s, sin_s)
        p_states.append(ps)
        s_states.append(ss)
    stk = lambda states, i: jnp.stack([st[i] for st in states], axis=0)
    win_t = caches["win"]
    wk = win_t.shape[3]
    new_win_t = jnp.concatenate([win_t[:, :, :, 1:], stk(s_states, 4)[:, :, :, None]], axis=3)
    new_win = jnp.transpose(new_win_t.reshape(depth, n_seq, 2, KV_A, HD, wk), (0, 1, 5, 2, 3, 4))
    out = [yp, ys]
    for i in range(6):
        out += [stk(p_states, i), new_win if i == 4 else stk(s_states, i)]
    return tuple(out)
```

```python
import functools

import numpy as np
import jax
import jax.numpy as jnp
from jax import lax
from jax.experimental import pallas as pl
from jax.experimental.pallas import tpu as pltpu

f32 = jnp.float32
bf16 = jnp.bfloat16

HD = 64
H_A, KV_A, GQA_A = 8, 2, 4
H_B, H_C = 4, 4
L_CMP, L_SEL, TOP_N, WINDOW = 32, 64, 8, 512
PAGE = 128
RET_CHUNK = 128
ROPE_THETA = 10000.0
EPS = 1e-6
SCALE = HD ** -0.5
FORCE_BONUS = 1e3
NEG = -1e30

LANES = 128
VMEM_LIMIT = 48 * 1024 * 1024

C_AQ, C_CMP, C_SLC, C_WIN, C_MISC, C_GATE, C_BQ, C_BKV, C_CQKV, D_PACK = (
    0, 512, 768, 1024, 1280, 1408, 2432, 2688, 3200, 3968)
MISC_G = 3 * H_A


def _cparams(sem):
    return pltpu.CompilerParams(dimension_semantics=sem, vmem_limit_bytes=VMEM_LIMIT)


def _dot(a, b):
    return jnp.dot(a, b, preferred_element_type=f32)


def _dot_nt(a, b):
    return lax.dot_general(a, b, (((1,), (1,)), ((), ())), preferred_element_type=f32)


def _split(x, n):
    parts = []
    r = x
    for i in range(n):
        h = r.astype(bf16)
        parts.append(h)
        if i + 1 < n:
            r = r - h.astype(f32)
    return parts


def _dot_split(x, w, n=2):
    acc = None
    for h in _split(x, n):
        d = _dot(h, w)
        acc = d if acc is None else acc + d
    return acc


def _dot_split_l(w, x, n=2):
    acc = None
    for h in _split(x, n):
        d = _dot(w, h)
        acc = d if acc is None else acc + d
    return acc


def _lane(shape):
    return lax.broadcasted_iota(jnp.int32, shape, len(shape) - 1)


def _row(shape):
    return lax.broadcasted_iota(jnp.int32, shape, len(shape) - 2)


def _swap_half_heads(y):
    lane = _lane(y.shape)
    return jnp.where((lane & 32) == 0, pltpu.roll(y, 96, 1), pltpu.roll(y, 32, 1))


def _head_rms(z, bd, g):
    ms = _dot_split(z * z, bd)
    return z * lax.rsqrt(ms + EPS) * g


def _rope(y, cos, sin):
    return y * cos + _swap_half_heads(y) * sin


def _sigmoid(z):
    return 1.0 / (1.0 + jnp.exp(-z))


def _proj_kernel(x_ref, nw_ref, w_ref, bd_ref, pool_ref, cos_ref, sin_ref, gn_ref, bf_ref,
                 qal_ref, cmp_ref, slc_ref, win_ref, misc_ref, gate_ref, bqal_ref, fkv_ref,
                 cqkv_ref, kvb_ref):
    x = x_ref[...]
    ms = jnp.mean(x * x, axis=-1, keepdims=True)
    xn = (x * lax.rsqrt(ms + EPS) * nw_ref[...]).astype(bf16)
    bd = bd_ref[...]
    cos = cos_ref[...]
    sin = sin_ref[...]
    lane = _lane((x.shape[0], LANES))
    lo = lane < HD

    def z_of(c0, width):
        return _dot_nt(xn, w_ref[c0:c0 + width, :])

    def unit(z, u):
        return z[:, u * LANES:(u + 1) * LANES]

    z = z_of(C_AQ, H_A * HD)
    g_aq = gn_ref[0:1, :]
    for u in range(H_A // 2):
        y = _rope(_head_rms(unit(z, u), bd, g_aq), cos, sin) * SCALE
        yr = pltpu.roll(y, HD, 1)
        kv_lo = (2 * u) // GQA_A == 0
        keep = lo if kv_lo else jnp.logical_not(lo)
        even, odd = (y, yr) if kv_lo else (yr, y)
        qal_ref[:, (2 * u) * LANES:(2 * u + 1) * LANES] = jnp.where(keep, even, 0.0).astype(bf16)
        qal_ref[:, (2 * u + 1) * LANES:(2 * u + 2) * LANES] = jnp.where(keep, odd, 0.0).astype(bf16)

    for i, (c0, o_ref) in enumerate(((C_CMP, cmp_ref), (C_SLC, slc_ref), (C_WIN, win_ref))):
        z = z_of(c0, 2 * LANES)
        k = _rope(_head_rms(unit(z, 0), bd, gn_ref[1 + i:2 + i, :]), cos, sin)
        v = unit(z, 1)
        o_ref[:, 0:LANES] = k
        o_ref[:, LANES:2 * LANES] = v
        if i == 0:
            pool = pool_ref[...]
            kvb_ref[:, 0:LANES] = _dot_split_l(pool, k)
            kvb_ref[:, LANES:2 * LANES] = _dot_split_l(pool, v)

    z = z_of(C_MISC, LANES)
    zf = z + bf_ref[...]
    logf = jnp.minimum(zf, 0.0) - jnp.log1p(jnp.exp(-jnp.abs(zf)))
    misc_ref[...] = jnp.where(lane < MISC_G, _sigmoid(z), jnp.where(lane < MISC_G + H_B, logf, 0.0))

    for c in range(0, 1024, 512):
        z = z_of(C_GATE + c, 512)
        gate_ref[:, c:c + 512] = z * _sigmoid(z)

    z = z_of(C_BQ, H_B * HD)
    for u in range(H_B // 2):
        y = _head_rms(unit(z, u), bd, gn_ref[4:5, :]) * SCALE
        bqal_ref[:, (2 * u) * LANES:(2 * u + 1) * LANES] = jnp.where(lo, y, 0.0).astype(bf16)
        bqal_ref[:, (2 * u + 1) * LANES:(2 * u + 2) * LANES] = jnp.where(lo, 0.0, y).astype(bf16)
    z = z_of(C_BKV, 2 * H_B * HD)
    for u in range(2):
        fkv_ref[:, u * LANES:(u + 1) * LANES] = _head_rms(unit(z, u), bd, gn_ref[5:6, :])
    fkv_ref[:, 2 * LANES:4 * LANES] = z[:, 2 * LANES:4 * LANES]

    z = z_of(C_CQKV, 3 * H_C * HD)
    for u in range(2):
        cqkv_ref[:, u * LANES:(u + 1) * LANES] = _rope(unit(z, u), cos, sin).astype(bf16)
        cqkv_ref[:, (2 + u) * LANES:(3 + u) * LANES] = (_rope(unit(z, 2 + u), cos, sin) * SCALE).astype(bf16)
    cqkv_ref[:, 4 * LANES:6 * LANES] = z[:, 4 * LANES:6 * LANES].astype(bf16)


def _pool_matrix(n_out, n_rows):
    r = np.arange(n_rows)[None, :] // L_CMP == np.arange(n_out)[:, None]
    return jnp.asarray(r.astype(np.float32) / L_CMP, bf16)


def _proj(x2, lw, cos_t, sin_t, tm):
    n = x2.shape[0]
    nt = cos_t.shape[0] // tm
    const = lambda i: (0, 0)
    rows = lambda i: (i, 0)
    outs = [(H_A * LANES, bf16), (2 * LANES, f32), (2 * LANES, f32), (2 * LANES, f32), (LANES, f32),
            (1024, f32), (H_B * LANES, bf16), (4 * LANES, f32), (6 * LANES, bf16)]
    out_shape = [jax.ShapeDtypeStruct((n, w), dt) for w, dt in outs]
    out_specs = [pl.BlockSpec((tm, w), rows) for w, _ in outs]
    out_shape.append(jax.ShapeDtypeStruct((n // L_CMP, 2 * LANES), f32))
    out_specs.append(pl.BlockSpec((tm // L_CMP, 2 * LANES), rows))
    return pl.pallas_call(
        _proj_kernel,
        grid=(n // tm,),
        in_specs=[pl.BlockSpec((tm, 1024), rows),
                  pl.BlockSpec((1, 1024), const),
                  pl.BlockSpec((D_PACK, 1024), const),
                  pl.BlockSpec((LANES, LANES), const),
                  pl.BlockSpec((tm // L_CMP, tm), const),
                  pl.BlockSpec((tm, LANES), lambda i: (i % nt, 0)),
                  pl.BlockSpec((tm, LANES), lambda i: (i % nt, 0)),
                  pl.BlockSpec((8, LANES), const),
                  pl.BlockSpec((1, LANES), const)],
        out_specs=out_specs,
        out_shape=out_shape,
        compiler_params=_cparams(("parallel",)),
        name="proj",
    )(x2, lw["norm_w"], lw["w_pack"], lw["bd"], _pool_matrix(tm // L_CMP, tm), cos_t, sin_t,
      lw["gains"], lw["bf"])


def _merge_kernel(x_ref, oa_ref, ob_ref, oc_ref, gate_ref, w_ref, y_ref):
    g = gate_ref[...]
    wa, wb = H_A * HD, H_A * HD + H_B * HD
    acc = _dot((oa_ref[...] * g[:, 0:wa]).astype(bf16), w_ref[0:wa, :])
    acc += _dot((ob_ref[...] * g[:, wa:wb]).astype(bf16), w_ref[wa:wb, :])
    acc += _dot((oc_ref[...] * g[:, wb:]).astype(bf16), w_ref[wb:, :])
    y_ref[...] = x_ref[...] + acc


def _merge(x2, oa, ob, oc, gate, w_out, tm):
    n = x2.shape[0]
    rows = lambda i: (i, 0)
    return pl.pallas_call(
        _merge_kernel,
        grid=(n // tm,),
        in_specs=[pl.BlockSpec((tm, 1024), rows), pl.BlockSpec((tm, 512), rows),
                  pl.BlockSpec((tm, 256), rows), pl.BlockSpec((tm, 256), rows),
                  pl.BlockSpec((tm, 1024), rows), pl.BlockSpec((1024, 1024), lambda i: (0, 0))],
        out_specs=pl.BlockSpec((tm, 1024), rows),
        out_shape=jax.ShapeDtypeStruct((n, 1024), f32),
        compiler_params=_cparams(("parallel",)),
        name="merge",
    )(x2, oa, ob, oc, gate, w_out)


def _ret_prompt_kernel(q_ref, k_ref, v_ref, dm_ref, xi_ref, zeta_ref, gc_ref, gnw_ref, bd_ref,
                       o_ref, r_ref):
    c = RET_CHUNK
    n_chunks = q_ref.shape[1] // c
    pairs = range(H_C // 2)
    lo = _lane((c, LANES)) < HD
    same_head = (_row((LANES, LANES)) < HD) == (_lane((LANES, LANES)) < HD)
    bd = bd_ref[...]

    def step(i, rs):
        rows = pl.ds(pl.multiple_of(i * c, c), c)
        out = []
        for p, r in zip(pairs, rs):
            lanes = slice(p * LANES, (p + 1) * LANES)
            q = q_ref[0, rows, lanes]
            k = k_ref[0, rows, lanes]
            v = v_ref[0, rows, lanes]
            zero = jnp.zeros_like(q)
            a0 = _dot_nt(jnp.where(lo, q, zero), k) * dm_ref[2 * p]
            a1 = _dot_nt(jnp.where(lo, zero, q), k) * dm_ref[2 * p + 1]
            o = jnp.where(lo, _dot(a0.astype(bf16), v), _dot(a1.astype(bf16), v))
            o = o + _dot(q, r.astype(bf16)) * xi_ref[p]
            kz_t = (k.astype(f32) * zeta_ref[p]).T.astype(bf16)
            out.append(r * gc_ref[p] + jnp.where(same_head, _dot(kz_t, v), 0.0))
            mu = _dot_split(o, bd)
            d = o - mu
            var = _dot_split(d * d, bd)
            o_ref[0, rows, lanes] = d * lax.rsqrt(var + EPS) * gnw_ref[:, lanes]
        return tuple(out)

    rs = lax.fori_loop(0, n_chunks, step, tuple(jnp.zeros((LANES, LANES), f32) for _ in pairs))
    for p, r in zip(pairs, rs):
        r_ref[0, 2 * p] = r[0:HD, 0:HD]
        r_ref[0, 2 * p + 1] = pltpu.roll(r, HD, 1)[HD:2 * HD, 0:HD]


def _ret_prompt(cqkv, lw, b, t):
    width = H_C * HD
    blk = lambda u: pl.BlockSpec((1, t, width), lambda i: (i, 0, u))
    whole = lambda a: pl.BlockSpec(a.shape, lambda i: (0,) * a.ndim)
    tables = [lw["ret_dmask"], lw["ret_xi"], lw["ret_zeta"], lw["ret_gc"], lw["ret_gn_w"], lw["bd"]]
    return pl.pallas_call(
        _ret_prompt_kernel,
        grid=(b,),
        in_specs=[blk(0), blk(1), blk(2)] + [whole(a) for a in tables],
        out_specs=[pl.BlockSpec((1, t, width), lambda i: (i, 0, 0)),
                   pl.BlockSpec((1, H_C, HD, HD), lambda i: (i, 0, 0, 0))],
        out_shape=[jax.ShapeDtypeStruct((b, t, width), f32),
                   jax.ShapeDtypeStruct((b, H_C, HD, HD), f32)],
        compiler_params=_cparams(("parallel",)),
        name="ret_prompt",
    )(cqkv, cqkv, cqkv, *tables)


def _fox_prep_kernel(misc_ref, ltri_ref, ccol_ref, crow_ref):
    c = LANES
    n_chunks = misc_ref.shape[1] // c
    ltri = ltri_ref[...]

    def step(i, carry):
        rows = pl.ds(pl.multiple_of(i * c, c), c)
        cs = _dot_split_l(ltri, misc_ref[0, rows, :], 3) + carry
        for p in range(2):
            sh = pltpu.roll(cs, LANES - (MISC_G + 2 * p), 1)
            ccol_ref[0, p, rows, :] = sh
            crow_ref[0, p, i] = sh.T[0:8, :]
        return cs[c - 1:c, :]

    lax.fori_loop(0, n_chunks, step, jnp.zeros((1, LANES), f32))


def _fox_prep(misc3, ltri):
    b, t, _ = misc3.shape
    return pl.pallas_call(
        _fox_prep_kernel,
        grid=(b,),
        in_specs=[pl.BlockSpec((1, t, LANES), lambda i: (i, 0, 0)),
                  pl.BlockSpec((LANES, LANES), lambda i: (0, 0))],
        out_specs=[pl.BlockSpec((1, 2, t, LANES), lambda i: (i, 0, 0, 0)),
                   pl.BlockSpec((1, 2, t // LANES, 8, LANES), lambda i: (i, 0, 0, 0, 0))],
        out_shape=[jax.ShapeDtypeStruct((b, 2, t, LANES), f32),
                   jax.ShapeDtypeStruct((b, 2, t // LANES, 8, LANES), f32)],
        compiler_params=_cparams(("parallel",)),
        name="fox_prep",
    )(misc3, ltri)


def _softmax_step2(s, m, l, acc, v):
    s_a, s_b = s[:, 0:LANES], s[:, LANES:2 * LANES]
    m_new = jnp.maximum(m, jnp.max(jnp.maximum(s_a, s_b), axis=1, keepdims=True))
    alpha = jnp.exp(m - m_new)
    p_a = jnp.exp(s_a - m_new)
    p_b = jnp.exp(s_b - m_new)
    l = alpha * l + (p_a + p_b)
    pv = _dot(p_a.astype(bf16), v[0:LANES]) + _dot(p_b.astype(bf16), v[LANES:2 * LANES])
    return m_new, l, alpha * acc + pv


def _softmax_finish(l, acc):
    return acc / jnp.sum(l, axis=1, keepdims=True)


def _fox_prompt_kernel(q_ref, k_ref, v_ref, ccol_ref, crow_ref, o_ref, m_scr, l_scr, acc_scr):
    tq = q_ref.shape[1]
    qi = pl.program_id(1)
    pairs = range(H_B // 2)
    q2, cq = [], []
    for p in pairs:
        q2.append(jnp.concatenate([q_ref[0, :, (2 * p) * LANES:(2 * p + 1) * LANES],
                                   q_ref[0, :, (2 * p + 1) * LANES:(2 * p + 2) * LANES]], axis=0))
        cc = ccol_ref[0, p]
        cq.append(jnp.broadcast_to(jnp.concatenate([cc[:, 0:1], cc[:, 1:2]], axis=0), (2 * tq, tq)))
    m_scr[...] = jnp.full(m_scr.shape, NEG, f32)
    l_scr[...] = jnp.zeros(l_scr.shape, f32)
    acc_scr[...] = jnp.zeros(acc_scr.shape, f32)

    def tile(kt, masked):
        rows = pl.ds(pl.multiple_of(kt * tq, tq), tq)
        for p in pairs:
            k = k_ref[0, rows, p * LANES:(p + 1) * LANES].astype(bf16)
            v = v_ref[0, rows, p * LANES:(p + 1) * LANES].astype(bf16)
            ck = jnp.concatenate([crow_ref[0, p, 2 * kt], crow_ref[0, p, 2 * kt + 1]], axis=1)
            ck2 = jnp.concatenate([jnp.broadcast_to(ck[0:1, :], (tq, tq)),
                                   jnp.broadcast_to(ck[1:2, :], (tq, tq))], axis=0)
            s = _dot_nt(q2[p], k) + (cq[p] - ck2)
            if masked:
                causal = _lane((tq, tq)) <= _row((tq, tq))
                s = jnp.where(jnp.concatenate([causal, causal], axis=0), s, NEG)
            m, l, acc = _softmax_step2(s, m_scr[p], l_scr[p], acc_scr[p], v)
            m_scr[p] = m
            l_scr[p] = l
            acc_scr[p] = acc

    def body(kt, carry):
        tile(kt, False)
        return carry

    lax.fori_loop(0, qi, body, 0)
    tile(qi, True)
    lo = _lane((tq, LANES)) < HD
    for p in pairs:
        o = _softmax_finish(l_scr[p], acc_scr[p])
        o_ref[0, :, p * LANES:(p + 1) * LANES] = jnp.where(lo, o[0:tq], o[tq:2 * tq])


def _fox_prompt(bqal, fkv, ccol, crow, b, t):
    tq = 2 * LANES
    scratch = pltpu.VMEM((H_B // 2, 2 * tq, LANES), f32)
    return pl.pallas_call(
        _fox_prompt_kernel,
        grid=(b, t // tq),
        in_specs=[pl.BlockSpec((1, tq, H_B * LANES), lambda i, j: (i, j, 0)),
                  pl.BlockSpec((1, t, 2 * LANES), lambda i, j: (i, 0, 0)),
                  pl.BlockSpec((1, t, 2 * LANES), lambda i, j: (i, 0, 1)),
                  pl.BlockSpec((1, 2, tq, LANES), lambda i, j: (i, 0, j, 0)),
                  pl.BlockSpec((1, 2, t // LANES, 8, LANES), lambda i, j: (i, 0, 0, 0, 0))],
        out_specs=pl.BlockSpec((1, tq, H_B * HD), lambda i, j: (i, j, 0)),
        out_shape=jax.ShapeDtypeStruct((b, t, H_B * HD), f32),
        scratch_shapes=[scratch, scratch, scratch],
        compiler_params=_cparams(("parallel", "parallel")),
        name="fox_prompt",
    )(bqal, fkv, fkv, ccol, crow)


def _top_n_mask(score, n):
    lane = _lane(score.shape).astype(f32)
    sel = jnp.zeros(score.shape, f32)
    for _ in range(n):
        m = jnp.broadcast_to(jnp.max(score, axis=1, keepdims=True), score.shape)
        first = jnp.min(jnp.where(score == m, lane, float(LANES)), axis=1, keepdims=True)
        pick = lane == jnp.broadcast_to(first, score.shape)
        sel = jnp.where(pick, 1.0, sel)
        score = jnp.where(pick, -2.0, score)
    return sel


def _masked_softmax_rows(s, valid):
    s = jnp.where(valid, s, NEG)
    m = jnp.broadcast_to(jnp.max(s, axis=1, keepdims=True), s.shape)
    m = jnp.where(m > 0.5 * NEG, m, 0.0)
    p = jnp.where(valid, jnp.exp(s - m), 0.0)
    den = jnp.broadcast_to(jnp.sum(p, axis=1, keepdims=True), s.shape)
    return p / jnp.where(den > 0, den, 1.0)


def _select_blocks(imp, cur2):
    return _select_pairs(imp + pltpu.roll(imp, LANES - 1, 1), cur2)


def _select_pairs(imp2, cur2, top_fn=_top_n_mask):
    lane = _lane(imp2.shape)
    is_blk = (lane & 1) == 0
    valid = is_blk & (lane <= cur2)
    forced = (lane == 0) | (lane == cur2) | (lane == cur2 - 2)
    score = jnp.where(valid, imp2 + jnp.where(forced, FORCE_BONUS, 0.0), -1.0)
    return top_fn(score, TOP_N)


def _top_n_rank(score, n):
    sq = (LANES, LANES)
    i, j = _row(sq), _lane(sq)

    def one(row):
        r = jnp.broadcast_to(row, sq)
        c = jnp.sum(jnp.where(i == j, r, 0.0), axis=1, keepdims=True)
        beats = (c > r) | ((c == r) & (i < j))
        rank = jnp.sum(jnp.where(beats, 1.0, 0.0), axis=0, keepdims=True)
        return jnp.where(rank < n, 1.0, 0.0)

    first = _row(score.shape) < GQA_A
    return jnp.where(first, one(score[0:1]), one(score[GQA_A:GQA_A + 1]))


def _nsa_prompt_kernel(q_ref, kvb_ref, slc_ref, win_ref, misc_ref, exp_ref, blkend_ref,
                       o_ref, m_scr, l_scr, acc_scr):
    tq = q_ref.shape[1]
    tk = 2 * LANES
    qi = pl.program_id(1)
    t_col = qi * tq + _row((tq, 1))
    t8 = jnp.concatenate([t_col] * H_A, axis=0)
    lane = _lane((tq, LANES))
    lo = lane < HD
    key = _lane((tq, tk))
    misc = misc_ref[0]
    nb = kvb_ref.shape[1]
    zpad = jnp.zeros((LANES - nb, LANES), bf16)
    q8 = jnp.concatenate([q_ref[0, :, g * LANES:(g + 1) * LANES] for g in range(H_A)], axis=0)

    kb = jnp.concatenate([kvb_ref[0, :, 0:LANES].astype(bf16), zpad], axis=0)
    vb = jnp.concatenate([kvb_ref[0, :, LANES:2 * LANES].astype(bf16), zpad], axis=0)
    p = _masked_softmax_rows(_dot_nt(q8, kb), blkend_ref[...] <= t8)
    o_cmp = _dot(p.astype(bf16), vb)
    sel = []
    for kv in range(KV_A):
        imp = p[GQA_A * kv * tq:(GQA_A * kv + 1) * tq]
        for j in range(1, GQA_A):
            imp = imp + p[(GQA_A * kv + j) * tq:(GQA_A * kv + j + 1) * tq]
        sel.append(_select_blocks(imp, (t_col // L_SEL) * 2).astype(bf16))

    def slc_bias(kt):
        causal = (kt * tk + key) <= t_col
        per_kv = [jnp.where((_dot(sel[kv], exp_ref[kt]) > 0.5) & causal, 0.0, NEG) for kv in range(KV_A)]
        return jnp.concatenate([per_kv[g // GQA_A] for g in range(H_A)], axis=0)

    def win_bias(kt):
        d = t_col - (kt * tk + key)
        return jnp.concatenate([jnp.where((d >= 0) & (d < WINDOW), 0.0, NEG)] * H_A, axis=0)

    def flash(kv_ref, kt_lo, kt_hi, bias_fn):
        m_scr[...] = jnp.full(m_scr.shape, NEG, f32)
        l_scr[...] = jnp.zeros(l_scr.shape, f32)
        acc_scr[...] = jnp.zeros(acc_scr.shape, f32)

        def body(kt, carry):
            rows = pl.ds(pl.multiple_of(kt * tk, tk), tk)
            k = kv_ref[0, rows, 0:LANES].astype(bf16)
            v = kv_ref[0, rows, LANES:2 * LANES].astype(bf16)
            s = _dot_nt(q8, k) + bias_fn(kt)
            m, l, acc = _softmax_step2(s, m_scr[...], l_scr[...], acc_scr[...], v)
            m_scr[...] = m
            l_scr[...] = l
            acc_scr[...] = acc
            return carry

        lax.fori_loop(kt_lo, kt_hi, body, 0)
        return _softmax_finish(l_scr[...], acc_scr[...])

    kt_end = ((qi + 1) * tq - 1) // tk + 1
    o_slc = flash(slc_ref, 0, kt_end, slc_bias)
    o_win = flash(win_ref, jnp.maximum(qi * tq - WINDOW, 0) // tk, kt_end, win_bias)

    heads = []
    for g in range(H_A):
        rows = slice(g * tq, (g + 1) * tq)
        o = (misc[:, 3 * g:3 * g + 1] * o_cmp[rows] + misc[:, 3 * g + 1:3 * g + 2] * o_slc[rows]
             + misc[:, 3 * g + 2:3 * g + 3] * o_win[rows])
        heads.append(o if (g % 2) == g // GQA_A else pltpu.roll(o, HD, 1))
    for u in range(H_A // 2):
        o_ref[0, :, u * LANES:(u + 1) * LANES] = jnp.where(lo, heads[2 * u], heads[2 * u + 1])


def _nsa_prompt(qal, kvb, slc, win, misc3, consts, b, t, tq):
    nb = t // L_CMP
    tk = 2 * LANES
    return pl.pallas_call(
        _nsa_prompt_kernel,
        grid=(b, t // tq),
        in_specs=[pl.BlockSpec((1, tq, H_A * LANES), lambda i, j: (i, j, 0)),
                  pl.BlockSpec((1, nb, 2 * LANES), lambda i, j: (i, 0, 0)),
                  pl.BlockSpec((1, t, 2 * LANES), lambda i, j: (i, 0, 0)),
                  pl.BlockSpec((1, t, 2 * LANES), lambda i, j: (i, 0, 0)),
                  pl.BlockSpec((1, tq, LANES), lambda i, j: (i, j, 0)),
                  pl.BlockSpec((t // tk, LANES, tk), lambda i, j: (0, 0, 0)),
                  pl.BlockSpec((1, LANES), lambda i, j: (0, 0))],
        out_specs=pl.BlockSpec((1, tq, H_A * HD), lambda i, j: (i, j, 0)),
        out_shape=jax.ShapeDtypeStruct((b, t, H_A * HD), f32),
        scratch_shapes=[pltpu.VMEM((H_A * tq, LANES), f32), pltpu.VMEM((H_A * tq, LANES), f32),
                        pltpu.VMEM((H_A * tq, LANES), f32)],
        compiler_params=_cparams(("parallel", "parallel")),
        name="nsa_prompt",
    )(qal, kvb, slc, win, misc3, consts["expand"], consts["blkend"])


def _rope_tables(pos):
    half = HD // 2
    inv = ROPE_THETA ** (-jnp.arange(half, dtype=f32) / half)
    ang = pos.astype(f32)[:, None] * inv[None, :]
    cos, sin = jnp.cos(ang), jnp.sin(ang)
    return (jnp.concatenate([cos, cos, cos, cos], axis=-1),
            jnp.concatenate([-sin, sin, -sin, sin], axis=-1))


def _constants(t, tq):
    lane = np.arange(LANES)
    bd = ((lane[:, None] // HD) == (lane[None, :] // HD)).astype(np.float32) / HD
    ltri = (lane[None, :] <= lane[:, None]).astype(np.float32)
    key = np.arange(t).reshape(t // tq, 1, tq)
    expand = ((lane[None, :, None] % 2 == 0) & (key // L_SEL == lane[None, :, None] // 2)).astype(np.float32)
    blkend = np.where(lane < t // L_CMP, (lane + 1) * L_CMP - 1, 2 ** 30).astype(np.int32)[None, :]
    log_g = jnp.log1p(-jnp.exp2(-5.0 - jnp.arange(H_C, dtype=f32)))
    i = jnp.arange(RET_CHUNK, dtype=f32)
    diff = i[:, None] - i[None, :]
    dmask = jnp.where(diff >= 0, jnp.exp(jnp.maximum(diff, 0.0) * log_g[:, None, None]), 0.0)
    xi = jnp.exp((i + 1.0)[None, :] * log_g[:, None])
    zeta = jnp.exp((RET_CHUNK - 1.0 - i)[None, :] * log_g[:, None])
    g_c = jnp.exp(RET_CHUNK * log_g)
    by_lane = lambda a: jnp.repeat(a.reshape(2, 2, -1).transpose(0, 2, 1), HD, axis=-1)
    return {"bd": jnp.asarray(bd, bf16), "pool8": _pool_matrix(8, 2 * PAGE),
            "ltri": jnp.asarray(ltri, bf16),
            "expand": jnp.asarray(expand, bf16), "blkend": jnp.asarray(blkend),
            "ret_dmask": dmask, "ret_xi": by_lane(xi), "ret_zeta": by_lane(zeta),
            "ret_gc": by_lane(g_c[:, None]), "log_g": log_g}


def _pack_w_in(wt):
    off = np.cumsum([0, 512, 128, 128, 128, 128, 128, 128, 24, 512, 256, 256, 256, 4, 256, 256, 256, 256, 256])
    seg = lambda i: wt[off[i]:off[i + 1], :]
    (a_q, a_kc, a_vc, a_ks, a_vs, a_kw, a_vw, a_g, a_gate,
     b_q, b_k, b_v, b_f, b_gate, c_q, c_k, c_v, c_gate) = [seg(i) for i in range(18)]
    pad = jnp.zeros((LANES - MISC_G - H_B, wt.shape[1]), wt.dtype)
    return jnp.concatenate([a_q, a_kc, a_vc, a_ks, a_vs, a_kw, a_vw, a_g, b_f, pad,
                            a_gate, b_gate, c_gate, b_q, b_k, b_v, c_q, c_k, c_v], axis=0).astype(bf16)


def _layer_params(l, consts, norm_w, w_in, fox_bf, nsa_q_norm, nsa_k_norm, fox_q_norm, fox_k_norm,
                  ret_gn_w, w_out):
    two = lambda g: jnp.concatenate([g, g])[None, :]
    gains = jnp.concatenate([two(nsa_q_norm[l]), two(nsa_k_norm[l, 0]), two(nsa_k_norm[l, 1]),
                             two(nsa_k_norm[l, 2]), two(fox_q_norm[l]), two(fox_k_norm[l]),
                             jnp.zeros((2, LANES), f32)], axis=0)
    bf = jnp.zeros((1, LANES), f32).at[0, MISC_G:MISC_G + H_B].set(fox_bf[l])
    lw = dict(consts)
    lw.update(norm_w=norm_w[l][None, :], w_pack=_pack_w_in(jnp.transpose(w_in, (2, 0, 1))[:, l, :]), gains=gains, bf=bf,
              ret_gn_w=ret_gn_w[l][None, :], w_out=w_out[l].astype(bf16))
    return lw


def _prompt_layer(x, lw, consts, cos_t, sin_t):
    b, t, d = x.shape
    tq = 2 * LANES
    x2 = x.reshape(b * t, d)
    qal, cmp_, slc, win, misc, gate, bqal, fkv, cqkv, kvb = _proj(x2, lw, cos_t, sin_t, 256)
    r3 = lambda a: a.reshape(b, t, a.shape[-1])
    misc3 = r3(misc)
    o_a = _nsa_prompt(r3(qal), kvb.reshape(b, t // L_CMP, 2 * LANES), r3(slc), r3(win), misc3,
                      consts, b, t, tq)
    ccol, crow = _fox_prep(misc3, consts["ltri"])
    o_b = _fox_prompt(r3(bqal), r3(fkv), ccol, crow, b, t)
    o_c, r_new = _ret_prompt(r3(cqkv), lw, b, t)
    y = _merge(x2, o_a.reshape(b * t, -1), o_b.reshape(b * t, -1), o_c.reshape(b * t, -1), gate,
               lw["w_out"], 256)
    wk = min(WINDOW, t)
    states = (cmp_.reshape(b, t, 2, KV_A, HD), slc.reshape(b, t, 2, KV_A, HD),
              fkv.reshape(b, t, 2, H_B, HD), misc3[:, :, MISC_G:MISC_G + H_B],
              r3(win)[:, t - wk:].reshape(b, wk, 2, KV_A, HD), r_new)
    return y.reshape(b, t, d), states


def _head_col(vec_row, shape, base, stride):
    pick = _lane(shape) == base + stride * _row(shape)
    return jnp.sum(jnp.where(pick, jnp.broadcast_to(vec_row, shape), 0.0), axis=1, keepdims=True)


def _seg_mean(x, seg):
    lane = _lane(x.shape)
    sh = seg // 2
    while sh >= 1:
        x = x + jnp.where((lane & sh) != 0, pltpu.roll(x, sh, 1), pltpu.roll(x, LANES - sh, 1))
        sh //= 2
    return x * (1.0 / seg)


def _tree(op, xs):
    xs = list(xs)
    while len(xs) > 1:
        xs = [op(xs[i], xs[i + 1]) if i + 1 < len(xs) else xs[i] for i in range(0, len(xs), 2)]
    return xs[0]


SEQ_PER_STEP = 2


def _seq_views(refs, s, lead_axis=0):
    one = pl.ds(s, 1)
    return [r.at[one] if lead_axis == 0 else r.at[:, one] for r in refs]


def _dec_nsa_kernel(pt_ref, q_ref, slc_new_ref, win_new_ref, misc_ref, win_ref, *rest, n_pages):
    o_ref = rest[2 * n_pages * SEQ_PER_STEP]
    for s in range(SEQ_PER_STEP):
        pages = rest[2 * n_pages * s:2 * n_pages * (s + 1)]
        q, slc_new, win_new, misc, o = _seq_views([q_ref, slc_new_ref, win_new_ref, misc_ref, o_ref], s)
        win, = _seq_views([win_ref], s, lead_axis=1)
        _dec_nsa_seq(q, slc_new, win_new, misc, win, pages[0:n_pages], pages[n_pages:], o, n_pages)


def _dec_nsa_seq(q_ref, slc_new_ref, win_new_ref, misc_ref, win_ref, cmp_pages, slc_pages, o_ref, n_pages):
    shape = (H_A, LANES)
    lane = _lane(shape)
    row = _row(shape)
    lo = lane < HD
    q8 = q_ref[0]
    qf = q8.astype(f32)
    past = n_pages * PAGE
    cur2 = 2 * (past // L_SEL)
    bf_round = lambda a: a.astype(bf16).astype(f32)
    k_of = lambda pg: pg[0, 0, 0:LANES, :].astype(bf16)
    v_of = lambda pg: pg[0, 0, LANES:2 * LANES, :].astype(bf16)
    row_max = lambda xs: jnp.max(_tree(jnp.maximum, xs), axis=1, keepdims=True)
    row_sum = lambda xs: jnp.sum(_tree(jnp.add, xs), axis=1, keepdims=True)

    sb = [_seg_mean(_dot(q8, k_of(pg)), L_CMP) for pg in cmp_pages]
    m = row_max(sb)
    p = [jnp.exp(x - m) * (1.0 / L_CMP) for x in sb]
    cl = row_sum(p)
    o_cmp = _tree(jnp.add, [_dot_nt(x.astype(bf16), v_of(pg)) for x, pg in zip(p, cmp_pages)]) / cl
    to_prob = L_CMP / cl
    imp2 = jnp.zeros(shape, f32)
    for page, x in enumerate(p):
        e = x * to_prob
        e = jnp.where(row < GQA_A,
                      jnp.sum(e[0:GQA_A], axis=0, keepdims=True),
                      jnp.sum(e[GQA_A:2 * GQA_A], axis=0, keepdims=True))
        pair = e + pltpu.roll(e, LANES - L_CMP, 1)
        first = pltpu.roll(pair, 4 * page, 1) if page else pair
        imp2 = jnp.where(lane == 4 * page, first, imp2)
        imp2 = jnp.where(lane == 4 * page + 2, pltpu.roll(pair, (4 * page + 2 + HD) % LANES, 1), imp2)
    sel = _select_pairs(imp2, cur2, _top_n_rank)

    k_new = bf_round(slc_new_ref[0, :, 0:LANES])
    v_new = bf_round(slc_new_ref[0, :, LANES:2 * LANES])
    s_new = jnp.sum(qf * k_new, axis=1, keepdims=True)
    s = []
    for page, pg in enumerate(slc_pages):
        hit = jnp.where(lo, sel[:, 4 * page:4 * page + 1], sel[:, 4 * page + 2:4 * page + 3]) > 0.5
        s.append(jnp.where(hit, _dot(q8, k_of(pg)), NEG))
    m = jnp.maximum(row_max(s), s_new)
    p = [jnp.exp(x - m) for x in s]
    p_new = jnp.exp(s_new - m)
    acc = _tree(jnp.add, [_dot_nt(x.astype(bf16), v_of(pg)) for x, pg in zip(p, slc_pages)])
    o_slc = (acc + p_new * v_new) / (row_sum(p) + p_new)

    wk = win_ref.shape[3]
    kw = win_ref[0, 0, 0:LANES, :].astype(bf16)
    vw = win_ref[0, 0, LANES:2 * LANES, :].astype(bf16)
    kw_new = bf_round(win_new_ref[0, :, 0:LANES])
    vw_new = bf_round(win_new_ref[0, :, LANES:2 * LANES])
    s = jnp.where(_lane((H_A, wk)) > wk - WINDOW, _dot(q8, kw), NEG)
    s_new = jnp.sum(qf * kw_new, axis=1, keepdims=True)
    m = jnp.maximum(jnp.max(s, axis=1, keepdims=True), s_new)
    pw = jnp.exp(s - m)
    pw_new = jnp.exp(s_new - m)
    o_win = ((_dot_nt(pw.astype(bf16), vw) + pw_new * vw_new)
             / (jnp.sum(pw, axis=1, keepdims=True) + pw_new))

    misc = misc_ref[0]
    o8 = (_head_col(misc, shape, 0, 3) * o_cmp + _head_col(misc, shape, 1, 3) * o_slc
          + _head_col(misc, shape, 2, 3) * o_win)
    o8r = pltpu.roll(o8, HD, 1)
    lo1 = _lane((1, LANES)) < HD
    for u in range(H_A // 2):
        kv_lo = (2 * u) // GQA_A == 0
        even = (o8 if kv_lo else o8r)[2 * u:2 * u + 1, :]
        odd = (o8r if kv_lo else o8)[2 * u + 1:2 * u + 2, :]
        o_ref[0, :, u * LANES:(u + 1) * LANES] = jnp.where(lo1, even, odd)


def _page_specs(block, layer, count, s):
    def spec(g):
        return pl.BlockSpec(block, lambda b, pt: (layer, pt[(b * SEQ_PER_STEP + s) * count + g])
                            + (0,) * (len(block) - 2))
    return [spec(g) for g in range(count)]


def _dec_nsa(pt_flat, q8, slc_new, win_new, misc, win_state, cache_cmp, cache_slc, layer):
    n_seq = q8.shape[0]
    n_pages = pt_flat.shape[0] // n_seq
    wk = win_state.shape[3]
    sub = SEQ_PER_STEP
    seq3 = lambda w: pl.BlockSpec((sub, 1, w), lambda b, pt: (b, 0, 0))
    page_block = (1, 1, 2 * LANES, PAGE)
    page_specs, page_args = [], []
    for s in range(sub):
        page_specs += _page_specs(page_block, layer, n_pages, s) + _page_specs(page_block, layer, n_pages, s)
        page_args += [cache_cmp] * n_pages + [cache_slc] * n_pages
    grid_spec = pltpu.PrefetchScalarGridSpec(
        num_scalar_prefetch=1,
        grid=(n_seq // sub,),
        in_specs=[pl.BlockSpec((sub, H_A, LANES), lambda b, pt: (b, 0, 0)),
                  seq3(2 * LANES), seq3(2 * LANES), seq3(LANES),
                  pl.BlockSpec((1, sub, 2 * LANES, wk), lambda b, pt: (layer, b, 0, 0))] + page_specs,
        out_specs=pl.BlockSpec((sub, 1, H_A * HD), lambda b, pt: (b, 0, 0)))
    return pl.pallas_call(
        functools.partial(_dec_nsa_kernel, n_pages=n_pages),
        grid_spec=grid_spec,
        out_shape=jax.ShapeDtypeStruct((n_seq, 1, H_A * HD), f32),
        compiler_params=_cparams(("parallel",)),
        name="dec_nsa",
    )(pt_flat, q8, slc_new, win_new, misc, win_state, *page_args)


def _lane_cumsum(x):
    lane = _lane(x.shape)
    sh = 1
    while sh < LANES:
        x = x + jnp.where(lane >= sh, pltpu.roll(x, sh, 1), 0.0)
        sh *= 2
    return x


def _dec_fox_kernel(pt_ref, q_ref, kv_new_ref, misc_ref, *rest, n_pages):
    o_ref = rest[2 * n_pages * SEQ_PER_STEP]
    for s in range(SEQ_PER_STEP):
        pages = rest[2 * n_pages * s:2 * n_pages * (s + 1)]
        q, kv_new, misc, o = _seq_views([q_ref, kv_new_ref, misc_ref, o_ref], s)
        _dec_fox_seq(q, kv_new, misc, pages[0:n_pages], pages[n_pages:], o)


def _dec_fox_seq(q_ref, kv_new_ref, misc_ref, kv_pages, lf_pages, o_ref):
    shape = (8, LANES)
    first_pair = _row(shape) < 2
    q8 = q_ref[0]
    unit = lambda kv, u: kv[0, 0, u * LANES:(u + 1) * LANES, :].astype(bf16)

    cs = [_lane_cumsum(lf[0, 0]) for lf in lf_pages]
    carry = jnp.zeros((8, 1), f32)
    s = []
    for kv, c in zip(kv_pages, cs):
        s.append(jnp.where(first_pair, _dot(q8, unit(kv, 0)), _dot(q8, unit(kv, 1))) - (carry + c))
        carry = carry + c[:, LANES - 1:LANES]
    bf_round = lambda a: jnp.broadcast_to(a.astype(bf16).astype(f32), shape)
    new = kv_new_ref[0]
    k_new = jnp.where(first_pair, bf_round(new[:, 0:LANES]), bf_round(new[:, LANES:2 * LANES]))
    v_new = jnp.where(first_pair, bf_round(new[:, 2 * LANES:3 * LANES]), bf_round(new[:, 3 * LANES:4 * LANES]))
    s_new = (jnp.sum(q8.astype(f32) * k_new, axis=1, keepdims=True)
             - (carry + _head_col(misc_ref[0], shape, MISC_G, 1)))
    m = jnp.maximum(jnp.max(_tree(jnp.maximum, s), axis=1, keepdims=True), s_new)
    p = [jnp.exp(x - m) for x in s]
    p_new = jnp.exp(s_new - m)
    den = jnp.sum(_tree(jnp.add, p), axis=1, keepdims=True) + p_new
    pb = [x.astype(bf16) for x in p]
    acc0 = _tree(jnp.add, [_dot_nt(x, unit(kv, 2)) for x, kv in zip(pb, kv_pages)])
    acc1 = _tree(jnp.add, [_dot_nt(x, unit(kv, 3)) for x, kv in zip(pb, kv_pages)])
    o8 = (jnp.where(first_pair, acc0, acc1) + p_new * v_new) / den
    lo1 = _lane((1, LANES)) < HD
    for u in range(H_B // 2):
        o_ref[0, :, u * LANES:(u + 1) * LANES] = jnp.where(lo1, o8[2 * u:2 * u + 1, :],
                                                           o8[2 * u + 1:2 * u + 2, :])


def _dec_fox(pt_flat, qb4, fkv_new, misc, cache_fkv, cache_lft, layer):
    n_seq = qb4.shape[0]
    n_pages = pt_flat.shape[0] // n_seq
    sub = SEQ_PER_STEP
    seq3 = lambda w: pl.BlockSpec((sub, 1, w), lambda b, pt: (b, 0, 0))
    page_specs, page_args = [], []
    for s in range(sub):
        page_specs += (_page_specs((1, 1, 4 * LANES, PAGE), layer, n_pages, s)
                       + _page_specs((1, 1, 8, PAGE), layer, n_pages, s))
        page_args += [cache_fkv] * n_pages + [cache_lft] * n_pages
    grid_spec = pltpu.PrefetchScalarGridSpec(
        num_scalar_prefetch=1,
        grid=(n_seq // sub,),
        in_specs=[pl.BlockSpec((sub, 8, LANES), lambda b, pt: (b, 0, 0)), seq3(4 * LANES), seq3(LANES)]
        + page_specs,
        out_specs=pl.BlockSpec((sub, 1, H_B * HD), lambda b, pt: (b, 0, 0)))
    return pl.pallas_call(
        functools.partial(_dec_fox_kernel, n_pages=n_pages),
        grid_spec=grid_spec,
        out_shape=jax.ShapeDtypeStruct((n_seq, 1, H_B * HD), f32),
        compiler_params=_cparams(("parallel",)),
        name="dec_fox",
    )(pt_flat, qb4, fkv_new, misc, *page_args)


def _dec_ret_kernel(q_ref, k_ref, v_ref, r_ref, gam_ref, gnw_ref, o_ref, rn_ref):
    q = q_ref[0].astype(f32)
    k = k_ref[0].astype(f32)
    v = v_ref[0].astype(f32)
    qk = jnp.sum(q * k, axis=1, keepdims=True)
    eye = _row((HD, HD)) == _lane((HD, HD))
    col = lambda a: jnp.sum(jnp.where(eye, jnp.broadcast_to(a, (HD, HD)), 0.0), axis=1, keepdims=True)
    for h in range(H_C):
        hs = slice(h, h + 1)
        r = r_ref[0, 0, h]
        gam = gam_ref[hs, :]
        o = qk[hs, :] * v[hs, :] + jnp.sum(col(q[hs, :]) * r, axis=0, keepdims=True) * gam
        rn_ref[0, h] = r * gam + col(k[hs, :]) * v[hs, :]
        d = o - jnp.mean(o, axis=1, keepdims=True)
        var = jnp.mean(d * d, axis=1, keepdims=True)
        o_ref[0, hs, :] = d * lax.rsqrt(var + EPS) * gnw_ref[hs, :]


def _dec_ret(cq, ck, cv, state_ret, gam, gnw, layer):
    n_seq = cq.shape[0]
    head = pl.BlockSpec((1, H_C, HD), lambda b: (b, 0, 0))
    table = pl.BlockSpec((H_C, HD), lambda b: (0, 0))
    return pl.pallas_call(
        _dec_ret_kernel,
        grid=(n_seq,),
        in_specs=[head, head, head,
                  pl.BlockSpec((1, 1, H_C, HD, HD), lambda b: (layer, b, 0, 0, 0)), table, table],
        out_specs=[head, pl.BlockSpec((1, H_C, HD, HD), lambda b: (b, 0, 0, 0))],
        out_shape=[jax.ShapeDtypeStruct((n_seq, H_C, HD), f32),
                   jax.ShapeDtypeStruct((n_seq, H_C, HD, HD), f32)],
        compiler_params=_cparams(("parallel",)),
        name="dec_ret",
    )(cq, ck, cv, state_ret, gam, gnw)


def _win_shift_kernel(win_ref, new_ref, o_ref):
    n_sub, feat, wk = win_ref.shape[1:]
    sq = (feat, feat)
    eye = _row(sq) == _lane(sq)
    last = _lane((feat, wk)) == wk - 1
    for s in range(n_sub):
        col = jnp.sum(jnp.where(eye, jnp.broadcast_to(new_ref[0, s], sq), 0.0), axis=1, keepdims=True)
        o_ref[0, s] = jnp.where(last, col, pltpu.roll(win_ref[0, s], wk - 1, 1))


def _win_shift(win_t, new_rows):
    depth, n_seq, feat, wk = win_t.shape
    n_sub = 8
    return pl.pallas_call(
        _win_shift_kernel,
        grid=(depth, n_seq // n_sub),
        in_specs=[pl.BlockSpec((1, n_sub, feat, wk), lambda l, i: (l, i, 0, 0)),
                  pl.BlockSpec((1, n_sub, 1, feat), lambda l, i: (l, i, 0, 0))],
        out_specs=pl.BlockSpec((1, n_sub, feat, wk), lambda l, i: (l, i, 0, 0)),
        out_shape=jax.ShapeDtypeStruct(win_t.shape, f32),
        compiler_params=_cparams(("parallel", "parallel")),
        name="win_shift",
    )(win_t, new_rows)


def _sample_layer(x, layer, lw, consts, caches, pt_flat, cos_t, sin_t):
    n, _, d = x.shape
    x2 = x.reshape(n, d)
    qal, cmp_, slc, win, misc, gate, bqal, fkv, cqkv, _ = _proj(x2, lw, cos_t, sin_t, n)
    row3 = lambda a: a[:, None, :]
    o_a8 = _dec_nsa(pt_flat, qal.reshape(n, H_A, LANES), row3(slc), row3(win), row3(misc),
                    caches["win"], caches["cmp"], caches["slc"], layer)
    qb8 = jnp.pad(bqal.reshape(n, H_B, LANES), ((0, 0), (0, 8 - H_B), (0, 0)))
    o_b8 = _dec_fox(pt_flat, qb8, row3(fkv), row3(misc),
                    caches["fkv"], caches["lft"], layer)
    heads = lambda u: cqkv[:, 2 * u * LANES:(2 * u + 2) * LANES].reshape(n, H_C, HD).astype(f32)
    gam = jnp.broadcast_to(jnp.exp(consts["log_g"])[:, None], (H_C, HD))
    o_c, r_new = _dec_ret(heads(0), heads(1), heads(2), caches["ret"], gam,
                          lw["ret_gn_w"].reshape(H_C, HD), layer)
    y = _merge(x2, o_a8.reshape(n, H_A * HD), o_b8.reshape(n, H_B * HD), o_c.reshape(n, H_C * HD), gate,
               lw["w_out"], n)
    states = (cmp_.reshape(n, 1, 2, KV_A, HD), slc.reshape(n, 1, 2, KV_A, HD),
              fkv.reshape(n, 1, 2, H_B, HD), misc[:, None, MISC_G:MISC_G + H_B], win, r_new)
    return y.reshape(n, 1, d), states


def kernel(x_prompt, x_sample, cache_nsa_cmp, cache_nsa_slc, cache_fox_kv, cache_fox_logf,
           state_nsa_win, state_ret, page_table, norm_w, w_in, fox_bf, nsa_q_norm, nsa_k_norm,
           fox_q_norm, fox_k_norm, ret_gn_w, w_out):
    depth = norm_w.shape[0]
    b, t, _ = x_prompt.shape
    n_seq, n_pages = page_table.shape
    assert x_sample.shape[1] == 1 and t % 256 == 0 and n_pages % 2 == 0
    past = n_pages * PAGE
    consts = _constants(t, 2 * LANES)
    cos_p, sin_p = _rope_tables(jnp.arange(t, dtype=jnp.int32))
    cos_s, sin_s = _rope_tables(jnp.full((n_seq,), past, dtype=jnp.int32))
    n_phys = cache_nsa_cmp.shape[1]
    feat_major = lambda a, lead: jnp.transpose(a, (0, 1, 3, 4, 5, 2)).reshape(depth, lead, -1, a.shape[2])
    caches = {"cmp": feat_major(cache_nsa_cmp, n_phys), "slc": feat_major(cache_nsa_slc, n_phys),
              "fkv": feat_major(cache_fox_kv, n_phys),
              "lft": jnp.pad(jnp.swapaxes(cache_fox_logf, 2, 3), ((0, 0), (0, 0), (0, 8 - H_B), (0, 0))),
              "win": feat_major(state_nsa_win, n_seq),
              "ret": state_ret}
    pt_flat = page_table.reshape(-1)
    yp, ys = x_prompt, x_sample
    p_states, s_states = [], []
    for l in range(depth):
        lw = _layer_params(l, consts, norm_w, w_in, fox_bf, nsa_q_norm, nsa_k_norm, fox_q_norm,
                           fox_k_norm, ret_gn_w, w_out)
        yp, ps = _prompt_layer(yp, lw, consts, cos_p, sin_p)
        ys, ss = _sample_layer(ys, l, lw, consts, caches, pt_flat, cos_s, sin_s)
        p_states.append(ps)
        s_states.append(ss)
    stk = lambda states, i: jnp.stack([st[i] for st in states], axis=0)
    win_t = caches["win"]
    wk = win_t.shape[3]
    new_win_t = _win_shift(win_t, stk(s_states, 4)[:, :, None, :])
    new_win = jnp.transpose(new_win_t.reshape(depth, n_seq, 2, KV_A, HD, wk), (0, 1, 5, 2, 3, 4))
    out = [yp, ys]
    for i in range(6):
        out += [stk(p_states, i), new_win if i == 4 else stk(s_states, i)]
    return tuple(out)
```

```python
import functools

import numpy as np
import jax
import jax.numpy as jnp
from jax import lax
from jax.experimental import pallas as pl
from jax.experimental.pallas import tpu as pltpu

f32 = jnp.float32
bf16 = jnp.bfloat16

HD = 64
H_A, KV_A, GQA_A = 8, 2, 4
H_B, H_C = 4, 4
L_CMP, L_SEL, TOP_N, WINDOW = 32, 64, 8, 512
PAGE = 128
RET_CHUNK = 128
ROPE_THETA = 10000.0
EPS = 1e-6
SCALE = HD ** -0.5
FORCE_BONUS = 1e3
NEG = -1e30

LANES = 128
VMEM_LIMIT = 48 * 1024 * 1024

C_AQ, C_CMP, C_SLC, C_WIN, C_MISC, C_GATE, C_BQ, C_BKV, C_CQKV, D_PACK = (
    0, 512, 768, 1024, 1280, 1408, 2432, 2688, 3200, 3968)
MISC_G = 3 * H_A


def _cparams(sem):
    return pltpu.CompilerParams(dimension_semantics=sem, vmem_limit_bytes=VMEM_LIMIT)


def _dot(a, b):
    return jnp.dot(a, b, preferred_element_type=f32)


def _dot_nt(a, b):
    return lax.dot_general(a, b, (((1,), (1,)), ((), ())), preferred_element_type=f32)


def _split(x, n):
    parts = []
    r = x
    for i in range(n):
        h = r.astype(bf16)
        parts.append(h)
        if i + 1 < n:
            r = r - h.astype(f32)
    return parts


def _dot_split(x, w, n=2):
    acc = None
    for h in _split(x, n):
        d = _dot(h, w)
        acc = d if acc is None else acc + d
    return acc


def _dot_split_l(w, x, n=2):
    acc = None
    for h in _split(x, n):
        d = _dot(w, h)
        acc = d if acc is None else acc + d
    return acc


def _lane(shape):
    return lax.broadcasted_iota(jnp.int32, shape, len(shape) - 1)


def _row(shape):
    return lax.broadcasted_iota(jnp.int32, shape, len(shape) - 2)


def _swap_half_heads(y):
    lane = _lane(y.shape)
    return jnp.where((lane & 32) == 0, pltpu.roll(y, 96, 1), pltpu.roll(y, 32, 1))


def _head_rms(z, bd, g):
    ms = _dot_split(z * z, bd)
    return z * lax.rsqrt(ms + EPS) * g


def _rope(y, cos, sin):
    return y * cos + _swap_half_heads(y) * sin


def _sigmoid(z):
    return 1.0 / (1.0 + jnp.exp(-z))


def _proj_kernel(x_ref, nw_ref, w_ref, bd_ref, pool_ref, cos_ref, sin_ref, gn_ref, bf_ref,
                 qal_ref, cmp_ref, slc_ref, win_ref, misc_ref, gate_ref, bqal_ref, fkv_ref,
                 cqkv_ref, kvb_ref):
    x = x_ref[...]
    ms = jnp.mean(x * x, axis=-1, keepdims=True)
    xn = (x * lax.rsqrt(ms + EPS) * nw_ref[...]).astype(bf16)
    bd = bd_ref[...]
    cos = cos_ref[...]
    sin = sin_ref[...]
    lane = _lane((x.shape[0], LANES))
    lo = lane < HD

    def z_of(c0, width):
        return _dot_nt(xn, w_ref[c0:c0 + width, :])

    def unit(z, u):
        return z[:, u * LANES:(u + 1) * LANES]

    z = z_of(C_AQ, H_A * HD)
    g_aq = gn_ref[0:1, :]
    for u in range(H_A // 2):
        y = _rope(_head_rms(unit(z, u), bd, g_aq), cos, sin) * SCALE
        yr = pltpu.roll(y, HD, 1)
        kv_lo = (2 * u) // GQA_A == 0
        keep = lo if kv_lo else jnp.logical_not(lo)
        even, odd = (y, yr) if kv_lo else (yr, y)
        qal_ref[:, (2 * u) * LANES:(2 * u + 1) * LANES] = jnp.where(keep, even, 0.0).astype(bf16)
        qal_ref[:, (2 * u + 1) * LANES:(2 * u + 2) * LANES] = jnp.where(keep, odd, 0.0).astype(bf16)

    for i, (c0, o_ref) in enumerate(((C_CMP, cmp_ref), (C_SLC, slc_ref), (C_WIN, win_ref))):
        z = z_of(c0, 2 * LANES)
        k = _rope(_head_rms(unit(z, 0), bd, gn_ref[1 + i:2 + i, :]), cos, sin)
        v = unit(z, 1)
        o_ref[:, 0:LANES] = k
        o_ref[:, LANES:2 * LANES] = v
        if i == 0:
            pool = pool_ref[...]
            kvb_ref[:, 0:LANES] = _dot_split_l(pool, k)
            kvb_ref[:, LANES:2 * LANES] = _dot_split_l(pool, v)

    z = z_of(C_MISC, LANES)
    zf = z + bf_ref[...]
    logf = jnp.minimum(zf, 0.0) - jnp.log1p(jnp.exp(-jnp.abs(zf)))
    misc_ref[...] = jnp.where(lane < MISC_G, _sigmoid(z), jnp.where(lane < MISC_G + H_B, logf, 0.0))

    for c in range(0, 1024, 512):
        z = z_of(C_GATE + c, 512)
        gate_ref[:, c:c + 512] = z * _sigmoid(z)

    z = z_of(C_BQ, H_B * HD)
    for u in range(H_B // 2):
        y = _head_rms(unit(z, u), bd, gn_ref[4:5, :]) * SCALE
        bqal_ref[:, (2 * u) * LANES:(2 * u + 1) * LANES] = jnp.where(lo, y, 0.0).astype(bf16)
        bqal_ref[:, (2 * u + 1) * LANES:(2 * u + 2) * LANES] = jnp.where(lo, 0.0, y).astype(bf16)
    z = z_of(C_BKV, 2 * H_B * HD)
    for u in range(2):
        fkv_ref[:, u * LANES:(u + 1) * LANES] = _head_rms(unit(z, u), bd, gn_ref[5:6, :])
    fkv_ref[:, 2 * LANES:4 * LANES] = z[:, 2 * LANES:4 * LANES]

    z = z_of(C_CQKV, 3 * H_C * HD)
    for u in range(2):
        cqkv_ref[:, u * LANES:(u + 1) * LANES] = _rope(unit(z, u), cos, sin).astype(bf16)
        cqkv_ref[:, (2 + u) * LANES:(3 + u) * LANES] = (_rope(unit(z, 2 + u), cos, sin) * SCALE).astype(bf16)
    cqkv_ref[:, 4 * LANES:6 * LANES] = z[:, 4 * LANES:6 * LANES].astype(bf16)


def _pool_matrix(n_out, n_rows):
    r = np.arange(n_rows)[None, :] // L_CMP == np.arange(n_out)[:, None]
    return jnp.asarray(r.astype(np.float32) / L_CMP, bf16)


def _proj(x2, lw, cos_t, sin_t, tm):
    n = x2.shape[0]
    nt = cos_t.shape[0] // tm
    const = lambda i: (0, 0)
    rows = lambda i: (i, 0)
    outs = [(H_A * LANES, bf16), (2 * LANES, f32), (2 * LANES, f32), (2 * LANES, f32), (LANES, f32),
            (1024, f32), (H_B * LANES, bf16), (4 * LANES, f32), (6 * LANES, bf16)]
    out_shape = [jax.ShapeDtypeStruct((n, w), dt) for w, dt in outs]
    out_specs = [pl.BlockSpec((tm, w), rows) for w, _ in outs]
    out_shape.append(jax.ShapeDtypeStruct((n // L_CMP, 2 * LANES), f32))
    out_specs.append(pl.BlockSpec((tm // L_CMP, 2 * LANES), rows))
    return pl.pallas_call(
        _proj_kernel,
        grid=(n // tm,),
        in_specs=[pl.BlockSpec((tm, 1024), rows),
                  pl.BlockSpec((1, 1024), const),
                  pl.BlockSpec((D_PACK, 1024), const),
                  pl.BlockSpec((LANES, LANES), const),
                  pl.BlockSpec((tm // L_CMP, tm), const),
                  pl.BlockSpec((tm, LANES), lambda i: (i % nt, 0)),
                  pl.BlockSpec((tm, LANES), lambda i: (i % nt, 0)),
                  pl.BlockSpec((8, LANES), const),
                  pl.BlockSpec((1, LANES), const)],
        out_specs=out_specs,
        out_shape=out_shape,
        compiler_params=_cparams(("parallel",)),
        name="proj",
    )(x2, lw["norm_w"], lw["w_pack"], lw["bd"], _pool_matrix(tm // L_CMP, tm), cos_t, sin_t,
      lw["gains"], lw["bf"])


def _merge_kernel(x_ref, oa_ref, ob_ref, oc_ref, gate_ref, w_ref, y_ref):
    g = gate_ref[...]
    wa, wb = H_A * HD, H_A * HD + H_B * HD
    acc = _dot((oa_ref[...] * g[:, 0:wa]).astype(bf16), w_ref[0:wa, :])
    acc += _dot((ob_ref[...] * g[:, wa:wb]).astype(bf16), w_ref[wa:wb, :])
    acc += _dot((oc_ref[...] * g[:, wb:]).astype(bf16), w_ref[wb:, :])
    y_ref[...] = x_ref[...] + acc


def _merge(x2, oa, ob, oc, gate, w_out, tm):
    n = x2.shape[0]
    rows = lambda i: (i, 0)
    return pl.pallas_call(
        _merge_kernel,
        grid=(n // tm,),
        in_specs=[pl.BlockSpec((tm, 1024), rows), pl.BlockSpec((tm, 512), rows),
                  pl.BlockSpec((tm, 256), rows), pl.BlockSpec((tm, 256), rows),
                  pl.BlockSpec((tm, 1024), rows), pl.BlockSpec((1024, 1024), lambda i: (0, 0))],
        out_specs=pl.BlockSpec((tm, 1024), rows),
        out_shape=jax.ShapeDtypeStruct((n, 1024), f32),
        compiler_params=_cparams(("parallel",)),
        name="merge",
    )(x2, oa, ob, oc, gate, w_out)


def _ret_prompt_kernel(q_ref, k_ref, v_ref, dm_ref, xi_ref, zeta_ref, gc_ref, gnw_ref, bd_ref,
                       o_ref, r_ref):
    c = RET_CHUNK
    n_chunks = q_ref.shape[1] // c
    pairs = range(H_C // 2)
    lo = _lane((c, LANES)) < HD
    same_head = (_row((LANES, LANES)) < HD) == (_lane((LANES, LANES)) < HD)
    bd = bd_ref[...]

    def step(i, rs):
        rows = pl.ds(pl.multiple_of(i * c, c), c)
        out = []
        for p, r in zip(pairs, rs):
            lanes = slice(p * LANES, (p + 1) * LANES)
            q = q_ref[0, rows, lanes]
            k = k_ref[0, rows, lanes]
            v = v_ref[0, rows, lanes]
            zero = jnp.zeros_like(q)
            a0 = _dot_nt(jnp.where(lo, q, zero), k) * dm_ref[2 * p]
            a1 = _dot_nt(jnp.where(lo, zero, q), k) * dm_ref[2 * p + 1]
            o = jnp.where(lo, _dot(a0.astype(bf16), v), _dot(a1.astype(bf16), v))
            o = o + _dot(q, r.astype(bf16)) * xi_ref[p]
            kz_t = (k.astype(f32) * zeta_ref[p]).T.astype(bf16)
            out.append(r * gc_ref[p] + jnp.where(same_head, _dot(kz_t, v), 0.0))
            mu = _dot_split(o, bd)
            d = o - mu
            var = _dot_split(d * d, bd)
            o_ref[0, rows, lanes] = d * lax.rsqrt(var + EPS) * gnw_ref[:, lanes]
        return tuple(out)

    rs = lax.fori_loop(0, n_chunks, step, tuple(jnp.zeros((LANES, LANES), f32) for _ in pairs))
    for p, r in zip(pairs, rs):
        r_ref[0, 2 * p] = r[0:HD, 0:HD]
        r_ref[0, 2 * p + 1] = pltpu.roll(r, HD, 1)[HD:2 * HD, 0:HD]


def _ret_prompt(cqkv, lw, b, t):
    width = H_C * HD
    blk = lambda u: pl.BlockSpec((1, t, width), lambda i: (i, 0, u))
    whole = lambda a: pl.BlockSpec(a.shape, lambda i: (0,) * a.ndim)
    tables = [lw["ret_dmask"], lw["ret_xi"], lw["ret_zeta"], lw["ret_gc"], lw["ret_gn_w"], lw["bd"]]
    return pl.pallas_call(
        _ret_prompt_kernel,
        grid=(b,),
        in_specs=[blk(0), blk(1), blk(2)] + [whole(a) for a in tables],
        out_specs=[pl.BlockSpec((1, t, width), lambda i: (i, 0, 0)),
                   pl.BlockSpec((1, H_C, HD, HD), lambda i: (i, 0, 0, 0))],
        out_shape=[jax.ShapeDtypeStruct((b, t, width), f32),
                   jax.ShapeDtypeStruct((b, H_C, HD, HD), f32)],
        compiler_params=_cparams(("parallel",)),
        name="ret_prompt",
    )(cqkv, cqkv, cqkv, *tables)


def _fox_prep_kernel(misc_ref, ltri_ref, ccol_ref, crow_ref):
    c = LANES
    n_chunks = misc_ref.shape[1] // c
    ltri = ltri_ref[...]

    def step(i, carry):
        rows = pl.ds(pl.multiple_of(i * c, c), c)
        cs = _dot_split_l(ltri, misc_ref[0, rows, :], 3) + carry
        for p in range(2):
            sh = pltpu.roll(cs, LANES - (MISC_G + 2 * p), 1)
            ccol_ref[0, p, rows, :] = sh
            crow_ref[0, p, i] = sh.T[0:8, :]
        return cs[c - 1:c, :]

    lax.fori_loop(0, n_chunks, step, jnp.zeros((1, LANES), f32))


def _fox_prep(misc3, ltri):
    b, t, _ = misc3.shape
    return pl.pallas_call(
        _fox_prep_kernel,
        grid=(b,),
        in_specs=[pl.BlockSpec((1, t, LANES), lambda i: (i, 0, 0)),
                  pl.BlockSpec((LANES, LANES), lambda i: (0, 0))],
        out_specs=[pl.BlockSpec((1, 2, t, LANES), lambda i: (i, 0, 0, 0)),
                   pl.BlockSpec((1, 2, t // LANES, 8, LANES), lambda i: (i, 0, 0, 0, 0))],
        out_shape=[jax.ShapeDtypeStruct((b, 2, t, LANES), f32),
                   jax.ShapeDtypeStruct((b, 2, t // LANES, 8, LANES), f32)],
        compiler_params=_cparams(("parallel",)),
        name="fox_prep",
    )(misc3, ltri)


def _softmax_step2(s, m, l, acc, v):
    s_a, s_b = s[:, 0:LANES], s[:, LANES:2 * LANES]
    m_new = jnp.maximum(m, jnp.max(jnp.maximum(s_a, s_b), axis=1, keepdims=True))
    alpha = jnp.exp(m - m_new)
    p_a = jnp.exp(s_a - m_new)
    p_b = jnp.exp(s_b - m_new)
    l = alpha * l + (p_a + p_b)
    pv = _dot(p_a.astype(bf16), v[0:LANES]) + _dot(p_b.astype(bf16), v[LANES:2 * LANES])
    return m_new, l, alpha * acc + pv


def _softmax_first2(s, v):
    s_a, s_b = s[:, 0:LANES], s[:, LANES:2 * LANES]
    m = jnp.broadcast_to(jnp.max(jnp.maximum(s_a, s_b), axis=1, keepdims=True), s_a.shape)
    p_a = jnp.exp(s_a - m)
    p_b = jnp.exp(s_b - m)
    pv = _dot(p_a.astype(bf16), v[0:LANES]) + _dot(p_b.astype(bf16), v[LANES:2 * LANES])
    return m, p_a + p_b, pv


def _softmax_finish(l, acc):
    return acc / jnp.sum(l, axis=1, keepdims=True)


def _fox_prompt_kernel(q_ref, k_ref, v_ref, ccol_ref, crow_ref, o_ref, m_scr, l_scr, acc_scr):
    tq = q_ref.shape[1]
    qi = pl.program_id(1)
    pairs = range(H_B // 2)
    q2, cq = [], []
    for p in pairs:
        q2.append(jnp.concatenate([q_ref[0, :, (2 * p) * LANES:(2 * p + 1) * LANES],
                                   q_ref[0, :, (2 * p + 1) * LANES:(2 * p + 2) * LANES]], axis=0))
        cc = ccol_ref[0, p]
        cq.append(jnp.broadcast_to(jnp.concatenate([cc[:, 0:1], cc[:, 1:2]], axis=0), (2 * tq, tq)))
    m_scr[...] = jnp.full(m_scr.shape, NEG, f32)
    l_scr[...] = jnp.zeros(l_scr.shape, f32)
    acc_scr[...] = jnp.zeros(acc_scr.shape, f32)

    def tile(kt, masked):
        rows = pl.ds(pl.multiple_of(kt * tq, tq), tq)
        for p in pairs:
            k = k_ref[0, rows, p * LANES:(p + 1) * LANES].astype(bf16)
            v = v_ref[0, rows, p * LANES:(p + 1) * LANES].astype(bf16)
            ck = jnp.concatenate([crow_ref[0, p, 2 * kt], crow_ref[0, p, 2 * kt + 1]], axis=1)
            ck2 = jnp.concatenate([jnp.broadcast_to(ck[0:1, :], (tq, tq)),
                                   jnp.broadcast_to(ck[1:2, :], (tq, tq))], axis=0)
            s = _dot_nt(q2[p], k) + (cq[p] - ck2)
            if masked:
                causal = _lane((tq, tq)) <= _row((tq, tq))
                s = jnp.where(jnp.concatenate([causal, causal], axis=0), s, NEG)
            m, l, acc = _softmax_step2(s, m_scr[p], l_scr[p], acc_scr[p], v)
            m_scr[p] = m
            l_scr[p] = l
            acc_scr[p] = acc

    def body(kt, carry):
        tile(kt, False)
        return carry

    lax.fori_loop(0, qi, body, 0)
    tile(qi, True)
    lo = _lane((tq, LANES)) < HD
    for p in pairs:
        o = _softmax_finish(l_scr[p], acc_scr[p])
        o_ref[0, :, p * LANES:(p + 1) * LANES] = jnp.where(lo, o[0:tq], o[tq:2 * tq])


def _fox_prompt(bqal, fkv, ccol, crow, b, t):
    tq = 2 * LANES
    scratch = pltpu.VMEM((H_B // 2, 2 * tq, LANES), f32)
    return pl.pallas_call(
        _fox_prompt_kernel,
        grid=(b, t // tq),
        in_specs=[pl.BlockSpec((1, tq, H_B * LANES), lambda i, j: (i, j, 0)),
                  pl.BlockSpec((1, t, 2 * LANES), lambda i, j: (i, 0, 0)),
                  pl.BlockSpec((1, t, 2 * LANES), lambda i, j: (i, 0, 1)),
                  pl.BlockSpec((1, 2, tq, LANES), lambda i, j: (i, 0, j, 0)),
                  pl.BlockSpec((1, 2, t // LANES, 8, LANES), lambda i, j: (i, 0, 0, 0, 0))],
        out_specs=pl.BlockSpec((1, tq, H_B * HD), lambda i, j: (i, j, 0)),
        out_shape=jax.ShapeDtypeStruct((b, t, H_B * HD), f32),
        scratch_shapes=[scratch, scratch, scratch],
        compiler_params=_cparams(("parallel", "parallel")),
        name="fox_prompt",
    )(bqal, fkv, fkv, ccol, crow)


def _top_n_mask(score, n):
    lane = _lane(score.shape).astype(f32)
    sel = jnp.zeros(score.shape, f32)
    for _ in range(n):
        m = jnp.broadcast_to(jnp.max(score, axis=1, keepdims=True), score.shape)
        first = jnp.min(jnp.where(score == m, lane, float(LANES)), axis=1, keepdims=True)
        pick = lane == jnp.broadcast_to(first, score.shape)
        sel = jnp.where(pick, 1.0, sel)
        score = jnp.where(pick, -2.0, score)
    return sel


def _masked_softmax_rows(s, valid):
    s = jnp.where(valid, s, NEG)
    m = jnp.broadcast_to(jnp.max(s, axis=1, keepdims=True), s.shape)
    m = jnp.where(m > 0.5 * NEG, m, 0.0)
    p = jnp.where(valid, jnp.exp(s - m), 0.0)
    den = jnp.broadcast_to(jnp.sum(p, axis=1, keepdims=True), s.shape)
    return p / jnp.where(den > 0, den, 1.0)


def _select_blocks(imp, cur2):
    return _select_pairs(imp + pltpu.roll(imp, LANES - 1, 1), cur2)


def _select_pairs(imp2, cur2, top_fn=_top_n_mask):
    lane = _lane(imp2.shape)
    is_blk = (lane & 1) == 0
    valid = is_blk & (lane <= cur2)
    forced = (lane == 0) | (lane == cur2) | (lane == cur2 - 2)
    score = jnp.where(valid, imp2 + jnp.where(forced, FORCE_BONUS, 0.0), -1.0)
    return top_fn(score, TOP_N)


def _top_n_rank(score, n):
    sq = (LANES, LANES)
    i, j = _row(sq), _lane(sq)

    def one(row):
        r = jnp.broadcast_to(row, sq)
        c = jnp.sum(jnp.where(i == j, r, 0.0), axis=1, keepdims=True)
        beats = (c > r) | ((c == r) & (i < j))
        rank = jnp.sum(jnp.where(beats, 1.0, 0.0), axis=0, keepdims=True)
        return jnp.where(rank < n, 1.0, 0.0)

    first = _row(score.shape) < GQA_A
    return jnp.where(first, one(score[0:1]), one(score[GQA_A:GQA_A + 1]))


def _nsa_prompt_kernel(q_ref, kvb_ref, slc_ref, win_ref, misc_ref, exp_ref, blkend_ref,
                       o_ref, m_scr, l_scr, acc_scr):
    tq = q_ref.shape[1]
    tk = 2 * LANES
    qi = pl.program_id(1)
    t_col = qi * tq + _row((tq, 1))
    t8 = jnp.concatenate([t_col] * H_A, axis=0)
    lane = _lane((tq, LANES))
    lo = lane < HD
    key = _lane((tq, tk))
    misc = misc_ref[0]
    nb = kvb_ref.shape[1]
    zpad = jnp.zeros((LANES - nb, LANES), bf16)
    q8 = jnp.concatenate([q_ref[0, :, g * LANES:(g + 1) * LANES] for g in range(H_A)], axis=0)

    kb = jnp.concatenate([kvb_ref[0, :, 0:LANES].astype(bf16), zpad], axis=0)
    vb = jnp.concatenate([kvb_ref[0, :, LANES:2 * LANES].astype(bf16), zpad], axis=0)
    p = _masked_softmax_rows(_dot_nt(q8, kb), blkend_ref[...] <= t8)
    o_cmp = _dot(p.astype(bf16), vb)
    sel = []
    for kv in range(KV_A):
        imp = p[GQA_A * kv * tq:(GQA_A * kv + 1) * tq]
        for j in range(1, GQA_A):
            imp = imp + p[(GQA_A * kv + j) * tq:(GQA_A * kv + j + 1) * tq]
        sel.append(_select_blocks(imp, (t_col // L_SEL) * 2).astype(bf16))

    def slc_bias(kt):
        causal = (kt * tk + key) <= t_col
        per_kv = [jnp.where((_dot(sel[kv], exp_ref[kt]) > 0.5) & causal, 0.0, NEG) for kv in range(KV_A)]
        return jnp.concatenate([per_kv[g // GQA_A] for g in range(H_A)], axis=0)

    def win_bias(kt):
        d = t_col - (kt * tk + key)
        return jnp.concatenate([jnp.where((d >= 0) & (d < WINDOW), 0.0, NEG)] * H_A, axis=0)

    def flash(kv_ref, kt_lo, kt_hi, bias_fn):
        def tile(kt, first):
            rows = pl.ds(pl.multiple_of(kt * tk, tk), tk)
            k = kv_ref[0, rows, 0:LANES].astype(bf16)
            v = kv_ref[0, rows, LANES:2 * LANES].astype(bf16)
            s = _dot_nt(q8, k) + bias_fn(kt)
            if first:
                m, l, acc = _softmax_first2(s, v)
            else:
                m, l, acc = _softmax_step2(s, m_scr[...], l_scr[...], acc_scr[...], v)
            m_scr[...] = m
            l_scr[...] = l
            acc_scr[...] = acc

        def body(kt, carry):
            tile(kt, False)
            return carry

        tile(kt_lo, True)
        lax.fori_loop(kt_lo + 1, kt_hi, body, 0)
        return _softmax_finish(l_scr[...], acc_scr[...])

    kt_end = ((qi + 1) * tq - 1) // tk + 1
    o_slc = flash(slc_ref, 0, kt_end, slc_bias)
    o_win = flash(win_ref, jnp.maximum(qi * tq - WINDOW, 0) // tk, kt_end, win_bias)

    heads = []
    for g in range(H_A):
        rows = slice(g * tq, (g + 1) * tq)
        o = (misc[:, 3 * g:3 * g + 1] * o_cmp[rows] + misc[:, 3 * g + 1:3 * g + 2] * o_slc[rows]
             + misc[:, 3 * g + 2:3 * g + 3] * o_win[rows])
        heads.append(o if (g % 2) == g // GQA_A else pltpu.roll(o, HD, 1))
    for u in range(H_A // 2):
        o_ref[0, :, u * LANES:(u + 1) * LANES] = jnp.where(lo, heads[2 * u], heads[2 * u + 1])


def _nsa_prompt(qal, kvb, slc, win, misc3, consts, b, t, tq):
    nb = t // L_CMP
    tk = 2 * LANES
    return pl.pallas_call(
        _nsa_prompt_kernel,
        grid=(b, t // tq),
        in_specs=[pl.BlockSpec((1, tq, H_A * LANES), lambda i, j: (i, j, 0)),
                  pl.BlockSpec((1, nb, 2 * LANES), lambda i, j: (i, 0, 0)),
                  pl.BlockSpec((1, t, 2 * LANES), lambda i, j: (i, 0, 0)),
                  pl.BlockSpec((1, t, 2 * LANES), lambda i, j: (i, 0, 0)),
                  pl.BlockSpec((1, tq, LANES), lambda i, j: (i, j, 0)),
                  pl.BlockSpec((t // tk, LANES, tk), lambda i, j: (0, 0, 0)),
                  pl.BlockSpec((1, LANES), lambda i, j: (0, 0))],
        out_specs=pl.BlockSpec((1, tq, H_A * HD), lambda i, j: (i, j, 0)),
        out_shape=jax.ShapeDtypeStruct((b, t, H_A * HD), f32),
        scratch_shapes=[pltpu.VMEM((H_A * tq, LANES), f32), pltpu.VMEM((H_A * tq, LANES), f32),
                        pltpu.VMEM((H_A * tq, LANES), f32)],
        compiler_params=_cparams(("parallel", "parallel")),
        name="nsa_prompt",
    )(qal, kvb, slc, win, misc3, consts["expand"], consts["blkend"])


def _rope_tables(pos):
    half = HD // 2
    inv = ROPE_THETA ** (-jnp.arange(half, dtype=f32) / half)
    ang = pos.astype(f32)[:, None] * inv[None, :]
    cos, sin = jnp.cos(ang), jnp.sin(ang)
    return (jnp.concatenate([cos, cos, cos, cos], axis=-1),
            jnp.concatenate([-sin, sin, -sin, sin], axis=-1))


def _constants(t, tq):
    lane = np.arange(LANES)
    bd = ((lane[:, None] // HD) == (lane[None, :] // HD)).astype(np.float32) / HD
    ltri = (lane[None, :] <= lane[:, None]).astype(np.float32)
    key = np.arange(t).reshape(t // tq, 1, tq)
    expand = ((lane[None, :, None] % 2 == 0) & (key // L_SEL == lane[None, :, None] // 2)).astype(np.float32)
    blkend = np.where(lane < t // L_CMP, (lane + 1) * L_CMP - 1, 2 ** 30).astype(np.int32)[None, :]
    log_g = jnp.log1p(-jnp.exp2(-5.0 - jnp.arange(H_C, dtype=f32)))
    i = jnp.arange(RET_CHUNK, dtype=f32)
    diff = i[:, None] - i[None, :]
    dmask = jnp.where(diff >= 0, jnp.exp(jnp.maximum(diff, 0.0) * log_g[:, None, None]), 0.0)
    xi = jnp.exp((i + 1.0)[None, :] * log_g[:, None])
    zeta = jnp.exp((RET_CHUNK - 1.0 - i)[None, :] * log_g[:, None])
    g_c = jnp.exp(RET_CHUNK * log_g)
    by_lane = lambda a: jnp.repeat(a.reshape(2, 2, -1).transpose(0, 2, 1), HD, axis=-1)
    return {"bd": jnp.asarray(bd, bf16), "pool8": _pool_matrix(8, 2 * PAGE),
            "ltri": jnp.asarray(ltri, bf16),
            "expand": jnp.asarray(expand, bf16), "blkend": jnp.asarray(blkend),
            "ret_dmask": dmask, "ret_xi": by_lane(xi), "ret_zeta": by_lane(zeta),
            "ret_gc": by_lane(g_c[:, None]), "log_g": log_g}


def _pack_w_in(wt):
    off = np.cumsum([0, 512, 128, 128, 128, 128, 128, 128, 24, 512, 256, 256, 256, 4, 256, 256, 256, 256, 256])
    seg = lambda i: wt[off[i]:off[i + 1], :]
    (a_q, a_kc, a_vc, a_ks, a_vs, a_kw, a_vw, a_g, a_gate,
     b_q, b_k, b_v, b_f, b_gate, c_q, c_k, c_v, c_gate) = [seg(i) for i in range(18)]
    pad = jnp.zeros((LANES - MISC_G - H_B, wt.shape[1]), wt.dtype)
    return jnp.concatenate([a_q, a_kc, a_vc, a_ks, a_vs, a_kw, a_vw, a_g, b_f, pad,
                            a_gate, b_gate, c_gate, b_q, b_k, b_v, c_q, c_k, c_v], axis=0).astype(bf16)


def _layer_params(l, consts, norm_w, w_in, fox_bf, nsa_q_norm, nsa_k_norm, fox_q_norm, fox_k_norm,
                  ret_gn_w, w_out):
    two = lambda g: jnp.concatenate([g, g])[None, :]
    gains = jnp.concatenate([two(nsa_q_norm[l]), two(nsa_k_norm[l, 0]), two(nsa_k_norm[l, 1]),
                             two(nsa_k_norm[l, 2]), two(fox_q_norm[l]), two(fox_k_norm[l]),
                             jnp.zeros((2, LANES), f32)], axis=0)
    bf = jnp.zeros((1, LANES), f32).at[0, MISC_G:MISC_G + H_B].set(fox_bf[l])
    lw = dict(consts)
    lw.update(norm_w=norm_w[l][None, :], w_pack=_pack_w_in(jnp.transpose(w_in, (2, 0, 1))[:, l, :]), gains=gains, bf=bf,
              ret_gn_w=ret_gn_w[l][None, :], w_out=w_out[l].astype(bf16))
    return lw


def _prompt_layer(x, lw, consts, cos_t, sin_t):
    b, t, d = x.shape
    tq = 2 * LANES
    x2 = x.reshape(b * t, d)
    qal, cmp_, slc, win, misc, gate, bqal, fkv, cqkv, kvb = _proj(x2, lw, cos_t, sin_t, 256)
    r3 = lambda a: a.reshape(b, t, a.shape[-1])
    misc3 = r3(misc)
    o_a = _nsa_prompt(r3(qal), kvb.reshape(b, t // L_CMP, 2 * LANES), r3(slc), r3(win), misc3,
                      consts, b, t, tq)
    ccol, crow = _fox_prep(misc3, consts["ltri"])
    o_b = _fox_prompt(r3(bqal), r3(fkv), ccol, crow, b, t)
    o_c, r_new = _ret_prompt(r3(cqkv), lw, b, t)
    y = _merge(x2, o_a.reshape(b * t, -1), o_b.reshape(b * t, -1), o_c.reshape(b * t, -1), gate,
               lw["w_out"], 256)
    wk = min(WINDOW, t)
    states = (cmp_.reshape(b, t, 2, KV_A, HD), slc.reshape(b, t, 2, KV_A, HD),
              fkv.reshape(b, t, 2, H_B, HD), misc3[:, :, MISC_G:MISC_G + H_B],
              r3(win)[:, t - wk:].reshape(b, wk, 2, KV_A, HD), r_new)
    return y.reshape(b, t, d), states


def _head_col(vec_row, shape, base, stride):
    pick = _lane(shape) == base + stride * _row(shape)
    return jnp.sum(jnp.where(pick, jnp.broadcast_to(vec_row, shape), 0.0), axis=1, keepdims=True)


def _seg_mean(x, seg):
    lane = _lane(x.shape)
    sh = seg // 2
    while sh >= 1:
        x = x + jnp.where((lane & sh) != 0, pltpu.roll(x, sh, 1), pltpu.roll(x, LANES - sh, 1))
        sh //= 2
    return x * (1.0 / seg)


def _tree(op, xs):
    xs = list(xs)
    while len(xs) > 1:
        xs = [op(xs[i], xs[i + 1]) if i + 1 < len(xs) else xs[i] for i in range(0, len(xs), 2)]
    return xs[0]


SEQ_PER_STEP = 2


def _seq_views(refs, s, lead_axis=0):
    one = pl.ds(s, 1)
    return [r.at[one] if lead_axis == 0 else r.at[:, one] for r in refs]


def _dec_nsa_kernel(pt_ref, q_ref, slc_new_ref, win_new_ref, misc_ref, win_ref, *rest, n_pages):
    o_ref = rest[2 * n_pages * SEQ_PER_STEP]
    for s in range(SEQ_PER_STEP):
        pages = rest[2 * n_pages * s:2 * n_pages * (s + 1)]
        q, slc_new, win_new, misc, o = _seq_views([q_ref, slc_new_ref, win_new_ref, misc_ref, o_ref], s)
        win, = _seq_views([win_ref], s, lead_axis=1)
        _dec_nsa_seq(q, slc_new, win_new, misc, win, pages[0:n_pages], pages[n_pages:], o, n_pages)


def _dec_nsa_seq(q_ref, slc_new_ref, win_new_ref, misc_ref, win_ref, cmp_pages, slc_pages, o_ref, n_pages):
    shape = (H_A, LANES)
    lane = _lane(shape)
    row = _row(shape)
    lo = lane < HD
    q8 = q_ref[0]
    qf = q8.astype(f32)
    past = n_pages * PAGE
    cur2 = 2 * (past // L_SEL)
    bf_round = lambda a: a.astype(bf16).astype(f32)
    k_of = lambda pg: pg[0, 0, 0:LANES, :].astype(bf16)
    v_of = lambda pg: pg[0, 0, LANES:2 * LANES, :].astype(bf16)
    row_max = lambda xs: jnp.max(_tree(jnp.maximum, xs), axis=1, keepdims=True)
    row_sum = lambda xs: jnp.sum(_tree(jnp.add, xs), axis=1, keepdims=True)

    sb = [_seg_mean(_dot(q8, k_of(pg)), L_CMP) for pg in cmp_pages]
    m = row_max(sb)
    p = [jnp.exp(x - m) * (1.0 / L_CMP) for x in sb]
    cl = row_sum(p)
    o_cmp = _tree(jnp.add, [_dot_nt(x.astype(bf16), v_of(pg)) for x, pg in zip(p, cmp_pages)]) / cl
    to_prob = L_CMP / cl
    imp2 = jnp.zeros(shape, f32)
    for page, x in enumerate(p):
        e = x * to_prob
        e = jnp.where(row < GQA_A,
                      jnp.sum(e[0:GQA_A], axis=0, keepdims=True),
                      jnp.sum(e[GQA_A:2 * GQA_A], axis=0, keepdims=True))
        pair = e + pltpu.roll(e, LANES - L_CMP, 1)
        first = pltpu.roll(pair, 4 * page, 1) if page else pair
        imp2 = jnp.where(lane == 4 * page, first, imp2)
        imp2 = jnp.where(lane == 4 * page + 2, pltpu.roll(pair, (4 * page + 2 + HD) % LANES, 1), imp2)
    sel = _select_pairs(imp2, cur2, _top_n_rank)

    k_new = bf_round(slc_new_ref[0, :, 0:LANES])
    v_new = bf_round(slc_new_ref[0, :, LANES:2 * LANES])
    s_new = jnp.sum(qf * k_new, axis=1, keepdims=True)
    s = []
    for page, pg in enumerate(slc_pages):
        hit = jnp.where(lo, sel[:, 4 * page:4 * page + 1], sel[:, 4 * page + 2:4 * page + 3]) > 0.5
        s.append(jnp.where(hit, _dot(q8, k_of(pg)), NEG))
    m = jnp.maximum(row_max(s), s_new)
    p = [jnp.exp(x - m) for x in s]
    p_new = jnp.exp(s_new - m)
    acc = _tree(jnp.add, [_dot_nt(x.astype(bf16), v_of(pg)) for x, pg in zip(p, slc_pages)])
    o_slc = (acc + p_new * v_new) / (row_sum(p) + p_new)

    wk = win_ref.shape[3]
    kw = win_ref[0, 0, 0:LANES, :].astype(bf16)
    vw = win_ref[0, 0, LANES:2 * LANES, :].astype(bf16)
    kw_new = bf_round(win_new_ref[0, :, 0:LANES])
    vw_new = bf_round(win_new_ref[0, :, LANES:2 * LANES])
    s = jnp.where(_lane((H_A, wk)) > wk - WINDOW, _dot(q8, kw), NEG)
    s_new = jnp.sum(qf * kw_new, axis=1, keepdims=True)
    m = jnp.maximum(jnp.max(s, axis=1, keepdims=True), s_new)
    pw = jnp.exp(s - m)
    pw_new = jnp.exp(s_new - m)
    o_win = ((_dot_nt(pw.astype(bf16), vw) + pw_new * vw_new)
             / (jnp.sum(pw, axis=1, keepdims=True) + pw_new))

    misc = misc_ref[0]
    o8 = (_head_col(misc, shape, 0, 3) * o_cmp + _head_col(misc, shape, 1, 3) * o_slc
          + _head_col(misc, shape, 2, 3) * o_win)
    o8r = pltpu.roll(o8, HD, 1)
    lo1 = _lane((1, LANES)) < HD
    for u in range(H_A // 2):
        kv_lo = (2 * u) // GQA_A == 0
        even = (o8 if kv_lo else o8r)[2 * u:2 * u + 1, :]
        odd = (o8r if kv_lo else o8)[2 * u + 1:2 * u + 2, :]
        o_ref[0, :, u * LANES:(u + 1) * LANES] = jnp.where(lo1, even, odd)


def _page_specs(block, layer, count, s):
    def spec(g):
        return pl.BlockSpec(block, lambda b, pt: (layer, pt[(b * SEQ_PER_STEP + s) * count + g])
                            + (0,) * (len(block) - 2))
    return [spec(g) for g in range(count)]


def _dec_nsa(pt_flat, q8, slc_new, win_new, misc, win_state, cache_cmp, cache_slc, layer):
    n_seq = q8.shape[0]
    n_pages = pt_flat.shape[0] // n_seq
    wk = win_state.shape[3]
    sub = SEQ_PER_STEP
    seq3 = lambda w: pl.BlockSpec((sub, 1, w), lambda b, pt: (b, 0, 0))
    page_block = (1, 1, 2 * LANES, PAGE)
    page_specs, page_args = [], []
    for s in range(sub):
        page_specs += _page_specs(page_block, layer, n_pages, s) + _page_specs(page_block, layer, n_pages, s)
        page_args += [cache_cmp] * n_pages + [cache_slc] * n_pages
    grid_spec = pltpu.PrefetchScalarGridSpec(
        num_scalar_prefetch=1,
        grid=(n_seq // sub,),
        in_specs=[pl.BlockSpec((sub, H_A, LANES), lambda b, pt: (b, 0, 0)),
                  seq3(2 * LANES), seq3(2 * LANES), seq3(LANES),
                  pl.BlockSpec((1, sub, 2 * LANES, wk), lambda b, pt: (layer, b, 0, 0))] + page_specs,
        out_specs=pl.BlockSpec((sub, 1, H_A * HD), lambda b, pt: (b, 0, 0)))
    return pl.pallas_call(
        functools.partial(_dec_nsa_kernel, n_pages=n_pages),
        grid_spec=grid_spec,
        out_shape=jax.ShapeDtypeStruct((n_seq, 1, H_A * HD), f32),
        compiler_params=_cparams(("parallel",)),
        name="dec_nsa",
    )(pt_flat, q8, slc_new, win_new, misc, win_state, *page_args)


def _lane_cumsum(x):
    lane = _lane(x.shape)
    sh = 1
    while sh < LANES:
        x = x + jnp.where(lane >= sh, pltpu.roll(x, sh, 1), 0.0)
        sh *= 2
    return x


def _dec_fox_kernel(pt_ref, q_ref, kv_new_ref, misc_ref, *rest, n_pages):
    o_ref = rest[2 * n_pages * SEQ_PER_STEP]
    for s in range(SEQ_PER_STEP):
        pages = rest[2 * n_pages * s:2 * n_pages * (s + 1)]
        q, kv_new, misc, o = _seq_views([q_ref, kv_new_ref, misc_ref, o_ref], s)
        _dec_fox_seq(q, kv_new, misc, pages[0:n_pages], pages[n_pages:], o)


def _dec_fox_seq(q_ref, kv_new_ref, misc_ref, kv_pages, lf_pages, o_ref):
    shape = (8, LANES)
    first_pair = _row(shape) < 2
    q8 = q_ref[0]
    unit = lambda kv, u: kv[0, 0, u * LANES:(u + 1) * LANES, :].astype(bf16)

    cs = [_lane_cumsum(lf[0, 0]) for lf in lf_pages]
    carry = jnp.zeros((8, 1), f32)
    s = []
    for kv, c in zip(kv_pages, cs):
        s.append(jnp.where(first_pair, _dot(q8, unit(kv, 0)), _dot(q8, unit(kv, 1))) - (carry + c))
        carry = carry + c[:, LANES - 1:LANES]
    bf_round = lambda a: jnp.broadcast_to(a.astype(bf16).astype(f32), shape)
    new = kv_new_ref[0]
    k_new = jnp.where(first_pair, bf_round(new[:, 0:LANES]), bf_round(new[:, LANES:2 * LANES]))
    v_new = jnp.where(first_pair, bf_round(new[:, 2 * LANES:3 * LANES]), bf_round(new[:, 3 * LANES:4 * LANES]))
    s_new = (jnp.sum(q8.astype(f32) * k_new, axis=1, keepdims=True)
             - (carry + _head_col(misc_ref[0], shape, MISC_G, 1)))
    m = jnp.maximum(jnp.max(_tree(jnp.maximum, s), axis=1, keepdims=True), s_new)
    p = [jnp.exp(x - m) for x in s]
    p_new = jnp.exp(s_new - m)
    den = jnp.sum(_tree(jnp.add, p), axis=1, keepdims=True) + p_new
    pb = [x.astype(bf16) for x in p]
    acc0 = _tree(jnp.add, [_dot_nt(x, unit(kv, 2)) for x, kv in zip(pb, kv_pages)])
    acc1 = _tree(jnp.add, [_dot_nt(x, unit(kv, 3)) for x, kv in zip(pb, kv_pages)])
    o8 = (jnp.where(first_pair, acc0, acc1) + p_new * v_new) / den
    lo1 = _lane((1, LANES)) < HD
    for u in range(H_B // 2):
        o_ref[0, :, u * LANES:(u + 1) * LANES] = jnp.where(lo1, o8[2 * u:2 * u + 1, :],
                                                           o8[2 * u + 1:2 * u + 2, :])


def _dec_fox(pt_flat, qb4, fkv_new, misc, cache_fkv, cache_lft, layer):
    n_seq = qb4.shape[0]
    n_pages = pt_flat.shape[0] // n_seq
    sub = SEQ_PER_STEP
    seq3 = lambda w: pl.BlockSpec((sub, 1, w), lambda b, pt: (b, 0, 0))
    page_specs, page_args = [], []
    for s in range(sub):
        page_specs += (_page_specs((1, 1, 4 * LANES, PAGE), layer, n_pages, s)
                       + _page_specs((1, 1, 8, PAGE), layer, n_pages, s))
        page_args += [cache_fkv] * n_pages + [cache_lft] * n_pages
    grid_spec = pltpu.PrefetchScalarGridSpec(
        num_scalar_prefetch=1,
        grid=(n_seq // sub,),
        in_specs=[pl.BlockSpec((sub, 8, LANES), lambda b, pt: (b, 0, 0)), seq3(4 * LANES), seq3(LANES)]
        + page_specs,
        out_specs=pl.BlockSpec((sub, 1, H_B * HD), lambda b, pt: (b, 0, 0)))
    return pl.pallas_call(
        functools.partial(_dec_fox_kernel, n_pages=n_pages),
        grid_spec=grid_spec,
        out_shape=jax.ShapeDtypeStruct((n_seq, 1, H_B * HD), f32),
        compiler_params=_cparams(("parallel",)),
        name="dec_fox",
    )(pt_flat, qb4, fkv_new, misc, *page_args)


def _dec_ret_kernel(q_ref, k_ref, v_ref, r_ref, gam_ref, gnw_ref, o_ref, rn_ref):
    q = q_ref[0].astype(f32)
    k = k_ref[0].astype(f32)
    v = v_ref[0].astype(f32)
    qk = jnp.sum(q * k, axis=1, keepdims=True)
    eye = _row((HD, HD)) == _lane((HD, HD))
    col = lambda a: jnp.sum(jnp.where(eye, jnp.broadcast_to(a, (HD, HD)), 0.0), axis=1, keepdims=True)
    for h in range(H_C):
        hs = slice(h, h + 1)
        r = r_ref[0, 0, h]
        gam = gam_ref[hs, :]
        o = qk[hs, :] * v[hs, :] + jnp.sum(col(q[hs, :]) * r, axis=0, keepdims=True) * gam
        rn_ref[0, h] = r * gam + col(k[hs, :]) * v[hs, :]
        d = o - jnp.mean(o, axis=1, keepdims=True)
        var = jnp.mean(d * d, axis=1, keepdims=True)
        o_ref[0, hs, :] = d * lax.rsqrt(var + EPS) * gnw_ref[hs, :]


def _dec_ret(cq, ck, cv, state_ret, gam, gnw, layer):
    n_seq = cq.shape[0]
    head = pl.BlockSpec((1, H_C, HD), lambda b: (b, 0, 0))
    table = pl.BlockSpec((H_C, HD), lambda b: (0, 0))
    return pl.pallas_call(
        _dec_ret_kernel,
        grid=(n_seq,),
        in_specs=[head, head, head,
                  pl.BlockSpec((1, 1, H_C, HD, HD), lambda b: (layer, b, 0, 0, 0)), table, table],
        out_specs=[head, pl.BlockSpec((1, H_C, HD, HD), lambda b: (b, 0, 0, 0))],
        out_shape=[jax.ShapeDtypeStruct((n_seq, H_C, HD), f32),
                   jax.ShapeDtypeStruct((n_seq, H_C, HD, HD), f32)],
        compiler_params=_cparams(("parallel",)),
        name="dec_ret",
    )(cq, ck, cv, state_ret, gam, gnw)


def _win_shift_kernel(win_ref, new_ref, o_ref):
    n_sub, feat, wk = win_ref.shape[1:]
    sq = (feat, feat)
    eye = _row(sq) == _lane(sq)
    last = _lane((feat, wk)) == wk - 1
    for s in range(n_sub):
        col = jnp.sum(jnp.where(eye, jnp.broadcast_to(new_ref[0, s], sq), 0.0), axis=1, keepdims=True)
        o_ref[0, s] = jnp.where(last, col, pltpu.roll(win_ref[0, s], wk - 1, 1))


def _win_shift(win_t, new_rows):
    depth, n_seq, feat, wk = win_t.shape
    n_sub = 8
    return pl.pallas_call(
        _win_shift_kernel,
        grid=(depth, n_seq // n_sub),
        in_specs=[pl.BlockSpec((1, n_sub, feat, wk), lambda l, i: (l, i, 0, 0)),
                  pl.BlockSpec((1, n_sub, 1, feat), lambda l, i: (l, i, 0, 0))],
        out_specs=pl.BlockSpec((1, n_sub, feat, wk), lambda l, i: (l, i, 0, 0)),
        out_shape=jax.ShapeDtypeStruct(win_t.shape, f32),
        compiler_params=_cparams(("parallel", "parallel")),
        name="win_shift",
    )(win_t, new_rows)


def _sample_layer(x, layer, lw, consts, caches, pt_flat, cos_t, sin_t):
    n, _, d = x.shape
    x2 = x.reshape(n, d)
    qal, cmp_, slc, win, misc, gate, bqal, fkv, cqkv, _ = _proj(x2, lw, cos_t, sin_t, n)
    row3 = lambda a: a[:, None, :]
    o_a8 = _dec_nsa(pt_flat, qal.reshape(n, H_A, LANES), row3(slc), row3(win), row3(misc),
                    caches["win"], caches["cmp"], caches["slc"], layer)
    qb8 = jnp.pad(bqal.reshape(n, H_B, LANES), ((0, 0), (0, 8 - H_B), (0, 0)))
    o_b8 = _dec_fox(pt_flat, qb8, row3(fkv), row3(misc),
                    caches["fkv"], caches["lft"], layer)
    heads = lambda u: cqkv[:, 2 * u * LANES:(2 * u + 2) * LANES].reshape(n, H_C, HD).astype(f32)
    gam = jnp.broadcast_to(jnp.exp(consts["log_g"])[:, None], (H_C, HD))
    o_c, r_new = _dec_ret(heads(0), heads(1), heads(2), caches["ret"], gam,
                          lw["ret_gn_w"].reshape(H_C, HD), layer)
    y = _merge(x2, o_a8.reshape(n, H_A * HD), o_b8.reshape(n, H_B * HD), o_c.reshape(n, H_C * HD), gate,
               lw["w_out"], n)
    states = (cmp_.reshape(n, 1, 2, KV_A, HD), slc.reshape(n, 1, 2, KV_A, HD),
              fkv.reshape(n, 1, 2, H_B, HD), misc[:, None, MISC_G:MISC_G + H_B], win, r_new)
    return y.reshape(n, 1, d), states


def kernel(x_prompt, x_sample, cache_nsa_cmp, cache_nsa_slc, cache_fox_kv, cache_fox_logf,
           state_nsa_win, state_ret, page_table, norm_w, w_in, fox_bf, nsa_q_norm, nsa_k_norm,
           fox_q_norm, fox_k_norm, ret_gn_w, w_out):
    depth = norm_w.shape[0]
    b, t, _ = x_prompt.shape
    n_seq, n_pages = page_table.shape
    assert x_sample.shape[1] == 1 and t % 256 == 0 and n_pages % 2 == 0
    past = n_pages * PAGE
    consts = _constants(t, 2 * LANES)
    cos_p, sin_p = _rope_tables(jnp.arange(t, dtype=jnp.int32))
    cos_s, sin_s = _rope_tables(jnp.full((n_seq,), past, dtype=jnp.int32))
    n_phys = cache_nsa_cmp.shape[1]
    feat_major = lambda a, lead: jnp.transpose(a, (0, 1, 3, 4, 5, 2)).reshape(depth, lead, -1, a.shape[2])
    caches = {"cmp": feat_major(cache_nsa_cmp, n_phys), "slc": feat_major(cache_nsa_slc, n_phys),
              "fkv": feat_major(cache_fox_kv, n_phys),
              "lft": jnp.pad(jnp.swapaxes(cache_fox_logf, 2, 3), ((0, 0), (0, 0), (0, 8 - H_B), (0, 0))),
              "win": feat_major(state_nsa_win, n_seq),
              "ret": state_ret}
    pt_flat = page_table.reshape(-1)
    yp, ys = x_prompt, x_sample
    p_states, s_states = [], []
    for l in range(depth):
        lw = _layer_params(l, consts, norm_w, w_in, fox_bf, nsa_q_norm, nsa_k_norm, fox_q_norm,
                           fox_k_norm, ret_gn_w, w_out)
        yp, ps = _prompt_layer(yp, lw, consts, cos_p, sin_p)
        ys, ss = _sample_layer(ys, l, lw, consts, caches, pt_flat, cos_s, sin_s)
        p_states.append(ps)
        s_states.append(ss)
    stk = lambda states, i: jnp.stack([st[i] for st in states], axis=0)
    win_t = caches["win"]
    wk = win_t.shape[3]
    new_win_t = _win_shift(win_t, stk(s_states, 4)[:, :, None, :])
    new_win = jnp.transpose(new_win_t.reshape(depth, n_seq, 2, KV_A, HD, wk), (0, 1, 5, 2, 3, 4))
    out = [yp, ys]
    for i in range(6):
        out += [stk(p_states, i), new_win if i == 4 else stk(s_states, i)]
    return tuple(out)
```
